```python
import jax
import jax.numpy as jnp
from jax import lax
import numpy as np

D_MODEL = 2048
BATCH = 1
SEQ = 8192
DEPTH = 4

GRID_W = 64
CTX_LEN = 256
MIX_W = D_MODEL
ATTN_W = MIX_W // 2
CONV_W = MIX_W - ATTN_W
HEAD_DIM = 64
N_HEADS = ATTN_W // HEAD_DIM
N_KV_HEADS = 2
KV_GROUP = N_HEADS // N_KV_HEADS
KV_W = N_KV_HEADS * HEAD_DIM
IN_COLS = ATTN_W + 2 * KV_W + 3 * CONV_W
WINDOW = 128
ATTN_BLOCK = 128
CONV_SIZE = 3
ROPE_THETA = 10000.0
ROPE_AXIS_DIM = HEAD_DIM // 2
N_EXPERTS = 16
N_GROUPS = 4
EXPERTS_PER_GROUP = N_EXPERTS // N_GROUPS
TOP_K = 2
D_FF_EXPERT = 1024
MOE_BLOCK = 128
EPS = 1e-6
NEG_INF = -1e30

kernel_name = 'hybrid_dit_conv_swa_grouped_moe'


def rmsnorm(x, g):
    xf = x.astype(jnp.float32)
    y = xf * lax.rsqrt(jnp.mean(xf * xf, axis=-1, keepdims=True) + EPS)
    return (y * g.astype(jnp.float32)).astype(x.dtype)


def modulate(h, shift, scale):
    return h * (1 + scale) + shift


def rope_axis(x, ang):
    cos = jnp.cos(ang).astype(x.dtype)[None, :, None, :]
    sin = jnp.sin(ang).astype(x.dtype)[None, :, None, :]
    x1, x2 = jnp.split(x, 2, axis=-1)
    return jnp.concatenate([x1 * cos - x2 * sin, x2 * cos + x1 * sin], axis=-1)


def rope_2d(x, ang_row, ang_col):
    return jnp.concatenate([rope_axis(x[..., :ROPE_AXIS_DIM], ang_row),
                            rope_axis(x[..., ROPE_AXIS_DIM:], ang_col)], axis=-1)


def dwconv3(u, w):
    L = u.shape[1]
    up = jnp.pad(u, ((0, 0), (1, 1), (0, 0)))
    return w[0] * up[:, :L] + w[1] * up[:, 1:L + 1] + w[2] * up[:, 2:L + 2]


def split_in(p):
    o = ATTN_W + 2 * KV_W
    q = p[..., :ATTN_W]
    k = p[..., ATTN_W:ATTN_W + KV_W]
    v = p[..., ATTN_W + KV_W:o]
    u = p[..., o:o + CONV_W]
    b_gate = p[..., o + CONV_W:o + 2 * CONV_W]
    c_gate = p[..., o + 2 * CONV_W:]
    return q, k, v, u, b_gate, c_gate


def window_attention(q, k, v, k_ctx, v_ctx, sink):
    B, L = q.shape[:2]
    C = k_ctx.shape[1]
    nb = L // ATTN_BLOCK
    nw = 3 * ATTN_BLOCK
    qb = q.reshape(B, nb, ATTN_BLOCK, N_KV_HEADS, KV_GROUP, HEAD_DIM)

    def band(t):
        tb = jnp.pad(t, ((0, 0), (ATTN_BLOCK, ATTN_BLOCK), (0, 0), (0, 0)))
        tb = tb.reshape(B, nb + 2, ATTN_BLOCK, N_KV_HEADS, HEAD_DIM)
        return jnp.concatenate([tb[:, :-2], tb[:, 1:-1], tb[:, 2:]], axis=2)

    k_win, v_win = band(k), band(v)
    scale = HEAD_DIM ** -0.5
    s_win = jnp.einsum('bnqhgd,bnkhd->bhgnqk', qb, k_win).astype(jnp.float32) * scale
    s_ctx = jnp.einsum('bnqhgd,bchd->bhgnqc', qb, k_ctx).astype(jnp.float32) * scale
    qi = jnp.arange(ATTN_BLOCK)[:, None]
    kj = jnp.arange(nw)[None, :]
    in_window = jnp.abs(qi + ATTN_BLOCK - kj) <= WINDOW
    key_pos = jnp.arange(nb)[:, None] * ATTN_BLOCK - ATTN_BLOCK + kj
    in_seq = (key_pos >= 0) & (key_pos < L)
    mask = in_window[None] & in_seq[:, None, :]
    s_win = jnp.where(mask, s_win, NEG_INF)
    s_sink = jnp.broadcast_to(sink.astype(jnp.float32).reshape(1, N_KV_HEADS, KV_GROUP, 1, 1, 1),
                              s_win.shape[:-1] + (1,))
    p = jax.nn.softmax(jnp.concatenate([s_win, s_ctx, s_sink], axis=-1), axis=-1).astype(v.dtype)
    out = (jnp.einsum('bhgnqk,bnkhd->bnqhgd', p[..., :nw], v_win)
           + jnp.einsum('bhgnqc,bchd->bnqhgd', p[..., nw:nw + C], v_ctx))
    return out.reshape(B, L, ATTN_W)


def context_attention(q, k, v, sink):
    B, C = q.shape[:2]
    qg = q.reshape(B, C, N_KV_HEADS, KV_GROUP, HEAD_DIM)
    s = jnp.einsum('bqhgd,bkhd->bhgqk', qg, k).astype(jnp.float32) * (HEAD_DIM ** -0.5)
    s_sink = jnp.broadcast_to(sink.astype(jnp.float32).reshape(1, N_KV_HEADS, KV_GROUP, 1, 1),
                              s.shape[:-1] + (1,))
    p = jax.nn.softmax(jnp.concatenate([s, s_sink], axis=-1), axis=-1).astype(v.dtype)
    out = jnp.einsum('bhgqk,bkhd->bqhgd', p[..., :C], v)
    return out.reshape(B, C, ATTN_W)


def grouped_moe(h, w_router, router_bias, w_gate, w_up, w_down):
    T, D = h.shape
    logits = jnp.dot(h.astype(jnp.float32), w_router.astype(jnp.float32))
    scores = jax.nn.sigmoid(logits)
    sel = (scores + router_bias.astype(jnp.float32)).reshape(T, N_GROUPS, EXPERTS_PER_GROUP)
    group_score = lax.top_k(sel, TOP_K)[0].sum(-1)
    g_idx = jnp.argmax(group_score, axis=-1).astype(jnp.int32)
    in_group = jnp.take_along_axis(sel, g_idx[:, None, None], axis=1)[:, 0]
    _, local = lax.top_k(in_group, TOP_K)
    expert_idx = g_idx[:, None] * EXPERTS_PER_GROUP + local.astype(jnp.int32)
    sel_scores = jnp.take_along_axis(scores, expert_idx, axis=1)
    gates = sel_scores / jnp.sum(sel_scores, axis=-1, keepdims=True)

    A = T * TOP_K
    flat_e = expert_idx.reshape(A)
    flat_tok = jnp.arange(A, dtype=jnp.int32) // TOP_K
    order = jnp.argsort(flat_e)
    sorted_e = flat_e[order]
    counts = jnp.bincount(flat_e, length=N_EXPERTS).astype(jnp.int32)
    padded = (counts + MOE_BLOCK - 1) // MOE_BLOCK * MOE_BLOCK
    start = jnp.cumsum(counts) - counts
    pend = jnp.cumsum(padded)
    pstart = pend - padded
    dest_sorted = pstart[sorted_e] + (jnp.arange(A, dtype=jnp.int32) - start[sorted_e])
    nblk = -(-(A + N_EXPERTS * (MOE_BLOCK - 1)) // MOE_BLOCK)
    row_tok = jnp.full((nblk * MOE_BLOCK,), T, dtype=jnp.int32).at[dest_sorted].set(flat_tok[order])
    block_expert = jnp.clip(jnp.searchsorted(pend, jnp.arange(nblk, dtype=jnp.int32) * MOE_BLOCK,
                                             side='right'), 0, N_EXPERTS - 1).astype(jnp.int32)
    h_pad = jnp.concatenate([h, jnp.zeros((1, D), h.dtype)], axis=0)

    def expert_block(args):
        tok, e = args
        xb = h_pad[tok]
        a = jnp.dot(xb, w_gate[e])
        b = jnp.dot(xb, w_up[e])
        return jnp.dot(jax.nn.silu(a) * b, w_down[e])

    y_rows = lax.map(expert_block, (row_tok.reshape(nblk, MOE_BLOCK), block_expert))
    y_rows = y_rows.reshape(nblk * MOE_BLOCK, D)
    dest = jnp.zeros((A,), jnp.int32).at[order].set(dest_sorted)
    y = y_rows[dest].reshape(T, TOP_K, D) * gates[..., None].astype(h.dtype)
    return jnp.sum(y, axis=1)


def setup_inputs(seed: int = 0) -> dict:
    key = jax.random.key(seed)
    ks = jax.random.split(key, 24)
    f32 = jnp.float32

    def nrm(k, shape, s):
        return jax.random.normal(k, shape, f32) * s

    D = D_MODEL
    return {
        'x': nrm(ks[0], (BATCH, SEQ, D), 1.0),
        'c': nrm(ks[1], (BATCH, D), 1.0),
        'ctx': nrm(ks[2], (BATCH, CTX_LEN, D), 1.0),
        'c_ctx': nrm(ks[3], (D,), 1.0),
        'w_ada': nrm(ks[4], (DEPTH, D, 6 * D), 0.2 * D ** -0.5),
        'b_ada': nrm(ks[5], (DEPTH, 6 * D), 0.02),
        'g_attn': 1.0 + nrm(ks[6], (DEPTH, D), 0.02),
        'w_in': nrm(ks[7], (DEPTH, D, IN_COLS), D ** -0.5),
        'q_norm_g': 1.0 + nrm(ks[8], (DEPTH, HEAD_DIM), 0.02),
        'k_norm_g': 1.0 + nrm(ks[9], (DEPTH, HEAD_DIM), 0.02),
        'sink': nrm(ks[10], (DEPTH, N_HEADS), 0.5),
        'conv_w': nrm(ks[11], (DEPTH, CONV_SIZE, CONV_W), CONV_SIZE ** -0.5),
        'g_out_attn': 1.0 + nrm(ks[12], (DEPTH, ATTN_W), 0.02),
        'g_out_conv': 1.0 + nrm(ks[13], (DEPTH, CONV_W), 0.02),
        'w_out': nrm(ks[14], (DEPTH, MIX_W, D), MIX_W ** -0.5),
        'g_ffn': 1.0 + nrm(ks[15], (DEPTH, D), 0.02),
        'w_router': nrm(ks[16], (D, N_EXPERTS), D ** -0.5),
        'router_bias': nrm(ks[17], (N_EXPERTS,), 0.01),
        'w_exp_gate': nrm(ks[18], (DEPTH, N_EXPERTS, D, D_FF_EXPERT), D ** -0.5),
        'w_exp_up': nrm(ks[19], (DEPTH, N_EXPERTS, D, D_FF_EXPERT), D ** -0.5),
        'w_exp_down': nrm(ks[20], (DEPTH, N_EXPERTS, D_FF_EXPERT, D), D_FF_EXPERT ** -0.5),
    }


def reference(x, c, ctx, c_ctx, w_ada, b_ada, g_attn, w_in, q_norm_g, k_norm_g, sink, conv_w,
              g_out_attn, g_out_conv, w_out, g_ffn, w_router, router_bias,
              w_exp_gate, w_exp_up, w_exp_down):
    B, L, D = x.shape
    C = ctx.shape[1]
    rows = L // GRID_W
    row_id = jnp.repeat(jnp.arange(rows, dtype=jnp.float32), GRID_W)
    col_id = jnp.tile(jnp.arange(GRID_W, dtype=jnp.float32), rows)
    inv_freq = ROPE_THETA ** (-jnp.arange(0, ROPE_AXIS_DIM, 2, dtype=jnp.float32) / ROPE_AXIS_DIM)
    ang_row = row_id[:, None] * inv_freq[None, :]
    ang_col = col_id[:, None] * inv_freq[None, :]
    silu_c = jax.nn.silu(c)
    silu_cc = jax.nn.silu(c_ctx)
    xc = ctx

    for l in range(DEPTH):
        last = l == DEPTH - 1
        mod = (jnp.dot(silu_c, w_ada[l]) + b_ada[l]).reshape(B, 6, 1, D)
        mod_c = (jnp.dot(silu_cc, w_ada[l]) + b_ada[l]).reshape(6, D)

        h = modulate(rmsnorm(x, g_attn[l]), mod[:, 0], mod[:, 1])
        hc = modulate(rmsnorm(xc, g_attn[l]), mod_c[0], mod_c[1])
        q, k, v, u, b_gate, c_gate = split_in(jnp.dot(h, w_in[l]))
        if last:
            kv_c = jnp.dot(hc, w_in[l][:, ATTN_W:ATTN_W + 2 * KV_W])
            k_c, v_c = kv_c[..., :KV_W], kv_c[..., KV_W:]
        else:
            q_c, k_c, v_c, u_c, b_gate_c, c_gate_c = split_in(jnp.dot(hc, w_in[l]))
        k_c = rmsnorm(k_c.reshape(B, C, N_KV_HEADS, HEAD_DIM), k_norm_g[l])
        v_c = v_c.reshape(B, C, N_KV_HEADS, HEAD_DIM)
        q = rope_2d(rmsnorm(q.reshape(B, L, N_HEADS, HEAD_DIM), q_norm_g[l]), ang_row, ang_col)
        k = rope_2d(rmsnorm(k.reshape(B, L, N_KV_HEADS, HEAD_DIM), k_norm_g[l]), ang_row, ang_col)
        v = v.reshape(B, L, N_KV_HEADS, HEAD_DIM)
        attn = window_attention(q, k, v, k_c, v_c, sink[l])
        conv = b_gate * dwconv3(c_gate * u, conv_w[l])
        mixed = jnp.dot(jnp.concatenate([rmsnorm(attn, g_out_attn[l]),
                                         rmsnorm(conv, g_out_conv[l])], axis=-1), w_out[l])
        x = x + mod[:, 2] * mixed
        if not last:
            q_c = rmsnorm(q_c.reshape(B, C, N_HEADS, HEAD_DIM), q_norm_g[l])
            attn_c = context_attention(q_c, k_c, v_c, sink[l])
            conv_c = b_gate_c * dwconv3(c_gate_c * u_c, conv_w[l])
            mixed_c = jnp.dot(jnp.concatenate([rmsnorm(attn_c, g_out_attn[l]),
                                               rmsnorm(conv_c, g_out_conv[l])], axis=-1), w_out[l])
            xc = xc + mod_c[2] * mixed_c

        h2 = modulate(rmsnorm(x, g_ffn[l]), mod[:, 3], mod[:, 4])
        if last:
            y = grouped_moe(h2.reshape(B * L, D), w_router, router_bias,
                            w_exp_gate[l], w_exp_up[l], w_exp_down[l])
            x = x + mod[:, 5] * y.reshape(B, L, D)
        else:
            h2c = modulate(rmsnorm(xc, g_ffn[l]), mod_c[3], mod_c[4])
            tokens = jnp.concatenate([h2c.reshape(B * C, D), h2.reshape(B * L, D)], axis=0)
            y = grouped_moe(tokens, w_router, router_bias,
                            w_exp_gate[l], w_exp_up[l], w_exp_down[l])
            xc = xc + mod_c[5] * y[:B * C].reshape(B, C, D)
            x = x + mod[:, 5] * y[B * C:].reshape(B, L, D)
    return x
```

```python
import functools

import jax
import jax.numpy as jnp
from jax import lax
from jax.experimental import pallas as pl
from jax.experimental.pallas import tpu as pltpu

f32 = jnp.float32
bf16 = jnp.bfloat16
i32 = jnp.int32

D = 2048
DEPTH = 4
SEQ = 8192
CTX = 256
R = CTX + SEQ
GRID_W = 64
ATTN_W = 1024
CONV_W = 1024
HEAD_DIM = 64
N_HEADS = 16
KV_W = 128
IN_COLS = ATTN_W + 2 * KV_W + 3 * CONV_W
ABLK = 128
WINDOW = 128
N_EXPERTS = 16
N_GROUPS = 4
EPG = 4
D_FF = 1024
EPS = 1e-6
NEG_INF = -1e30
ROPE_THETA = 10000.0

LANES = 128
SUBLANES = 8
TM = 256
NT = R // TM
EBLK = 256
NBLK = -(-(2 * R + N_EXPERTS * (EBLK - 1)) // EBLK)
NROWS = NBLK * EBLK
RT = LANES
VMEM_LIMIT = 56 * 1024 * 1024


def _cparams(sem):
    return pltpu.CompilerParams(dimension_semantics=sem, vmem_limit_bytes=VMEM_LIMIT)


ADA_TK = 512
ADA_TN = 1536


def _adaln_kernel(s_ref, w_ref, b_ref, o_ref, acc_ref):
    k = pl.program_id(2)

    @pl.when(k == 0)
    def _():
        acc_ref[...] = jnp.zeros_like(acc_ref)

    for j in range(ADA_TN // LANES):
        w = w_ref[0, :, j * LANES:(j + 1) * LANES]
        for v in range(2):
            p = (w * s_ref[v]).reshape(ADA_TK // SUBLANES, SUBLANES, LANES).sum(axis=0)
            acc_ref[v, :, j * LANES:(j + 1) * LANES] += p

    @pl.when(k == pl.num_programs(2) - 1)
    def _():
        for v in range(2):
            o_ref[0, v:v + 1, :] = acc_ref[v].sum(axis=0, keepdims=True) + b_ref[0]


def _adaln(s_bcast, w_ada, b_ada):
    return pl.pallas_call(
        _adaln_kernel,
        grid=(DEPTH, 6 * D // ADA_TN, D // ADA_TK),
        in_specs=[pl.BlockSpec((2, ADA_TK, LANES), lambda l, n, k: (0, k, 0)),
                  pl.BlockSpec((1, ADA_TK, ADA_TN), lambda l, n, k: (l, k, n)),
                  pl.BlockSpec((1, 1, ADA_TN), lambda l, n, k: (l, 0, n))],
        out_specs=pl.BlockSpec((1, 2, ADA_TN), lambda l, n, k: (l, 0, n)),
        out_shape=jax.ShapeDtypeStruct((DEPTH, 2, 6 * D), f32),
        scratch_shapes=[pltpu.VMEM((2, SUBLANES, ADA_TN), f32)],
        compiler_params=_cparams(("arbitrary", "arbitrary", "arbitrary")),
        name="adaln",
    )(s_bcast, w_ada, b_ada.reshape(DEPTH, 1, 6 * D))


def _rms_scale(x):
    return lax.rsqrt(jnp.mean(x * x, axis=-1, keepdims=True) + EPS)


def _inproj_kernel(x_ref, ms_ref, w_ref, o_ref):
    stream = jnp.where(pl.program_id(0) == 0, 0, 1)
    x = x_ref[...]
    gain = ms_ref[stream, 0:1, :]
    shift = ms_ref[stream, 1:2, :]
    h = (x * _rms_scale(x) * gain + shift).astype(bf16)
    o_ref[...] = jnp.dot(h, w_ref[...], preferred_element_type=f32)


def _inproj(x, ms, w):
    return pl.pallas_call(
        _inproj_kernel,
        grid=(NT,),
        in_specs=[pl.BlockSpec((TM, D), lambda i: (i, 0)),
                  pl.BlockSpec((2, 2, D), lambda i: (0, 0, 0)),
                  pl.BlockSpec((D, IN_COLS), lambda i: (0, 0))],
        out_specs=pl.BlockSpec((TM, IN_COLS), lambda i: (i, 0)),
        out_shape=jax.ShapeDtypeStruct((R, IN_COLS), f32),
        compiler_params=_cparams(("arbitrary",)),
        name="inproj",
    )(x, ms, w)


COL_Q, COL_U, COL_B, COL_C = 0, 1, 2, 3
COL_K, COL_V = 32, 33


def _qkprep_kernel(q_ref, k_ref, v_ref, cos_ref, sa_ref, sb_ref, g_ref, ones_ref, qo_ref, ko_ref, vo_ref):
    cos, sa, sb = cos_ref[...], sa_ref[...], sb_ref[...]
    ones = ones_ref[...]

    def norm_rope(xc, g):
        sq = xc * xc
        hi = sq.astype(bf16)
        lo = (sq - hi.astype(f32)).astype(bf16)
        ss = (jnp.dot(hi, ones, preferred_element_type=f32)
              + jnp.dot(lo, ones, preferred_element_type=f32))
        y = xc * lax.rsqrt(ss * (1.0 / HEAD_DIM) + EPS) * g
        return y * cos + pltpu.roll(y, LANES - 16, 1) * sa + pltpu.roll(y, 16, 1) * sb

    for j in range(ATTN_W // LANES):
        sl = slice(j * LANES, (j + 1) * LANES)
        qo_ref[:, sl] = norm_rope(q_ref[:, sl], g_ref[0:1, :]).astype(bf16)
    ko_ref[...] = norm_rope(k_ref[...], g_ref[1:2, :]).astype(bf16)
    vo_ref[...] = v_ref[...].astype(bf16)


def _qkprep(p, cos, sa, sb, g2, ones_bd):
    row = lambda i: (i, 0)
    return pl.pallas_call(
        _qkprep_kernel,
        grid=(NT,),
        in_specs=[pl.BlockSpec((TM, ATTN_W), lambda i: (i, COL_Q)),
                  pl.BlockSpec((TM, KV_W), lambda i: (i, COL_K)),
                  pl.BlockSpec((TM, KV_W), lambda i: (i, COL_V)),
                  pl.BlockSpec((TM, LANES), row),
                  pl.BlockSpec((TM, LANES), row),
                  pl.BlockSpec((TM, LANES), row),
                  pl.BlockSpec((2, LANES), lambda i: (0, 0)),
                  pl.BlockSpec((LANES, LANES), lambda i: (0, 0))],
        out_specs=[pl.BlockSpec((TM, ATTN_W), row),
                   pl.BlockSpec((TM, KV_W), row),
                   pl.BlockSpec((TM, KV_W), row)],
        out_shape=[jax.ShapeDtypeStruct((R, ATTN_W), bf16),
                   jax.ShapeDtypeStruct((R, KV_W), bf16),
                   jax.ShapeDtypeStruct((R, KV_W), bf16)],
        compiler_params=_cparams(("arbitrary",)),
        name="qkprep",
    )(p, p, p, cos, sa, sb, g2, ones_bd)


NQB = R // ABLK
FIRST_LAT = CTX // ABLK
NKEYS = CTX + 3 * ABLK


def _attn_kernel(sink_ref, q_ref, kc_ref, kp_ref, kk_ref, kn_ref, vc_ref, vp_ref, vk_ref, vn_ref, o_ref):
    i = pl.program_id(0)
    kf = jnp.concatenate([kc_ref[...], kp_ref[...], kk_ref[...], kn_ref[...]], axis=0).astype(f32)
    vf = jnp.concatenate([vc_ref[...], vp_ref[...], vk_ref[...], vn_ref[...]], axis=0).astype(f32)
    lane = lax.broadcasted_iota(i32, (NKEYS, LANES), 1)
    low = lane < HEAD_DIM
    kr = pltpu.roll(kf, HEAD_DIM, 1)
    vr = pltpu.roll(vf, HEAD_DIM, 1)
    zero = jnp.zeros_like(kf)
    k_lo = (jnp.where(low, kf, zero), jnp.where(low, kr, zero))
    k_hi = (jnp.where(low, zero, kr), jnp.where(low, zero, kf))
    v_lo = (jnp.where(low, vf, zero), jnp.where(low, vr, zero))
    v_hi = (jnp.where(low, zero, vr), jnp.where(low, zero, vf))

    qi = lax.broadcasted_iota(i32, (ABLK, NKEYS), 0)
    col = lax.broadcasted_iota(i32, (ABLK, NKEYS), 1)
    kj = col - CTX
    in_window = jnp.abs(qi + ABLK - kj) <= WINDOW
    key_block = i - 1 + jnp.right_shift(kj, 7)
    first_ok = jnp.where(i >= FIRST_LAT, FIRST_LAT, NQB)
    mask = (col < CTX) | (in_window & (key_block >= first_ok) & (key_block < NQB))
    lane_q = lax.broadcasted_iota(i32, (ABLK, LANES), 1)

    contract_last = (((1,), (1,)), ((), ()))
    for h in range(2):
        qs = jnp.concatenate([q_ref[:, (h * 4 + p) * LANES:(h * 4 + p + 1) * LANES] for p in range(4)], axis=0)
        s_par = (lax.dot_general(qs, k_lo[h].astype(bf16), contract_last, preferred_element_type=f32),
                 lax.dot_general(qs, k_hi[h].astype(bf16), contract_last, preferred_element_type=f32))
        probs = ([], [])
        rden = ([], [])
        for par in range(2):
            for p in range(4):
                sink = sink_ref[h * 8 + 2 * p + par]
                s = jnp.where(mask, s_par[par][p * ABLK:(p + 1) * ABLK], NEG_INF)
                m = jnp.maximum(jnp.max(s, axis=-1, keepdims=True), sink)
                e = jnp.exp(s - m)
                den = jnp.sum(e, axis=-1, keepdims=True) + jnp.exp(sink - m)
                probs[par].append(e.astype(bf16))
                rden[par].append(1.0 / den)
        o = (jnp.dot(jnp.concatenate(probs[0], axis=0), v_lo[h].astype(bf16), preferred_element_type=f32)
             + jnp.dot(jnp.concatenate(probs[1], axis=0), v_hi[h].astype(bf16), preferred_element_type=f32))
        for p in range(4):
            scale = jnp.where(lane_q < HEAD_DIM, rden[0][p], rden[1][p])
            o_ref[:, (h * 4 + p) * LANES:(h * 4 + p + 1) * LANES] = o[p * ABLK:(p + 1) * ABLK] * scale


def _attention(sink, q, k, v):
    clip = lambda b: jnp.clip(b, FIRST_LAT, NQB - 1)
    kv_specs = [pl.BlockSpec((CTX, KV_W), lambda i: (0, 0)),
                pl.BlockSpec((ABLK, KV_W), lambda i: (clip(i - 1), 0)),
                pl.BlockSpec((ABLK, KV_W), lambda i: (i, 0)),
                pl.BlockSpec((ABLK, KV_W), lambda i: (clip(i + 1), 0))]
    return pl.pallas_call(
        _attn_kernel,
        grid=(NQB,),
        in_specs=[pl.BlockSpec(memory_space=pltpu.SMEM),
                  pl.BlockSpec((ABLK, ATTN_W), lambda i: (i, 0))] + kv_specs + kv_specs,
        out_specs=pl.BlockSpec((ABLK, ATTN_W), lambda i: (i, 0)),
        out_shape=jax.ShapeDtypeStruct((R, ATTN_W), f32),
        compiler_params=_cparams(("arbitrary",)),
        name="attention",
    )(sink, q, k, k, k, k, v, v, v, v)


def _outproj_kernel(attn_ref, u_ref, b_ref, c_ref, up_ref, cp_ref, un_ref, cn_ref, x_ref, tab_ref, gout_ref,
                    cw_ref, w_ref, wr_ref, xo_ref, h2_ref, lg_ref):
    i = pl.program_id(0)
    stream = jnp.where(i == 0, 0, 1)
    prev_ok = (i >= 2).astype(f32)
    next_ok = ((i >= 1) & (i < pl.num_programs(0) - 1)).astype(f32)
    w = c_ref[...] * u_ref[...]
    w_before = cp_ref[SUBLANES - 1:SUBLANES, :] * up_ref[SUBLANES - 1:SUBLANES, :] * prev_ok
    w_after = cn_ref[0:1, :] * un_ref[0:1, :] * next_ok
    rows = lax.broadcasted_iota(i32, (TM, 1), 0)
    w_prev = jnp.where(rows == 0, w_before, pltpu.roll(w, 1, 0))
    w_next = jnp.where(rows == TM - 1, w_after, pltpu.roll(w, TM - 1, 0))
    conv = b_ref[...] * (cw_ref[0:1, :] * w_prev + cw_ref[1:2, :] * w + cw_ref[2:3, :] * w_next)

    attn = attn_ref[...]
    a_n = attn * _rms_scale(attn) * gout_ref[0:1, :]
    c_n = conv * _rms_scale(conv) * gout_ref[1:2, :]
    cat = jnp.concatenate([a_n, c_n], axis=1).astype(bf16)
    mixed = jnp.dot(cat, w_ref[...], preferred_element_type=f32)
    xn = x_ref[...] + tab_ref[stream, 0:1, :] * mixed
    xo_ref[...] = xn

    h2 = xn * _rms_scale(xn) * tab_ref[stream, 1:2, :] + tab_ref[stream, 2:3, :]
    h2_ref[...] = h2
    hi = h2.astype(bf16)
    lo = (h2 - hi.astype(f32)).astype(bf16)
    lg_ref[...] = (jnp.dot(hi, wr_ref[0], preferred_element_type=f32)
                   + jnp.dot(lo, wr_ref[0], preferred_element_type=f32)
                   + jnp.dot(hi, wr_ref[1], preferred_element_type=f32))


def _outproj(attn, p, x, tab, gout, cw, w, wr):
    halo = TM // SUBLANES
    last8 = R // SUBLANES - 1
    row = lambda i: (i, 0)
    prev8 = lambda c: (lambda i: (jnp.maximum(i * halo - 1, 0), c))
    next8 = lambda c: (lambda i: (jnp.minimum((i + 1) * halo, last8), c))
    return pl.pallas_call(
        _outproj_kernel,
        grid=(NT,),
        in_specs=[pl.BlockSpec((TM, ATTN_W), row),
                  pl.BlockSpec((TM, CONV_W), lambda i: (i, COL_U)),
                  pl.BlockSpec((TM, CONV_W), lambda i: (i, COL_B)),
                  pl.BlockSpec((TM, CONV_W), lambda i: (i, COL_C)),
                  pl.BlockSpec((SUBLANES, CONV_W), prev8(COL_U)),
                  pl.BlockSpec((SUBLANES, CONV_W), prev8(COL_C)),
                  pl.BlockSpec((SUBLANES, CONV_W), next8(COL_U)),
                  pl.BlockSpec((SUBLANES, CONV_W), next8(COL_C)),
                  pl.BlockSpec((TM, D), row),
                  pl.BlockSpec((2, 3, D), lambda i: (0, 0, 0)),
                  pl.BlockSpec((2, CONV_W), lambda i: (0, 0)),
                  pl.BlockSpec((3, CONV_W), lambda i: (0, 0)),
                  pl.BlockSpec((D, D), lambda i: (0, 0)),
                  pl.BlockSpec((2, D, LANES), lambda i: (0, 0, 0))],
        out_specs=[pl.BlockSpec((TM, D), row),
                   pl.BlockSpec((TM, D), row),
                   pl.BlockSpec((TM, LANES), row)],
        out_shape=[jax.ShapeDtypeStruct((R, D), f32),
                   jax.ShapeDtypeStruct((R, D), f32),
                   jax.ShapeDtypeStruct((R, LANES), f32)],
        compiler_params=_cparams(("arbitrary",)),
        name="outproj",
    )(attn, p, p, p, p, p, p, p, x, tab, gout, cw, w, wr)


def _router_kernel(bias_ref, lt_ref, upper_ref, lower_ref, ones_ref, e_ref, g_ref, rk_ref, cnt_ref):
    score = [1.0 / (1.0 + jnp.exp(-lt_ref[e])) for e in range(N_EXPERTS)]
    sel = [score[e] + bias_ref[e] for e in range(N_EXPERTS)]

    def top2_sum(a, b, c, d):
        p, q = jnp.maximum(a, b), jnp.minimum(a, b)
        r, s = jnp.maximum(c, d), jnp.minimum(c, d)
        return jnp.maximum(p, r) + jnp.maximum(jnp.minimum(p, r), jnp.maximum(q, s))

    gscore = [top2_sum(*sel[EPG * g:EPG * (g + 1)]) for g in range(N_GROUPS)]
    best, gidx = gscore[0], jnp.zeros(gscore[0].shape, i32)
    for g in range(1, N_GROUPS):
        take = gscore[g] > best
        best = jnp.where(take, gscore[g], best)
        gidx = jnp.where(take, g, gidx)

    def pick_group(vals):
        out = []
        for j in range(EPG):
            v = vals[j]
            for g in range(1, N_GROUPS):
                v = jnp.where(gidx == g, vals[EPG * g + j], v)
            out.append(v)
        return out

    in_sel = pick_group(sel)
    in_score = pick_group(score)

    def argmax_first(vals, excluded):
        bv, bi = None, None
        for j in range(EPG):
            v = vals[j] if excluded is None else jnp.where(excluded == j, -jnp.inf, vals[j])
            if bv is None:
                bv, bi = v, jnp.zeros(v.shape, i32)
            else:
                take = v > bv
                bv = jnp.where(take, v, bv)
                bi = jnp.where(take, j, bi)
        return bi

    i1 = argmax_first(in_sel, None)
    i2 = argmax_first(in_sel, i1)

    def pick_local(vals, idx):
        v = vals[0]
        for j in range(1, EPG):
            v = jnp.where(idx == j, vals[j], v)
        return v

    s1, s2 = pick_local(in_score, i1), pick_local(in_score, i2)
    tot = s1 + s2
    e1 = gidx * EPG + i1
    e2 = gidx * EPG + i2
    e_ref[0], e_ref[1] = e1, e2
    g_ref[0], g_ref[1] = s1 / tot, s2 / tot

    tok = (lax.broadcasted_iota(i32, (RT, LANES), 0) * LANES + lax.broadcasted_iota(i32, (RT, LANES), 1))
    valid = tok < R
    onehot = [(((e1 == e) | (e2 == e)) & valid).astype(f32) for e in range(N_EXPERTS)]
    stack = jnp.concatenate(onehot, axis=0).astype(bf16)
    within = jnp.dot(stack, upper_ref[...], preferred_element_type=f32)
    rowtot = jnp.dot(stack, ones_ref[...], preferred_element_type=f32)
    rk1 = jnp.zeros((RT, LANES), f32)
    rk2 = jnp.zeros((RT, LANES), f32)
    for e in range(N_EXPERTS):
        rt_e = rowtot[e * RT:(e + 1) * RT]
        before = jnp.dot(lower_ref[...], rt_e.astype(bf16), preferred_element_type=f32)
        rank_e = within[e * RT:(e + 1) * RT] + before
        rk1 = jnp.where(e1 == e, rank_e, rk1)
        rk2 = jnp.where(e2 == e, rank_e, rk2)
        cnt_ref[e:e + 1, :] = (before + rt_e)[RT - 1:RT, :]
    rk_ref[0], rk_ref[1] = rk1, rk2


def _router(bias, logits_t, upper, lower, ones):
    full = lambda *shape: pl.BlockSpec(shape, lambda: (0,) * len(shape))
    return pl.pallas_call(
        _router_kernel,
        in_specs=[pl.BlockSpec(memory_space=pltpu.SMEM),
                  full(N_EXPERTS, RT, LANES), full(LANES, LANES), full(RT, RT), full(LANES, LANES)],
        out_specs=[full(2, RT, LANES), full(2, RT, LANES), full(2, RT, LANES), full(N_EXPERTS, LANES)],
        out_shape=[jax.ShapeDtypeStruct((2, RT, LANES), i32),
                   jax.ShapeDtypeStruct((2, RT, LANES), f32),
                   jax.ShapeDtypeStruct((2, RT, LANES), f32),
                   jax.ShapeDtypeStruct((N_EXPERTS, LANES), f32)],
        compiler_params=pltpu.CompilerParams(vmem_limit_bytes=VMEM_LIMIT),
        name="router",
    )(bias, logits_t, upper, lower, ones)


DSEM = 8


def _dispatch_kernel(dest_ref, h_hbm, xs_in, xs_hbm, sems):
    del xs_in
    i = pl.program_id(0)

    def copy(n):
        t = i * TM + n // 2
        return pltpu.make_async_copy(h_hbm.at[pl.ds(t, 1)], xs_hbm.at[pl.ds(dest_ref[0, 0, n], 1)],
                                     sems.at[n % DSEM])

    def start(n, c):
        copy(n).start()
        return c

    def wait(n, c):
        copy(n).wait()
        return c

    lax.fori_loop(0, 2 * TM, start, 0)
    lax.fori_loop(0, 2 * TM, wait, 0)


def _dispatch(dest3, h2, xs_zero):
    return pl.pallas_call(
        _dispatch_kernel,
        grid=(NT,),
        in_specs=[pl.BlockSpec((1, 1, 2 * TM), lambda i: (i, 0, 0), memory_space=pltpu.SMEM),
                  pl.BlockSpec(memory_space=pl.ANY),
                  pl.BlockSpec(memory_space=pl.ANY)],
        out_specs=pl.BlockSpec(memory_space=pl.ANY),
        out_shape=jax.ShapeDtypeStruct((NROWS, D), f32),
        scratch_shapes=[pltpu.SemaphoreType.DMA((DSEM,))],
        input_output_aliases={2: 0},
        compiler_params=_cparams(("arbitrary",)),
        name="dispatch",
    )(dest3, h2, xs_zero)


def _expert_kernel(be_ref, nb_ref, x_ref, wg_ref, wu_ref, wd_ref, o_ref):
    del be_ref
    used = pl.program_id(0) < nb_ref[0]

    @pl.when(used)
    def _():
        x = x_ref[...].astype(bf16)
        a = jnp.dot(x, wg_ref[0], preferred_element_type=f32)
        b = jnp.dot(x, wu_ref[0], preferred_element_type=f32)
        h = (a / (1.0 + jnp.exp(-a)) * b).astype(bf16)
        o_ref[...] = jnp.dot(h, wd_ref[0], preferred_element_type=f32)

    @pl.when(jnp.logical_not(used))
    def _():
        o_ref[...] = jnp.zeros_like(o_ref)


def _experts(block_expert, nblocks, xs, wg, wu, wd):
    blk = lambda j, be, nb: (jnp.minimum(j, nb[0] - 1), 0)
    wsel = lambda j, be, nb: (be[jnp.minimum(j, nb[0] - 1)], 0, 0)
    return pl.pallas_call(
        _expert_kernel,
        grid_spec=pltpu.PrefetchScalarGridSpec(
            num_scalar_prefetch=2,
            grid=(NBLK,),
            in_specs=[pl.BlockSpec((EBLK, D), blk),
                      pl.BlockSpec((1, D, D_FF), wsel),
                      pl.BlockSpec((1, D, D_FF), wsel),
                      pl.BlockSpec((1, D_FF, D), wsel)],
            out_specs=pl.BlockSpec((EBLK, D), lambda j, be, nb: (j, 0))),
        out_shape=jax.ShapeDtypeStruct((NROWS, D), f32),
        compiler_params=_cparams(("arbitrary",)),
        name="experts",
    )(block_expert, nblocks, xs, wg, wu, wd)


def _combine_kernel(dest_ref, ys_hbm, x_ref, gates_ref, gate_ref, o_ref, ybuf, sems):
    i = pl.program_id(0)
    stream = jnp.where(i == 0, 0, 1)

    def copy(n):
        return pltpu.make_async_copy(ys_hbm.at[pl.ds(dest_ref[0, 0, n], 1)],
                                     ybuf.at[n % 2, pl.ds(n // 2, 1)], sems.at[n % DSEM])

    def start(n, c):
        copy(n).start()
        return c

    def wait(n, c):
        copy(n).wait()
        return c

    lax.fori_loop(0, 2 * TM, start, 0)
    lax.fori_loop(0, 2 * TM, wait, 0)
    y = ybuf[0] * gates_ref[:, 0:1] + ybuf[1] * gates_ref[:, 1:2]
    o_ref[...] = x_ref[...] + gate_ref[stream] * y


def _combine(dest3, ys, x, gates, gate2):
    row = lambda i: (i, 0)
    return pl.pallas_call(
        _combine_kernel,
        grid=(NT,),
        in_specs=[pl.BlockSpec((1, 1, 2 * TM), lambda i: (i, 0, 0), memory_space=pltpu.SMEM),
                  pl.BlockSpec(memory_space=pl.ANY),
                  pl.BlockSpec((TM, D), row),
                  pl.BlockSpec((TM, 2), row),
                  pl.BlockSpec((2, 1, D), lambda i: (0, 0, 0))],
        out_specs=pl.BlockSpec((TM, D), row),
        out_shape=jax.ShapeDtypeStruct((R, D), f32),
        scratch_shapes=[pltpu.VMEM((2, TM, D), f32), pltpu.SemaphoreType.DMA((DSEM,))],
        compiler_params=_cparams(("arbitrary",)),
        name="combine",
    )(dest3, ys, x, gates, gate2)


def _rope_tables():
    half = HEAD_DIM // 2
    inv_freq = ROPE_THETA ** (-jnp.arange(0, half, 2, dtype=f32) / half)
    t = jnp.arange(SEQ)
    pos = jnp.stack([(t // GRID_W).astype(f32), (t % GRID_W).astype(f32)], axis=1)
    d = jnp.arange(LANES) % HEAD_DIM
    axis, sub = d // half, d % half
    ang = pos[:, axis] * inv_freq[sub % (half // 2)][None, :]
    cos, sin = jnp.cos(ang), jnp.sin(ang)
    first = (sub < half // 2)[None, :]
    sa = jnp.where(first, -sin, 0.0)
    sb = jnp.where(first, 0.0, sin)
    ident = jnp.ones((CTX, LANES), f32)
    zeros = jnp.zeros((CTX, LANES), f32)
    return (jnp.concatenate([ident, cos]), jnp.concatenate([zeros, sa]), jnp.concatenate([zeros, sb]))


def _permute_in_cols(w):
    kv0, kv1 = ATTN_W, ATTN_W + 2 * KV_W
    return jnp.concatenate([w[..., :kv0], w[..., kv1:], w[..., kv0:kv1]], axis=-1)


def kernel(x, c, ctx, c_ctx, w_ada, b_ada, g_attn, w_in, q_norm_g, k_norm_g, sink, conv_w, g_out_attn, g_out_conv, w_out, g_ffn, w_router, router_bias, w_exp_gate, w_exp_up, w_exp_down):
    assert x.shape == (1, SEQ, D) and ctx.shape == (1, CTX, D)
    xa = jnp.concatenate([ctx[0], x[0]], axis=0)

    silu = lambda t: t * jax.nn.sigmoid(t)
    s_vec = jnp.stack([silu(c_ctx), silu(c[0])])
    mods = _adaln(jnp.broadcast_to(s_vec[:, :, None], (2, D, LANES)), w_ada, b_ada).reshape(DEPTH, 2, 6, D)

    cos, sa, sb = _rope_tables()
    lane = jnp.arange(LANES)
    ones_bd = (lane[:, None] // HEAD_DIM == lane[None, :] // HEAD_DIM).astype(bf16)
    upper = (lane[:, None] < lane[None, :]).astype(bf16)
    ones = jnp.ones((LANES, LANES), bf16)
    rt = jnp.arange(RT)
    lower = (rt[None, :] < rt[:, None]).astype(bf16)
    wr_pad = jnp.pad(w_router, ((0, 0), (0, LANES - N_EXPERTS)))
    wr_hi = wr_pad.astype(bf16)
    wr = jnp.stack([wr_hi, (wr_pad - wr_hi.astype(f32)).astype(bf16)])

    w_in_b = _permute_in_cols(w_in).astype(bf16)
    w_out_b = w_out.astype(bf16)
    xs_zero = jnp.zeros((NROWS, D), f32)

    for l in range(DEPTH):
        mod = mods[l]
        ms1 = jnp.stack([g_attn[l] * (1.0 + mod[:, 1]), mod[:, 0]], axis=1)
        p = _inproj(xa, ms1, w_in_b[l])
        g2 = jnp.stack([jnp.tile(q_norm_g[l] * HEAD_DIM ** -0.5, 2), jnp.tile(k_norm_g[l], 2)])
        q, k, v = _qkprep(p, cos, sa, sb, g2, ones_bd)
        attn = _attention(sink[l], q, k, v)
        tab = jnp.stack([mod[:, 2], g_ffn[l] * (1.0 + mod[:, 4]), mod[:, 3]], axis=1)
        gout = jnp.stack([g_out_attn[l], g_out_conv[l]])
        xa, h2, logits = _outproj(attn, p, xa, tab, gout, conv_w[l], w_out_b[l], wr)

        lt = jnp.pad(logits[:, :N_EXPERTS].T, ((0, 0), (0, RT * LANES - R))).reshape(N_EXPERTS, RT, LANES)
        e_t, g_t, rk_t, cnt = _router(router_bias, lt, upper, lower, ones)
        e_idx = e_t.reshape(2, RT * LANES)[:, :R]
        gates = g_t.reshape(2, RT * LANES)[:, :R].T
        rank = rk_t.reshape(2, RT * LANES)[:, :R].astype(i32)
        counts = cnt[:, 0].astype(i32)
        padded = (counts + EBLK - 1) // EBLK * EBLK
        pend = jnp.cumsum(padded)
        dest = (pend - padded)[e_idx] + rank
        dest3 = dest.T.reshape(NT, 1, 2 * TM)
        nblocks = (pend[-1] // EBLK).astype(i32).reshape(1)
        block_expert = jnp.clip(jnp.searchsorted(pend, jnp.arange(NBLK, dtype=i32) * EBLK, side='right'),
                                0, N_EXPERTS - 1).astype(i32)

        xs = _dispatch(dest3, h2, xs_zero)
        ys = _experts(block_expert, nblocks, xs, w_exp_gate[l].astype(bf16), w_exp_up[l].astype(bf16),
                      w_exp_down[l].astype(bf16))
        xa = _combine(dest3, ys, xa, gates, mod[:, 5][:, None, :])

    return xa[CTX:][None]
```

```python
import functools

import jax
import jax.numpy as jnp
from jax import lax
from jax.experimental import pallas as pl
from jax.experimental.pallas import tpu as pltpu

f32 = jnp.float32
bf16 = jnp.bfloat16
i32 = jnp.int32

D = 2048
DEPTH = 4
SEQ = 8192
CTX = 256
R = CTX + SEQ
GRID_W = 64
ATTN_W = 1024
CONV_W = 1024
HEAD_DIM = 64
N_HEADS = 16
KV_W = 128
IN_COLS = ATTN_W + 2 * KV_W + 3 * CONV_W
ABLK = 128
WINDOW = 128
N_EXPERTS = 16
N_GROUPS = 4
EPG = 4
D_FF = 1024
EPS = 1e-6
NEG_INF = -1e30
ROPE_THETA = 10000.0

LANES = 128
SUBLANES = 8
TM = 256
NT = R // TM
EBLK = 256
EBLK_SHIFT = EBLK.bit_length() - 1
assert 1 << EBLK_SHIFT == EBLK
NBLK = -(-(2 * R + N_EXPERTS * (EBLK - 1)) // EBLK)
NROWS = NBLK * EBLK
RT = LANES
VMEM_LIMIT = 56 * 1024 * 1024


def _cparams(sem):
    return pltpu.CompilerParams(dimension_semantics=sem, vmem_limit_bytes=VMEM_LIMIT)


ADA_TK = 512
ADA_TN = 1536


def _adaln_kernel(s_ref, w_ref, b_ref, o_ref, acc_ref):
    k = pl.program_id(2)

    @pl.when(k == 0)
    def _():
        acc_ref[...] = jnp.zeros_like(acc_ref)

    for j in range(ADA_TN // LANES):
        w = w_ref[0, :, j * LANES:(j + 1) * LANES]
        for v in range(2):
            p = (w * s_ref[v]).reshape(ADA_TK // SUBLANES, SUBLANES, LANES).sum(axis=0)
            acc_ref[v, :, j * LANES:(j + 1) * LANES] += p

    @pl.when(k == pl.num_programs(2) - 1)
    def _():
        for v in range(2):
            o_ref[0, v:v + 1, :] = acc_ref[v].sum(axis=0, keepdims=True) + b_ref[0]


def _adaln(s_bcast, w_ada, b_ada):
    return pl.pallas_call(
        _adaln_kernel,
        grid=(DEPTH, 6 * D // ADA_TN, D // ADA_TK),
        in_specs=[pl.BlockSpec((2, ADA_TK, LANES), lambda l, n, k: (0, k, 0)),
                  pl.BlockSpec((1, ADA_TK, ADA_TN), lambda l, n, k: (l, k, n)),
                  pl.BlockSpec((1, 1, ADA_TN), lambda l, n, k: (l, 0, n))],
        out_specs=pl.BlockSpec((1, 2, ADA_TN), lambda l, n, k: (l, 0, n)),
        out_shape=jax.ShapeDtypeStruct((DEPTH, 2, 6 * D), f32),
        scratch_shapes=[pltpu.VMEM((2, SUBLANES, ADA_TN), f32)],
        compiler_params=_cparams(("arbitrary", "arbitrary", "arbitrary")),
        name="adaln",
    )(s_bcast, w_ada, b_ada.reshape(DEPTH, 1, 6 * D))


def _rms_scale(x):
    return lax.rsqrt(jnp.mean(x * x, axis=-1, keepdims=True) + EPS)


def _inproj_kernel(x_ref, ms_ref, w_ref, o_ref):
    stream = jnp.where(pl.program_id(0) == 0, 0, 1)
    x = x_ref[...]
    gain = ms_ref[stream, 0:1, :]
    shift = ms_ref[stream, 1:2, :]
    h = (x * _rms_scale(x) * gain + shift).astype(bf16)
    o_ref[...] = jnp.dot(h, w_ref[...], preferred_element_type=f32)


def _inproj(x, ms, w):
    return pl.pallas_call(
        _inproj_kernel,
        grid=(NT,),
        in_specs=[pl.BlockSpec((TM, D), lambda i: (i, 0)),
                  pl.BlockSpec((2, 2, D), lambda i: (0, 0, 0)),
                  pl.BlockSpec((D, IN_COLS), lambda i: (0, 0))],
        out_specs=pl.BlockSpec((TM, IN_COLS), lambda i: (i, 0)),
        out_shape=jax.ShapeDtypeStruct((R, IN_COLS), f32),
        compiler_params=_cparams(("arbitrary",)),
        name="inproj",
    )(x, ms, w)


COL_Q, COL_U, COL_B, COL_C = 0, 1, 2, 3
COL_K, COL_V = 32, 33


def _qkprep_kernel(q_ref, k_ref, v_ref, cos_ref, sa_ref, sb_ref, g_ref, ones_ref, qo_ref, ko_ref, vo_ref):
    cos, sa, sb = cos_ref[...], sa_ref[...], sb_ref[...]
    ones = ones_ref[...]

    def norm_rope(xc, g):
        sq = xc * xc
        hi = sq.astype(bf16)
        lo = (sq - hi.astype(f32)).astype(bf16)
        ss = (jnp.dot(hi, ones, preferred_element_type=f32)
              + jnp.dot(lo, ones, preferred_element_type=f32))
        y = xc * lax.rsqrt(ss * (1.0 / HEAD_DIM) + EPS) * g
        return y * cos + pltpu.roll(y, LANES - 16, 1) * sa + pltpu.roll(y, 16, 1) * sb

    for j in range(ATTN_W // LANES):
        sl = slice(j * LANES, (j + 1) * LANES)
        qo_ref[:, sl] = norm_rope(q_ref[:, sl], g_ref[0:1, :]).astype(bf16)
    ko_ref[...] = norm_rope(k_ref[...], g_ref[1:2, :]).astype(bf16)
    vo_ref[...] = v_ref[...].astype(bf16)


def _qkprep(p, cos, sa, sb, g2, ones_bd):
    row = lambda i: (i, 0)
    return pl.pallas_call(
        _qkprep_kernel,
        grid=(NT,),
        in_specs=[pl.BlockSpec((TM, ATTN_W), lambda i: (i, COL_Q)),
                  pl.BlockSpec((TM, KV_W), lambda i: (i, COL_K)),
                  pl.BlockSpec((TM, KV_W), lambda i: (i, COL_V)),
                  pl.BlockSpec((TM, LANES), row),
                  pl.BlockSpec((TM, LANES), row),
                  pl.BlockSpec((TM, LANES), row),
                  pl.BlockSpec((2, LANES), lambda i: (0, 0)),
                  pl.BlockSpec((LANES, LANES), lambda i: (0, 0))],
        out_specs=[pl.BlockSpec((TM, ATTN_W), row),
                   pl.BlockSpec((TM, KV_W), row),
                   pl.BlockSpec((TM, KV_W), row)],
        out_shape=[jax.ShapeDtypeStruct((R, ATTN_W), bf16),
                   jax.ShapeDtypeStruct((R, KV_W), bf16),
                   jax.ShapeDtypeStruct((R, KV_W), bf16)],
        compiler_params=_cparams(("arbitrary",)),
        name="qkprep",
    )(p, p, p, cos, sa, sb, g2, ones_bd)


NQB = R // ABLK
FIRST_LAT = CTX // ABLK
NKEYS = CTX + 3 * ABLK


def _attn_kernel(sink_ref, q_ref, kc_ref, kp_ref, kk_ref, kn_ref, vc_ref, vp_ref, vk_ref, vn_ref, o_ref):
    i = pl.program_id(0)
    kf = jnp.concatenate([kc_ref[...], kp_ref[...], kk_ref[...], kn_ref[...]], axis=0).astype(f32)
    vf = jnp.concatenate([vc_ref[...], vp_ref[...], vk_ref[...], vn_ref[...]], axis=0).astype(f32)
    lane = lax.broadcasted_iota(i32, (NKEYS, LANES), 1)
    low = lane < HEAD_DIM
    kr = pltpu.roll(kf, HEAD_DIM, 1)
    vr = pltpu.roll(vf, HEAD_DIM, 1)
    zero = jnp.zeros_like(kf)
    k_lo = (jnp.where(low, kf, zero), jnp.where(low, kr, zero))
    k_hi = (jnp.where(low, zero, kr), jnp.where(low, zero, kf))
    v_lo = (jnp.where(low, vf, zero), jnp.where(low, vr, zero))
    v_hi = (jnp.where(low, zero, vr), jnp.where(low, zero, vf))

    qi = lax.broadcasted_iota(i32, (ABLK, NKEYS), 0)
    col = lax.broadcasted_iota(i32, (ABLK, NKEYS), 1)
    kj = col - CTX
    in_window = jnp.abs(qi + ABLK - kj) <= WINDOW
    key_block = i - 1 + jnp.right_shift(kj, 7)
    first_ok = jnp.where(i >= FIRST_LAT, FIRST_LAT, NQB)
    mask = (col < CTX) | (in_window & (key_block >= first_ok) & (key_block < NQB))
    lane_q = lax.broadcasted_iota(i32, (ABLK, LANES), 1)

    contract_last = (((1,), (1,)), ((), ()))
    for h in range(2):
        qs = jnp.concatenate([q_ref[:, (h * 4 + p) * LANES:(h * 4 + p + 1) * LANES] for p in range(4)], axis=0)
        s_par = (lax.dot_general(qs, k_lo[h].astype(bf16), contract_last, preferred_element_type=f32),
                 lax.dot_general(qs, k_hi[h].astype(bf16), contract_last, preferred_element_type=f32))
        probs = ([], [])
        rden = ([], [])
        for par in range(2):
            for p in range(4):
                sink = sink_ref[h * 8 + 2 * p + par]
                s = jnp.where(mask, s_par[par][p * ABLK:(p + 1) * ABLK], NEG_INF)
                m = jnp.maximum(jnp.max(s, axis=-1, keepdims=True), sink)
                e = jnp.exp(s - m)
                den = jnp.sum(e, axis=-1, keepdims=True) + jnp.exp(sink - m)
                probs[par].append(e.astype(bf16))
                rden[par].append(1.0 / den)
        o = (jnp.dot(jnp.concatenate(probs[0], axis=0), v_lo[h].astype(bf16), preferred_element_type=f32)
             + jnp.dot(jnp.concatenate(probs[1], axis=0), v_hi[h].astype(bf16), preferred_element_type=f32))
        for p in range(4):
            scale = jnp.where(lane_q < HEAD_DIM, rden[0][p], rden[1][p])
            o_ref[:, (h * 4 + p) * LANES:(h * 4 + p + 1) * LANES] = o[p * ABLK:(p + 1) * ABLK] * scale


def _attention(sink, q, k, v):
    clip = lambda b: jnp.clip(b, FIRST_LAT, NQB - 1)
    kv_specs = [pl.BlockSpec((CTX, KV_W), lambda i: (0, 0)),
                pl.BlockSpec((ABLK, KV_W), lambda i: (clip(i - 1), 0)),
                pl.BlockSpec((ABLK, KV_W), lambda i: (i, 0)),
                pl.BlockSpec((ABLK, KV_W), lambda i: (clip(i + 1), 0))]
    return pl.pallas_call(
        _attn_kernel,
        grid=(NQB,),
        in_specs=[pl.BlockSpec(memory_space=pltpu.SMEM),
                  pl.BlockSpec((ABLK, ATTN_W), lambda i: (i, 0))] + kv_specs + kv_specs,
        out_specs=pl.BlockSpec((ABLK, ATTN_W), lambda i: (i, 0)),
        out_shape=jax.ShapeDtypeStruct((R, ATTN_W), f32),
        compiler_params=_cparams(("arbitrary",)),
        name="attention",
    )(sink, q, k, k, k, k, v, v, v, v)


def _outproj_kernel(attn_ref, u_ref, b_ref, c_ref, up_ref, cp_ref, un_ref, cn_ref, x_ref, tab_ref, gout_ref,
                    cw_ref, w_ref, wr_ref, xo_ref, h2_ref, lg_ref):
    i = pl.program_id(0)
    stream = jnp.where(i == 0, 0, 1)
    prev_ok = (i >= 2).astype(f32)
    next_ok = ((i >= 1) & (i < pl.num_programs(0) - 1)).astype(f32)
    w = c_ref[...] * u_ref[...]
    w_before = cp_ref[SUBLANES - 1:SUBLANES, :] * up_ref[SUBLANES - 1:SUBLANES, :] * prev_ok
    w_after = cn_ref[0:1, :] * un_ref[0:1, :] * next_ok
    rows = lax.broadcasted_iota(i32, (TM, 1), 0)
    w_prev = jnp.where(rows == 0, w_before, pltpu.roll(w, 1, 0))
    w_next = jnp.where(rows == TM - 1, w_after, pltpu.roll(w, TM - 1, 0))
    conv = b_ref[...] * (cw_ref[0:1, :] * w_prev + cw_ref[1:2, :] * w + cw_ref[2:3, :] * w_next)

    attn = attn_ref[...]
    a_n = attn * _rms_scale(attn) * gout_ref[0:1, :]
    c_n = conv * _rms_scale(conv) * gout_ref[1:2, :]
    cat = jnp.concatenate([a_n, c_n], axis=1).astype(bf16)
    mixed = jnp.dot(cat, w_ref[...], preferred_element_type=f32)
    xn = x_ref[...] + tab_ref[stream, 0:1, :] * mixed
    xo_ref[...] = xn

    h2 = xn * _rms_scale(xn) * tab_ref[stream, 1:2, :] + tab_ref[stream, 2:3, :]
    h2_ref[...] = h2
    hi = h2.astype(bf16)
    lo = (h2 - hi.astype(f32)).astype(bf16)
    lg_ref[...] = (jnp.dot(hi, wr_ref[0], preferred_element_type=f32)
                   + jnp.dot(lo, wr_ref[0], preferred_element_type=f32)
                   + jnp.dot(hi, wr_ref[1], preferred_element_type=f32))


def _outproj(attn, p, x, tab, gout, cw, w, wr):
    halo = TM // SUBLANES
    last8 = R // SUBLANES - 1
    row = lambda i: (i, 0)
    prev8 = lambda c: (lambda i: (jnp.maximum(i * halo - 1, 0), c))
    next8 = lambda c: (lambda i: (jnp.minimum((i + 1) * halo, last8), c))
    return pl.pallas_call(
        _outproj_kernel,
        grid=(NT,),
        in_specs=[pl.BlockSpec((TM, ATTN_W), row),
                  pl.BlockSpec((TM, CONV_W), lambda i: (i, COL_U)),
                  pl.BlockSpec((TM, CONV_W), lambda i: (i, COL_B)),
                  pl.BlockSpec((TM, CONV_W), lambda i: (i, COL_C)),
                  pl.BlockSpec((SUBLANES, CONV_W), prev8(COL_U)),
                  pl.BlockSpec((SUBLANES, CONV_W), prev8(COL_C)),
                  pl.BlockSpec((SUBLANES, CONV_W), next8(COL_U)),
                  pl.BlockSpec((SUBLANES, CONV_W), next8(COL_C)),
                  pl.BlockSpec((TM, D), row),
                  pl.BlockSpec((2, 3, D), lambda i: (0, 0, 0)),
                  pl.BlockSpec((2, CONV_W), lambda i: (0, 0)),
                  pl.BlockSpec((3, CONV_W), lambda i: (0, 0)),
                  pl.BlockSpec((D, D), lambda i: (0, 0)),
                  pl.BlockSpec((2, D, LANES), lambda i: (0, 0, 0))],
        out_specs=[pl.BlockSpec((TM, D), row),
                   pl.BlockSpec((TM, D), row),
                   pl.BlockSpec((TM, LANES), row)],
        out_shape=[jax.ShapeDtypeStruct((R, D), f32),
                   jax.ShapeDtypeStruct((R, D), f32),
                   jax.ShapeDtypeStruct((R, LANES), f32)],
        compiler_params=_cparams(("arbitrary",)),
        name="outproj",
    )(attn, p, p, p, p, p, p, p, x, tab, gout, cw, w, wr)


def _router_kernel(bias_ref, lt_ref, upper_ref, lower_ref, ones_ref, e_ref, g_ref, d_ref, cnt_ref):
    score = [1.0 / (1.0 + jnp.exp(-lt_ref[e])) for e in range(N_EXPERTS)]
    sel = [score[e] + bias_ref[e] for e in range(N_EXPERTS)]

    def top2_sum(a, b, c, d):
        p, q = jnp.maximum(a, b), jnp.minimum(a, b)
        r, s = jnp.maximum(c, d), jnp.minimum(c, d)
        return jnp.maximum(p, r) + jnp.maximum(jnp.minimum(p, r), jnp.maximum(q, s))

    gscore = [top2_sum(*sel[EPG * g:EPG * (g + 1)]) for g in range(N_GROUPS)]
    best, gidx = gscore[0], jnp.zeros(gscore[0].shape, i32)
    for g in range(1, N_GROUPS):
        take = gscore[g] > best
        best = jnp.where(take, gscore[g], best)
        gidx = jnp.where(take, g, gidx)

    def pick_group(vals):
        out = []
        for j in range(EPG):
            v = vals[j]
            for g in range(1, N_GROUPS):
                v = jnp.where(gidx == g, vals[EPG * g + j], v)
            out.append(v)
        return out

    in_sel = pick_group(sel)
    in_score = pick_group(score)

    def argmax_first(vals, excluded):
        bv, bi = None, None
        for j in range(EPG):
            v = vals[j] if excluded is None else jnp.where(excluded == j, -jnp.inf, vals[j])
            if bv is None:
                bv, bi = v, jnp.zeros(v.shape, i32)
            else:
                take = v > bv
                bv = jnp.where(take, v, bv)
                bi = jnp.where(take, j, bi)
        return bi

    i1 = argmax_first(in_sel, None)
    i2 = argmax_first(in_sel, i1)

    def pick_local(vals, idx):
        v = vals[0]
        for j in range(1, EPG):
            v = jnp.where(idx == j, vals[j], v)
        return v

    s1, s2 = pick_local(in_score, i1), pick_local(in_score, i2)
    tot = s1 + s2
    e1 = gidx * EPG + i1
    e2 = gidx * EPG + i2
    e_ref[0], e_ref[1] = e1, e2
    g_ref[0], g_ref[1] = s1 / tot, s2 / tot

    tok = (lax.broadcasted_iota(i32, (RT, LANES), 0) * LANES + lax.broadcasted_iota(i32, (RT, LANES), 1))
    valid = tok < R
    onehot = [(((e1 == e) | (e2 == e)) & valid).astype(f32) for e in range(N_EXPERTS)]
    stack = jnp.concatenate(onehot, axis=0).astype(bf16)
    within = jnp.dot(stack, upper_ref[...], preferred_element_type=f32)
    rowtot = jnp.dot(stack, ones_ref[...], preferred_element_type=f32)
    d1 = jnp.zeros((RT, LANES), i32)
    d2 = jnp.zeros((RT, LANES), i32)
    seg_start = jnp.zeros((1, LANES), i32)
    for e in range(N_EXPERTS):
        rt_e = rowtot[e * RT:(e + 1) * RT]
        before = jnp.dot(lower_ref[...], rt_e.astype(bf16), preferred_element_type=f32)
        rank_e = within[e * RT:(e + 1) * RT] + before
        count_e = (before + rt_e)[RT - 1:RT, :]
        cnt_ref[e:e + 1, :] = count_e
        slot = seg_start + rank_e.astype(i32)
        d1 = jnp.where(e1 == e, slot, d1)
        d2 = jnp.where(e2 == e, slot, d2)
        blocks_e = jnp.right_shift(count_e.astype(i32) + (EBLK - 1), EBLK_SHIFT)
        seg_start = seg_start + jnp.left_shift(blocks_e, EBLK_SHIFT)
    d_ref[0], d_ref[1] = d1, d2


def _router(bias, logits_t, upper, lower, ones):
    full = lambda *shape: pl.BlockSpec(shape, lambda: (0,) * len(shape))
    return pl.pallas_call(
        _router_kernel,
        in_specs=[pl.BlockSpec(memory_space=pltpu.SMEM),
                  full(N_EXPERTS, RT, LANES), full(LANES, LANES), full(RT, RT), full(LANES, LANES)],
        out_specs=[full(2, RT, LANES), full(2, RT, LANES), full(2, RT, LANES), full(N_EXPERTS, LANES)],
        out_shape=[jax.ShapeDtypeStruct((2, RT, LANES), i32),
                   jax.ShapeDtypeStruct((2, RT, LANES), f32),
                   jax.ShapeDtypeStruct((2, RT, LANES), i32),
                   jax.ShapeDtypeStruct((N_EXPERTS, LANES), f32)],
        compiler_params=pltpu.CompilerParams(vmem_limit_bytes=VMEM_LIMIT),
        name="router",
    )(bias, logits_t, upper, lower, ones)


ROW_UNROLL = 8


def _dispatch_kernel(pend_ref, dest_ref, h_ref, xs_hbm, zbuf, sem, zsem):
    @pl.when(pl.program_id(0) == 0)
    def _():
        zbuf[...] = jnp.zeros_like(zbuf)

        def zero_copy(e):
            first = pl.multiple_of(pend_ref[e] - EBLK, EBLK)
            return pltpu.make_async_copy(zbuf, xs_hbm.at[pl.ds(first, EBLK)], zsem)

        def nonempty(e):
            return pend_ref[e] > (pend_ref[e - 1] if e else 0)

        for e in range(N_EXPERTS):
            pl.when(nonempty(e))(lambda e=e: zero_copy(e).start())
        for e in range(N_EXPERTS):
            pl.when(nonempty(e))(lambda e=e: zero_copy(e).wait())

    def start(t, c):
        for k in range(2):
            pltpu.make_async_copy(h_ref.at[pl.ds(t, 1)], xs_hbm.at[pl.ds(dest_ref[0, 0, k * TM + t], 1)], sem).start()
        return c

    def wait(t, c):
        for k in range(2):
            pltpu.make_async_copy(h_ref.at[pl.ds(0, 1)], xs_hbm.at[pl.ds(0, 1)], sem).wait()
        return c

    lax.fori_loop(0, TM, start, 0, unroll=ROW_UNROLL)
    lax.fori_loop(0, TM, wait, 0, unroll=ROW_UNROLL)


def _dispatch(pend, dest3, h2):
    return pl.pallas_call(
        _dispatch_kernel,
        grid_spec=pltpu.PrefetchScalarGridSpec(
            num_scalar_prefetch=1,
            grid=(NT,),
            in_specs=[pl.BlockSpec((1, 1, 2 * TM), lambda i, pend: (i, 0, 0), memory_space=pltpu.SMEM),
                      pl.BlockSpec((TM, D), lambda i, pend: (i, 0))],
            out_specs=pl.BlockSpec(memory_space=pl.ANY),
            scratch_shapes=[pltpu.VMEM((EBLK, D), f32), pltpu.SemaphoreType.DMA(()), pltpu.SemaphoreType.DMA(())]),
        out_shape=jax.ShapeDtypeStruct((NROWS, D), f32),
        compiler_params=_cparams(("arbitrary",)),
        name="dispatch",
    )(pend, dest3, h2)


def _expert_kernel(be_ref, nb_ref, x_ref, wg_ref, wu_ref, wd_ref, o_ref):
    del be_ref
    used = pl.program_id(0) < nb_ref[0]

    @pl.when(used)
    def _():
        x = x_ref[...].astype(bf16)
        a = jnp.dot(x, wg_ref[0], preferred_element_type=f32)
        b = jnp.dot(x, wu_ref[0], preferred_element_type=f32)
        h = (a / (1.0 + jnp.exp(-a)) * b).astype(bf16)
        o_ref[...] = jnp.dot(h, wd_ref[0], preferred_element_type=f32)

    @pl.when(jnp.logical_not(used))
    def _():
        o_ref[...] = jnp.zeros_like(o_ref)


def _experts(block_expert, nblocks, xs, wg, wu, wd):
    blk = lambda j, be, nb: (jnp.minimum(j, nb[0] - 1), 0)
    wsel = lambda j, be, nb: (be[jnp.minimum(j, nb[0] - 1)], 0, 0)
    return pl.pallas_call(
        _expert_kernel,
        grid_spec=pltpu.PrefetchScalarGridSpec(
            num_scalar_prefetch=2,
            grid=(NBLK,),
            in_specs=[pl.BlockSpec((EBLK, D), blk),
                      pl.BlockSpec((1, D, D_FF), wsel),
                      pl.BlockSpec((1, D, D_FF), wsel),
                      pl.BlockSpec((1, D_FF, D), wsel)],
            out_specs=pl.BlockSpec((EBLK, D), lambda j, be, nb: (j, 0))),
        out_shape=jax.ShapeDtypeStruct((NROWS, D), f32),
        compiler_params=_cparams(("arbitrary",)),
        name="experts",
    )(block_expert, nblocks, xs, wg, wu, wd)


def _combine_kernel(dest_ref, dnext_ref, ys_hbm, x_ref, gates_ref, gate_ref, o_ref, ybuf, sems):
    i = pl.program_id(0)
    slot = jnp.bitwise_and(i, 1)
    stream = jnp.where(i == 0, 0, 1)

    def start_tile(idx_ref, s):
        def body(t, c):
            for k in range(2):
                pltpu.make_async_copy(ys_hbm.at[pl.ds(idx_ref[0, 0, k * TM + t], 1)],
                                      ybuf.at[s, k, pl.ds(t, 1)], sems.at[s]).start()
            return c
        lax.fori_loop(0, TM, body, 0, unroll=ROW_UNROLL)

    @pl.when(i == 0)
    def _():
        start_tile(dest_ref, 0)

    @pl.when(i + 1 < pl.num_programs(0))
    def _():
        start_tile(dnext_ref, 1 - slot)

    def wait(t, c):
        for k in range(2):
            pltpu.make_async_copy(ys_hbm.at[pl.ds(0, 1)], ybuf.at[slot, k, pl.ds(0, 1)], sems.at[slot]).wait()
        return c

    lax.fori_loop(0, TM, wait, 0, unroll=ROW_UNROLL)
    y = ybuf[slot, 0] * gates_ref[:, 0:1] + ybuf[slot, 1] * gates_ref[:, 1:2]
    o_ref[...] = x_ref[...] + gate_ref[stream] * y


def _combine(dest3, ys, x, gates, gate2):
    row = lambda i: (i, 0)
    return pl.pallas_call(
        _combine_kernel,
        grid=(NT,),
        in_specs=[pl.BlockSpec((1, 1, 2 * TM), lambda i: (i, 0, 0), memory_space=pltpu.SMEM),
                  pl.BlockSpec((1, 1, 2 * TM), lambda i: (jnp.minimum(i + 1, NT - 1), 0, 0),
                               memory_space=pltpu.SMEM),
                  pl.BlockSpec(memory_space=pl.ANY),
                  pl.BlockSpec((TM, D), row),
                  pl.BlockSpec((TM, 2), row),
                  pl.BlockSpec((2, 1, D), lambda i: (0, 0, 0))],
        out_specs=pl.BlockSpec((TM, D), row),
        out_shape=jax.ShapeDtypeStruct((R, D), f32),
        scratch_shapes=[pltpu.VMEM((2, 2, TM, D), f32), pltpu.SemaphoreType.DMA((2,))],
        compiler_params=_cparams(("arbitrary",)),
        name="combine",
    )(dest3, dest3, ys, x, gates, gate2)


def _rope_tables():
    half = HEAD_DIM // 2
    inv_freq = ROPE_THETA ** (-jnp.arange(0, half, 2, dtype=f32) / half)
    t = jnp.arange(SEQ)
    pos = jnp.stack([(t // GRID_W).astype(f32), (t % GRID_W).astype(f32)], axis=1)
    d = jnp.arange(LANES) % HEAD_DIM
    axis, sub = d // half, d % half
    ang = pos[:, axis] * inv_freq[sub % (half // 2)][None, :]
    cos, sin = jnp.cos(ang), jnp.sin(ang)
    first = (sub < half // 2)[None, :]
    sa = jnp.where(first, -sin, 0.0)
    sb = jnp.where(first, 0.0, sin)
    ident = jnp.ones((CTX, LANES), f32)
    zeros = jnp.zeros((CTX, LANES), f32)
    return (jnp.concatenate([ident, cos]), jnp.concatenate([zeros, sa]), jnp.concatenate([zeros, sb]))


def _permute_in_cols(w):
    kv0, kv1 = ATTN_W, ATTN_W + 2 * KV_W
    return jnp.concatenate([w[..., :kv0], w[..., kv1:], w[..., kv0:kv1]], axis=-1)


def kernel(x, c, ctx, c_ctx, w_ada, b_ada, g_attn, w_in, q_norm_g, k_norm_g, sink, conv_w, g_out_attn, g_out_conv, w_out, g_ffn, w_router, router_bias, w_exp_gate, w_exp_up, w_exp_down):
    assert x.shape == (1, SEQ, D) and ctx.shape == (1, CTX, D)
    xa = jnp.concatenate([ctx[0], x[0]], axis=0)

    silu = lambda t: t * jax.nn.sigmoid(t)
    s_vec = jnp.stack([silu(c_ctx), silu(c[0])])
    mods = _adaln(jnp.broadcast_to(s_vec[:, :, None], (2, D, LANES)), w_ada, b_ada).reshape(DEPTH, 2, 6, D)

    cos, sa, sb = _rope_tables()
    lane = jnp.arange(LANES)
    ones_bd = (lane[:, None] // HEAD_DIM == lane[None, :] // HEAD_DIM).astype(bf16)
    upper = (lane[:, None] < lane[None, :]).astype(bf16)
    ones = jnp.ones((LANES, LANES), bf16)
    rt = jnp.arange(RT)
    lower = (rt[None, :] < rt[:, None]).astype(bf16)
    wr_pad = jnp.pad(w_router, ((0, 0), (0, LANES - N_EXPERTS)))
    wr_hi = wr_pad.astype(bf16)
    wr = jnp.stack([wr_hi, (wr_pad - wr_hi.astype(f32)).astype(bf16)])

    w_in_b = _permute_in_cols(w_in).astype(bf16)
    w_out_b = w_out.astype(bf16)

    for l in range(DEPTH):
        mod = mods[l]
        ms1 = jnp.stack([g_attn[l] * (1.0 + mod[:, 1]), mod[:, 0]], axis=1)
        p = _inproj(xa, ms1, w_in_b[l])
        g2 = jnp.stack([jnp.tile(q_norm_g[l] * HEAD_DIM ** -0.5, 2), jnp.tile(k_norm_g[l], 2)])
        q, k, v = _qkprep(p, cos, sa, sb, g2, ones_bd)
        attn = _attention(sink[l], q, k, v)
        tab = jnp.stack([mod[:, 2], g_ffn[l] * (1.0 + mod[:, 4]), mod[:, 3]], axis=1)
        gout = jnp.stack([g_out_attn[l], g_out_conv[l]])
        xa, h2, logits = _outproj(attn, p, xa, tab, gout, conv_w[l], w_out_b[l], wr)

        lt = jnp.pad(logits[:, :N_EXPERTS].T, ((0, 0), (0, RT * LANES - R))).reshape(N_EXPERTS, RT, LANES)
        _, g_t, d_t, cnt = _router(router_bias, lt, upper, lower, ones)
        gates = g_t.reshape(2, RT * LANES)[:, :R].T
        dest = d_t.reshape(2, RT * LANES)[:, :R]
        dest3 = dest.reshape(2, NT, TM).transpose(1, 0, 2).reshape(NT, 1, 2 * TM)
        counts = cnt[:, 0].astype(i32)
        pend = jnp.cumsum((counts + EBLK - 1) // EBLK * EBLK)
        nblocks = (pend[-1] // EBLK).astype(i32).reshape(1)
        block_start = jnp.arange(NBLK, dtype=i32) * EBLK
        block_expert = jnp.minimum(jnp.sum(block_start[:, None] >= pend[None, :], axis=1), N_EXPERTS - 1).astype(i32)

        xs = _dispatch(pend, dest3, h2)
        ys = _experts(block_expert, nblocks, xs, w_exp_gate[l].astype(bf16), w_exp_up[l].astype(bf16),
                      w_exp_down[l].astype(bf16))
        xa = _combine(dest3, ys, xa, gates, mod[:, 5][:, None, :])

    return xa[CTX:][None]
```

```python
import functools

import jax
import jax.numpy as jnp
from jax import lax
from jax.experimental import pallas as pl
from jax.experimental.pallas import tpu as pltpu

f32 = jnp.float32
bf16 = jnp.bfloat16
i32 = jnp.int32

D = 2048
DEPTH = 4
SEQ = 8192
CTX = 256
R = CTX + SEQ
GRID_W = 64
ATTN_W = 1024
CONV_W = 1024
HEAD_DIM = 64
N_HEADS = 16
KV_W = 128
IN_COLS = ATTN_W + 2 * KV_W + 3 * CONV_W
ABLK = 128
WINDOW = 128
N_EXPERTS = 16
N_GROUPS = 4
EPG = 4
D_FF = 1024
EPS = 1e-6
NEG_INF = -1e30
ROPE_THETA = 10000.0

LANES = 128
SUBLANES = 8
TM = 256
NT = R // TM
EBLK = 256
EBLK_SHIFT = EBLK.bit_length() - 1
assert 1 << EBLK_SHIFT == EBLK
NBLK = -(-(2 * R + N_EXPERTS * (EBLK - 1)) // EBLK)
NROWS = NBLK * EBLK
RT = LANES
VMEM_LIMIT = 56 * 1024 * 1024


def _cparams(sem):
    return pltpu.CompilerParams(dimension_semantics=sem, vmem_limit_bytes=VMEM_LIMIT)


ADA_TK = 512
ADA_TN = 1536


def _adaln_kernel(s_ref, w_ref, b_ref, o_ref, acc_ref):
    k = pl.program_id(2)

    @pl.when(k == 0)
    def _():
        acc_ref[...] = jnp.zeros_like(acc_ref)

    for j in range(ADA_TN // LANES):
        w = w_ref[0, :, j * LANES:(j + 1) * LANES]
        for v in range(2):
            p = (w * s_ref[v]).reshape(ADA_TK // SUBLANES, SUBLANES, LANES).sum(axis=0)
            acc_ref[v, :, j * LANES:(j + 1) * LANES] += p

    @pl.when(k == pl.num_programs(2) - 1)
    def _():
        for v in range(2):
            o_ref[0, v:v + 1, :] = acc_ref[v].sum(axis=0, keepdims=True) + b_ref[0]


def _adaln(s_bcast, w_ada, b_ada):
    return pl.pallas_call(
        _adaln_kernel,
        grid=(DEPTH, 6 * D // ADA_TN, D // ADA_TK),
        in_specs=[pl.BlockSpec((2, ADA_TK, LANES), lambda l, n, k: (0, k, 0)),
                  pl.BlockSpec((1, ADA_TK, ADA_TN), lambda l, n, k: (l, k, n)),
                  pl.BlockSpec((1, 1, ADA_TN), lambda l, n, k: (l, 0, n))],
        out_specs=pl.BlockSpec((1, 2, ADA_TN), lambda l, n, k: (l, 0, n)),
        out_shape=jax.ShapeDtypeStruct((DEPTH, 2, 6 * D), f32),
        scratch_shapes=[pltpu.VMEM((2, SUBLANES, ADA_TN), f32)],
        compiler_params=_cparams(("arbitrary", "arbitrary", "arbitrary")),
        name="adaln",
    )(s_bcast, w_ada, b_ada.reshape(DEPTH, 1, 6 * D))


def _rms_scale(x):
    return lax.rsqrt(jnp.mean(x * x, axis=-1, keepdims=True) + EPS)


WCH = 256


def _load_weight_bf16(w_hbm, wbuf, stage, sems):
    n = wbuf.shape[0] // WCH

    def chunk(c):
        return pltpu.make_async_copy(w_hbm.at[pl.ds(c * WCH, WCH)], stage.at[c % 2], sems.at[c % 2])

    chunk(0).start()
    for c in range(n):
        if c + 1 < n:
            chunk(c + 1).start()
        chunk(c).wait()
        wbuf[pl.ds(c * WCH, WCH), :] = stage[c % 2].astype(bf16)


Q0, K0, V0, U0, B0, C0 = 0, ATTN_W, ATTN_W + KV_W, ATTN_W + 2 * KV_W, ATTN_W + 2 * KV_W + CONV_W, \
    ATTN_W + 2 * KV_W + 2 * CONV_W


def _inproj_kernel(layer, x_ref, ms_ref, w_hbm, q_ref, kv_ref, u_ref, b_ref, c_ref, wbuf, stage, sems):
    @pl.when(pl.program_id(0) == 0)
    def _():
        _load_weight_bf16(w_hbm.at[layer], wbuf, stage, sems)

    stream = jnp.where(pl.program_id(0) == 0, 0, 1)
    x = x_ref[...]
    gain = ms_ref[stream, 0:1, :]
    shift = ms_ref[stream, 1:2, :]
    h = (x * _rms_scale(x) * gain + shift).astype(bf16)
    p = jnp.dot(h, wbuf[...], preferred_element_type=f32)
    q_ref[...] = p[:, Q0:K0]
    kv_ref[...] = p[:, K0:U0]
    u_ref[...] = p[:, U0:B0]
    b_ref[...] = p[:, B0:C0]
    c_ref[...] = p[:, C0:]


def _inproj(layer, x, ms, w_in):
    row = lambda i: (i, 0)
    widths = (ATTN_W, 2 * KV_W, CONV_W, CONV_W, CONV_W)
    return pl.pallas_call(
        functools.partial(_inproj_kernel, layer),
        grid=(NT,),
        in_specs=[pl.BlockSpec((TM, D), row),
                  pl.BlockSpec((2, 2, D), lambda i: (0, 0, 0)),
                  pl.BlockSpec(memory_space=pl.ANY)],
        out_specs=[pl.BlockSpec((TM, n), row) for n in widths],
        out_shape=[jax.ShapeDtypeStruct((R, n), f32) for n in widths],
        scratch_shapes=[pltpu.VMEM((D, IN_COLS), bf16), pltpu.VMEM((2, WCH, IN_COLS), f32),
                        pltpu.SemaphoreType.DMA((2,))],
        compiler_params=_cparams(("arbitrary",)),
        name="inproj",
    )(x, ms, w_in)


def _qkprep_kernel(q_ref, k_ref, v_ref, cos_ref, sa_ref, sb_ref, g_ref, ones_ref, qo_ref, ko_ref, vo_ref):
    cos, sa, sb = cos_ref[...], sa_ref[...], sb_ref[...]
    ones = ones_ref[...]

    def norm_rope(xc, g):
        sq = xc * xc
        hi = sq.astype(bf16)
        lo = (sq - hi.astype(f32)).astype(bf16)
        ss = (jnp.dot(hi, ones, preferred_element_type=f32)
              + jnp.dot(lo, ones, preferred_element_type=f32))
        y = xc * lax.rsqrt(ss * (1.0 / HEAD_DIM) + EPS) * g
        return y * cos + pltpu.roll(y, LANES - 16, 1) * sa + pltpu.roll(y, 16, 1) * sb

    for j in range(ATTN_W // LANES):
        sl = slice(j * LANES, (j + 1) * LANES)
        qo_ref[:, sl] = norm_rope(q_ref[:, sl], g_ref[0:1, :]).astype(bf16)
    ko_ref[...] = norm_rope(k_ref[...], g_ref[1:2, :]).astype(bf16)
    vo_ref[...] = v_ref[...].astype(bf16)


def _qkprep(q, kv, cos, sa, sb, g2, ones_bd):
    row = lambda i: (i, 0)
    return pl.pallas_call(
        _qkprep_kernel,
        grid=(NT,),
        in_specs=[pl.BlockSpec((TM, ATTN_W), row),
                  pl.BlockSpec((TM, KV_W), lambda i: (i, 0)),
                  pl.BlockSpec((TM, KV_W), lambda i: (i, 1)),
                  pl.BlockSpec((TM, LANES), row),
                  pl.BlockSpec((TM, LANES), row),
                  pl.BlockSpec((TM, LANES), row),
                  pl.BlockSpec((2, LANES), lambda i: (0, 0)),
                  pl.BlockSpec((LANES, LANES), lambda i: (0, 0))],
        out_specs=[pl.BlockSpec((TM, ATTN_W), row),
                   pl.BlockSpec((TM, KV_W), row),
                   pl.BlockSpec((TM, KV_W), row)],
        out_shape=[jax.ShapeDtypeStruct((R, ATTN_W), bf16),
                   jax.ShapeDtypeStruct((R, KV_W), bf16),
                   jax.ShapeDtypeStruct((R, KV_W), bf16)],
        compiler_params=_cparams(("arbitrary",)),
        name="qkprep",
    )(q, kv, kv, cos, sa, sb, g2, ones_bd)


NQB = R // ABLK
FIRST_LAT = CTX // ABLK
NKEYS = CTX + 3 * ABLK


def _attn_kernel(sink_ref, q_ref, kc_ref, kp_ref, kk_ref, kn_ref, vc_ref, vp_ref, vk_ref, vn_ref, o_ref):
    i = pl.program_id(0)
    kf = jnp.concatenate([kc_ref[...], kp_ref[...], kk_ref[...], kn_ref[...]], axis=0).astype(f32)
    vf = jnp.concatenate([vc_ref[...], vp_ref[...], vk_ref[...], vn_ref[...]], axis=0).astype(f32)
    lane = lax.broadcasted_iota(i32, (NKEYS, LANES), 1)
    low = lane < HEAD_DIM
    kr = pltpu.roll(kf, HEAD_DIM, 1)
    vr = pltpu.roll(vf, HEAD_DIM, 1)
    zero = jnp.zeros_like(kf)
    k_lo = (jnp.where(low, kf, zero), jnp.where(low, kr, zero))
    k_hi = (jnp.where(low, zero, kr), jnp.where(low, zero, kf))
    v_lo = (jnp.where(low, vf, zero), jnp.where(low, vr, zero))
    v_hi = (jnp.where(low, zero, vr), jnp.where(low, zero, vf))

    qi = lax.broadcasted_iota(i32, (ABLK, NKEYS), 0)
    col = lax.broadcasted_iota(i32, (ABLK, NKEYS), 1)
    kj = col - CTX
    in_window = jnp.abs(qi + ABLK - kj) <= WINDOW
    key_block = i - 1 + jnp.right_shift(kj, 7)
    first_ok = jnp.where(i >= FIRST_LAT, FIRST_LAT, NQB)
    mask = (col < CTX) | (in_window & (key_block >= first_ok) & (key_block < NQB))
    lane_q = lax.broadcasted_iota(i32, (ABLK, LANES), 1)

    contract_last = (((1,), (1,)), ((), ()))
    for h in range(2):
        qs = jnp.concatenate([q_ref[:, (h * 4 + p) * LANES:(h * 4 + p + 1) * LANES] for p in range(4)], axis=0)
        s_par = (lax.dot_general(qs, k_lo[h].astype(bf16), contract_last, preferred_element_type=f32),
                 lax.dot_general(qs, k_hi[h].astype(bf16), contract_last, preferred_element_type=f32))
        probs = ([], [])
        rden = ([], [])
        for par in range(2):
            for p in range(4):
                sink = sink_ref[h * 8 + 2 * p + par]
                s = jnp.where(mask, s_par[par][p * ABLK:(p + 1) * ABLK], NEG_INF)
                m = jnp.maximum(jnp.max(s, axis=-1, keepdims=True), sink)
                e = jnp.exp(s - m)
                den = jnp.sum(e, axis=-1, keepdims=True) + jnp.exp(sink - m)
                probs[par].append(e.astype(bf16))
                rden[par].append(1.0 / den)
        o = (jnp.dot(jnp.concatenate(probs[0], axis=0), v_lo[h].astype(bf16), preferred_element_type=f32)
             + jnp.dot(jnp.concatenate(probs[1], axis=0), v_hi[h].astype(bf16), preferred_element_type=f32))
        for p in range(4):
            scale = jnp.where(lane_q < HEAD_DIM, rden[0][p], rden[1][p])
            o_ref[:, (h * 4 + p) * LANES:(h * 4 + p + 1) * LANES] = o[p * ABLK:(p + 1) * ABLK] * scale


def _attention(sink, q, k, v):
    clip = lambda b: jnp.clip(b, FIRST_LAT, NQB - 1)
    kv_specs = [pl.BlockSpec((CTX, KV_W), lambda i: (0, 0)),
                pl.BlockSpec((ABLK, KV_W), lambda i: (clip(i - 1), 0)),
                pl.BlockSpec((ABLK, KV_W), lambda i: (i, 0)),
                pl.BlockSpec((ABLK, KV_W), lambda i: (clip(i + 1), 0))]
    return pl.pallas_call(
        _attn_kernel,
        grid=(NQB,),
        in_specs=[pl.BlockSpec(memory_space=pltpu.SMEM),
                  pl.BlockSpec((ABLK, ATTN_W), lambda i: (i, 0))] + kv_specs + kv_specs,
        out_specs=pl.BlockSpec((ABLK, ATTN_W), lambda i: (i, 0)),
        out_shape=jax.ShapeDtypeStruct((R, ATTN_W), f32),
        compiler_params=_cparams(("arbitrary",)),
        name="attention",
    )(sink, q, k, k, k, k, v, v, v, v)


def _outproj_kernel(layer, attn_ref, u_ref, b_ref, c_ref, up_ref, cp_ref, un_ref, cn_ref, x_ref, tab_ref, gout_ref,
                    cw_ref, w_hbm, wr_ref, xo_ref, h2_ref, lg_ref, wbuf, stage, sems):
    i = pl.program_id(0)

    @pl.when(i == 0)
    def _():
        _load_weight_bf16(w_hbm.at[layer], wbuf, stage, sems)

    stream = jnp.where(i == 0, 0, 1)
    prev_ok = (i >= 2).astype(f32)
    next_ok = ((i >= 1) & (i < pl.num_programs(0) - 1)).astype(f32)
    w = c_ref[...] * u_ref[...]
    w_before = cp_ref[SUBLANES - 1:SUBLANES, :] * up_ref[SUBLANES - 1:SUBLANES, :] * prev_ok
    w_after = cn_ref[0:1, :] * un_ref[0:1, :] * next_ok
    rows = lax.broadcasted_iota(i32, (TM, 1), 0)
    w_prev = jnp.where(rows == 0, w_before, pltpu.roll(w, 1, 0))
    w_next = jnp.where(rows == TM - 1, w_after, pltpu.roll(w, TM - 1, 0))
    conv = b_ref[...] * (cw_ref[0:1, :] * w_prev + cw_ref[1:2, :] * w + cw_ref[2:3, :] * w_next)

    attn = attn_ref[...]
    a_n = attn * _rms_scale(attn) * gout_ref[0:1, :]
    c_n = conv * _rms_scale(conv) * gout_ref[1:2, :]
    cat = jnp.concatenate([a_n, c_n], axis=1).astype(bf16)
    mixed = jnp.dot(cat, wbuf[...], preferred_element_type=f32)
    xn = x_ref[...] + tab_ref[stream, 0:1, :] * mixed
    xo_ref[...] = xn

    h2 = xn * _rms_scale(xn) * tab_ref[stream, 1:2, :] + tab_ref[stream, 2:3, :]
    h2_ref[...] = h2
    hi = h2.astype(bf16)
    lo = (h2 - hi.astype(f32)).astype(bf16)
    lg_ref[...] = (jnp.dot(hi, wr_ref[0], preferred_element_type=f32)
                   + jnp.dot(lo, wr_ref[0], preferred_element_type=f32)
                   + jnp.dot(hi, wr_ref[1], preferred_element_type=f32))


def _outproj(layer, attn, u, b, c, x, tab, gout, cw, w_out, wr):
    halo = TM // SUBLANES
    last8 = R // SUBLANES - 1
    row = lambda i: (i, 0)
    prev8 = lambda i: (jnp.maximum(i * halo - 1, 0), 0)
    next8 = lambda i: (jnp.minimum((i + 1) * halo, last8), 0)
    return pl.pallas_call(
        functools.partial(_outproj_kernel, layer),
        grid=(NT,),
        in_specs=[pl.BlockSpec((TM, ATTN_W), row),
                  pl.BlockSpec((TM, CONV_W), row),
                  pl.BlockSpec((TM, CONV_W), row),
                  pl.BlockSpec((TM, CONV_W), row),
                  pl.BlockSpec((SUBLANES, CONV_W), prev8),
                  pl.BlockSpec((SUBLANES, CONV_W), prev8),
                  pl.BlockSpec((SUBLANES, CONV_W), next8),
                  pl.BlockSpec((SUBLANES, CONV_W), next8),
                  pl.BlockSpec((TM, D), row),
                  pl.BlockSpec((2, 3, D), lambda i: (0, 0, 0)),
                  pl.BlockSpec((2, CONV_W), lambda i: (0, 0)),
                  pl.BlockSpec((3, CONV_W), lambda i: (0, 0)),
                  pl.BlockSpec(memory_space=pl.ANY),
                  pl.BlockSpec((2, D, LANES), lambda i: (0, 0, 0))],
        out_specs=[pl.BlockSpec((TM, D), row),
                   pl.BlockSpec((TM, D), row),
                   pl.BlockSpec((TM, LANES), row)],
        out_shape=[jax.ShapeDtypeStruct((R, D), f32),
                   jax.ShapeDtypeStruct((R, D), f32),
                   jax.ShapeDtypeStruct((R, LANES), f32)],
        scratch_shapes=[pltpu.VMEM((D, D), bf16), pltpu.VMEM((2, WCH, D), f32), pltpu.SemaphoreType.DMA((2,))],
        compiler_params=_cparams(("arbitrary",)),
        name="outproj",
    )(attn, u, b, c, u, c, u, c, x, tab, gout, cw, w_out, wr)


def _router_kernel(bias_ref, lt_ref, upper_ref, lower_ref, ones_ref, e_ref, g_ref, d_ref, cnt_ref):
    score = [1.0 / (1.0 + jnp.exp(-lt_ref[e])) for e in range(N_EXPERTS)]
    sel = [score[e] + bias_ref[e] for e in range(N_EXPERTS)]

    def top2_sum(a, b, c, d):
        p, q = jnp.maximum(a, b), jnp.minimum(a, b)
        r, s = jnp.maximum(c, d), jnp.minimum(c, d)
        return jnp.maximum(p, r) + jnp.maximum(jnp.minimum(p, r), jnp.maximum(q, s))

    gscore = [top2_sum(*sel[EPG * g:EPG * (g + 1)]) for g in range(N_GROUPS)]
    best, gidx = gscore[0], jnp.zeros(gscore[0].shape, i32)
    for g in range(1, N_GROUPS):
        take = gscore[g] > best
        best = jnp.where(take, gscore[g], best)
        gidx = jnp.where(take, g, gidx)

    def pick_group(vals):
        out = []
        for j in range(EPG):
            v = vals[j]
            for g in range(1, N_GROUPS):
                v = jnp.where(gidx == g, vals[EPG * g + j], v)
            out.append(v)
        return out

    in_sel = pick_group(sel)
    in_score = pick_group(score)

    def argmax_first(vals, excluded):
        bv, bi = None, None
        for j in range(EPG):
            v = vals[j] if excluded is None else jnp.where(excluded == j, -jnp.inf, vals[j])
            if bv is None:
                bv, bi = v, jnp.zeros(v.shape, i32)
            else:
                take = v > bv
                bv = jnp.where(take, v, bv)
                bi = jnp.where(take, j, bi)
        return bi

    i1 = argmax_first(in_sel, None)
    i2 = argmax_first(in_sel, i1)

    def pick_local(vals, idx):
        v = vals[0]
        for j in range(1, EPG):
            v = jnp.where(idx == j, vals[j], v)
        return v

    s1, s2 = pick_local(in_score, i1), pick_local(in_score, i2)
    tot = s1 + s2
    e1 = gidx * EPG + i1
    e2 = gidx * EPG + i2
    e_ref[0], e_ref[1] = e1, e2
    g_ref[0], g_ref[1] = s1 / tot, s2 / tot

    tok = (lax.broadcasted_iota(i32, (RT, LANES), 0) * LANES + lax.broadcasted_iota(i32, (RT, LANES), 1))
    valid = tok < R
    onehot = [(((e1 == e) | (e2 == e)) & valid).astype(f32) for e in range(N_EXPERTS)]
    stack = jnp.concatenate(onehot, axis=0).astype(bf16)
    within = jnp.dot(stack, upper_ref[...], preferred_element_type=f32)
    rowtot = jnp.dot(stack, ones_ref[...], preferred_element_type=f32)
    d1 = jnp.zeros((RT, LANES), i32)
    d2 = jnp.zeros((RT, LANES), i32)
    seg_start = jnp.zeros((1, LANES), i32)
    for e in range(N_EXPERTS):
        rt_e = rowtot[e * RT:(e + 1) * RT]
        before = jnp.dot(lower_ref[...], rt_e.astype(bf16), preferred_element_type=f32)
        rank_e = within[e * RT:(e + 1) * RT] + before
        count_e = (before + rt_e)[RT - 1:RT, :]
        cnt_ref[e:e + 1, :] = count_e
        slot = seg_start + rank_e.astype(i32)
        d1 = jnp.where(e1 == e, slot, d1)
        d2 = jnp.where(e2 == e, slot, d2)
        blocks_e = jnp.right_shift(count_e.astype(i32) + (EBLK - 1), EBLK_SHIFT)
        seg_start = seg_start + jnp.left_shift(blocks_e, EBLK_SHIFT)
    d_ref[0], d_ref[1] = d1, d2


def _router(bias, logits_t, upper, lower, ones):
    full = lambda *shape: pl.BlockSpec(shape, lambda: (0,) * len(shape))
    return pl.pallas_call(
        _router_kernel,
        in_specs=[pl.BlockSpec(memory_space=pltpu.SMEM),
                  full(N_EXPERTS, RT, LANES), full(LANES, LANES), full(RT, RT), full(LANES, LANES)],
        out_specs=[full(2, RT, LANES), full(2, RT, LANES), full(2, RT, LANES), full(N_EXPERTS, LANES)],
        out_shape=[jax.ShapeDtypeStruct((2, RT, LANES), i32),
                   jax.ShapeDtypeStruct((2, RT, LANES), f32),
                   jax.ShapeDtypeStruct((2, RT, LANES), i32),
                   jax.ShapeDtypeStruct((N_EXPERTS, LANES), f32)],
        compiler_params=pltpu.CompilerParams(vmem_limit_bytes=VMEM_LIMIT),
        name="router",
    )(bias, logits_t, upper, lower, ones)


ROW_UNROLL = 8


def _dispatch_kernel(pend_ref, dest_ref, h_ref, xs_hbm, zbuf, sem, zsem):
    @pl.when(pl.program_id(0) == 0)
    def _():
        zbuf[...] = jnp.zeros_like(zbuf)

        def zero_copy(e):
            first = pl.multiple_of(pend_ref[e] - EBLK, EBLK)
            return pltpu.make_async_copy(zbuf, xs_hbm.at[pl.ds(first, EBLK)], zsem)

        def nonempty(e):
            return pend_ref[e] > (pend_ref[e - 1] if e else 0)

        for e in range(N_EXPERTS):
            pl.when(nonempty(e))(lambda e=e: zero_copy(e).start())
        for e in range(N_EXPERTS):
            pl.when(nonempty(e))(lambda e=e: zero_copy(e).wait())

        def tail_copy(j):
            return pltpu.make_async_copy(zbuf, xs_hbm.at[pl.ds(pl.multiple_of(j * EBLK, EBLK), EBLK)], zsem)

        first_unused = jnp.right_shift(pend_ref[N_EXPERTS - 1], EBLK_SHIFT)
        lax.fori_loop(first_unused, NBLK, lambda j, c: (tail_copy(j).start(), c)[1], 0)
        lax.fori_loop(first_unused, NBLK, lambda j, c: (tail_copy(j).wait(), c)[1], 0)

    def start(t, c):
        for k in range(2):
            pltpu.make_async_copy(h_ref.at[pl.ds(t, 1)], xs_hbm.at[pl.ds(dest_ref[0, 0, k * TM + t], 1)],
                                  sem).start(priority=k)
        return c

    def wait(t, c):
        for k in range(2):
            pltpu.make_async_copy(h_ref.at[pl.ds(0, 1)], xs_hbm.at[pl.ds(0, 1)], sem).wait()
        return c

    lax.fori_loop(0, TM, start, 0, unroll=ROW_UNROLL)
    lax.fori_loop(0, TM, wait, 0, unroll=ROW_UNROLL)


def _dispatch(pend, dest3, h2):
    return pl.pallas_call(
        _dispatch_kernel,
        grid_spec=pltpu.PrefetchScalarGridSpec(
            num_scalar_prefetch=1,
            grid=(NT,),
            in_specs=[pl.BlockSpec((1, 1, 2 * TM), lambda i, pend: (i, 0, 0), memory_space=pltpu.SMEM),
                      pl.BlockSpec((TM, D), lambda i, pend: (i, 0))],
            out_specs=pl.BlockSpec(memory_space=pl.ANY),
            scratch_shapes=[pltpu.VMEM((EBLK, D), f32), pltpu.SemaphoreType.DMA(()), pltpu.SemaphoreType.DMA(())]),
        out_shape=jax.ShapeDtypeStruct((NROWS, D), f32),
        compiler_params=_cparams(("arbitrary",)),
        name="dispatch",
    )(pend, dest3, h2)


NCH = 4
FCH = D_FF // NCH
NSTEPS = NBLK + (N_EXPERTS + 1) * NCH
(T_COMP, T_XBLK, T_CONV, T_CEXP, T_CCH, T_CSLOT, T_SLOT, T_ZERO, T_OBLK) = range(9)
T_STRIDE = 16


def _expert_schedule(counts):
    experts = jnp.arange(N_EXPERTS, dtype=i32)
    nb = (counts + EBLK - 1) // EBLK
    bstart = jnp.cumsum(nb) - nb
    nblocks = jnp.sum(nb)
    steps = jnp.maximum(nb, NCH)
    send = NCH + jnp.cumsum(steps)
    total = send[-1]
    s = jnp.arange(NSTEPS, dtype=i32)
    e_s = jnp.minimum(jnp.sum((s[:, None] >= send[None, :]).astype(i32), axis=1), N_EXPERTS - 1)
    onehot = (e_s[:, None] == experts[None, :]).astype(i32)
    pick = lambda table: jnp.sum(onehot * table[None, :], axis=1)
    i_s = s - pick(send - steps)
    nb_s = pick(nb)
    pre = s < NCH
    active = (s >= NCH) & (s < total)
    done = s >= total
    comp = active & (i_s < nb_s)
    xblk = jnp.clip(pick(bstart) + jnp.minimum(i_s, nb_s - 1), 0, nblocks - 1)
    xblk = jnp.where(pre, 0, jnp.where(done, nblocks - 1, xblk))
    has_next = e_s < N_EXPERTS - 1
    conv = pre | (active & (i_s < NCH) & has_next)
    cexp = jnp.where(pre, 0, jnp.minimum(e_s + 1, N_EXPERTS - 1))
    cch = jnp.where(pre, s, jnp.where(has_next & active, jnp.minimum(i_s, NCH - 1), NCH - 1))
    zblk = nblocks + (s - total)
    zero = done & (zblk < NBLK)
    oblk = jnp.where(done, jnp.minimum(zblk, NBLK - 1), xblk)
    cols = [comp, xblk, conv, cexp, cch, cexp % 2, e_s % 2, zero, oblk]
    cols = [col.astype(i32) for col in cols] + [jnp.zeros_like(s)] * (T_STRIDE - 9)
    return jnp.stack(cols, axis=1).reshape(NSTEPS * T_STRIDE)


def _expert_kernel(tab_ref, x_ref, g_ref, u_ref, d_ref, o_ref, wg_s, wu_s, wd_s):
    base = pl.program_id(0) * T_STRIDE

    @pl.when(tab_ref[base + T_CONV] == 1)
    def _():
        slot, c = tab_ref[base + T_CSLOT], tab_ref[base + T_CCH]
        wg_s[slot, c] = g_ref[0, 0].astype(bf16)
        wu_s[slot, c] = u_ref[0, 0].astype(bf16)
        wd_s[slot, c] = d_ref[0, 0].astype(bf16)

    @pl.when(tab_ref[base + T_COMP] == 1)
    def _():
        slot = tab_ref[base + T_SLOT]
        x = x_ref[...].astype(bf16)
        y = None
        for c in range(NCH):
            a = jnp.dot(x, wg_s[slot, c], preferred_element_type=f32)
            b = jnp.dot(x, wu_s[slot, c], preferred_element_type=f32)
            h = (a / (1.0 + jnp.exp(-a)) * b).astype(bf16)
            yc = jnp.dot(h, wd_s[slot, c], preferred_element_type=f32)
            y = yc if y is None else y + yc
        o_ref[...] = y

    @pl.when(tab_ref[base + T_ZERO] == 1)
    def _():
        o_ref[...] = jnp.zeros_like(o_ref)


def _experts(layer, schedule, xs, w_gate, w_up, w_down):
    at = lambda s, tab, col: tab[s * T_STRIDE + col]
    up_spec = pl.BlockSpec((1, 1, D, FCH), lambda s, tab: (layer, at(s, tab, T_CEXP), 0, at(s, tab, T_CCH)))
    return pl.pallas_call(
        _expert_kernel,
        grid_spec=pltpu.PrefetchScalarGridSpec(
            num_scalar_prefetch=1,
            grid=(NSTEPS,),
            in_specs=[pl.BlockSpec((EBLK, D), lambda s, tab: (at(s, tab, T_XBLK), 0)),
                      up_spec, up_spec,
                      pl.BlockSpec((1, 1, FCH, D),
                                   lambda s, tab: (layer, at(s, tab, T_CEXP), at(s, tab, T_CCH), 0))],
            out_specs=pl.BlockSpec((EBLK, D), lambda s, tab: (at(s, tab, T_OBLK), 0)),
            scratch_shapes=[pltpu.VMEM((2, NCH, D, FCH), bf16), pltpu.VMEM((2, NCH, D, FCH), bf16),
                            pltpu.VMEM((2, NCH, FCH, D), bf16)]),
        out_shape=jax.ShapeDtypeStruct((NROWS, D), f32),
        compiler_params=_cparams(("arbitrary",)),
        name="experts",
    )(schedule, xs, w_gate, w_up, w_down)


def _combine_kernel(first, dest_ref, dnext_ref, ys_hbm, x_ref, gates_ref, gate_ref, o_ref, ybuf, sems):
    i = pl.program_id(0)
    slot = jnp.bitwise_and(i, 1)
    stream = jnp.where(i + first == 0, 0, 1)

    def start_tile(idx_ref, s):
        def body(t, c):
            for k in range(2):
                pltpu.make_async_copy(ys_hbm.at[pl.ds(idx_ref[0, 0, k * TM + t], 1)],
                                      ybuf.at[s, k, pl.ds(t, 1)], sems.at[s]).start(priority=k)
            return c
        lax.fori_loop(0, TM, body, 0, unroll=ROW_UNROLL)

    @pl.when(i == 0)
    def _():
        start_tile(dest_ref, 0)

    @pl.when(i + 1 < pl.num_programs(0))
    def _():
        start_tile(dnext_ref, 1 - slot)

    def wait(t, c):
        for k in range(2):
            pltpu.make_async_copy(ys_hbm.at[pl.ds(0, 1)], ybuf.at[slot, k, pl.ds(0, 1)], sems.at[slot]).wait()
        return c

    lax.fori_loop(0, TM, wait, 0, unroll=ROW_UNROLL)
    y = ybuf[slot, 0] * gates_ref[:, 0:1] + ybuf[slot, 1] * gates_ref[:, 1:2]
    o_ref[...] = x_ref[...] + gate_ref[stream] * y


def _combine(first, dest3, ys, x, gates, gate2):
    row = lambda i: (i + first, 0)
    return pl.pallas_call(
        functools.partial(_combine_kernel, first),
        grid=(NT - first,),
        in_specs=[pl.BlockSpec((1, 1, 2 * TM), lambda i: (i + first, 0, 0), memory_space=pltpu.SMEM),
                  pl.BlockSpec((1, 1, 2 * TM), lambda i: (jnp.minimum(i + first + 1, NT - 1), 0, 0),
                               memory_space=pltpu.SMEM),
                  pl.BlockSpec(memory_space=pl.ANY),
                  pl.BlockSpec((TM, D), row),
                  pl.BlockSpec((TM, 2), row),
                  pl.BlockSpec((2, 1, D), lambda i: (0, 0, 0))],
        out_specs=pl.BlockSpec((TM, D), lambda i: (i, 0)),
        out_shape=jax.ShapeDtypeStruct((R - first * TM, D), f32),
        scratch_shapes=[pltpu.VMEM((2, 2, TM, D), f32), pltpu.SemaphoreType.DMA((2,))],
        compiler_params=_cparams(("arbitrary",)),
        name="combine",
    )(dest3, dest3, ys, x, gates, gate2)


def _rope_tables():
    half = HEAD_DIM // 2
    inv_freq = ROPE_THETA ** (-jnp.arange(0, half, 2, dtype=f32) / half)
    t = jnp.arange(SEQ)
    pos = jnp.stack([(t // GRID_W).astype(f32), (t % GRID_W).astype(f32)], axis=1)
    d = jnp.arange(LANES) % HEAD_DIM
    axis, sub = d // half, d % half
    ang = pos[:, axis] * inv_freq[sub % (half // 2)][None, :]
    cos, sin = jnp.cos(ang), jnp.sin(ang)
    first = (sub < half // 2)[None, :]
    sa = jnp.where(first, -sin, 0.0)
    sb = jnp.where(first, 0.0, sin)
    ident = jnp.ones((CTX, LANES), f32)
    zeros = jnp.zeros((CTX, LANES), f32)
    return (jnp.concatenate([ident, cos]), jnp.concatenate([zeros, sa]), jnp.concatenate([zeros, sb]))


def kernel(x, c, ctx, c_ctx, w_ada, b_ada, g_attn, w_in, q_norm_g, k_norm_g, sink, conv_w, g_out_attn, g_out_conv, w_out, g_ffn, w_router, router_bias, w_exp_gate, w_exp_up, w_exp_down):
    assert x.shape == (1, SEQ, D) and ctx.shape == (1, CTX, D)
    xa = jnp.concatenate([ctx[0], x[0]], axis=0)

    silu = lambda t: t * jax.nn.sigmoid(t)
    s_vec = jnp.stack([silu(c_ctx), silu(c[0])])
    mods = _adaln(jnp.broadcast_to(s_vec[:, :, None], (2, D, LANES)), w_ada, b_ada).reshape(DEPTH, 2, 6, D)

    cos, sa, sb = _rope_tables()
    lane = jnp.arange(LANES)
    ones_bd = (lane[:, None] // HEAD_DIM == lane[None, :] // HEAD_DIM).astype(bf16)
    upper = (lane[:, None] < lane[None, :]).astype(bf16)
    ones = jnp.ones((LANES, LANES), bf16)
    rt = jnp.arange(RT)
    lower = (rt[None, :] < rt[:, None]).astype(bf16)
    wr_pad = jnp.pad(w_router, ((0, 0), (0, LANES - N_EXPERTS)))
    wr_hi = wr_pad.astype(bf16)
    wr = jnp.stack([wr_hi, (wr_pad - wr_hi.astype(f32)).astype(bf16)])

    for l in range(DEPTH):
        mod = mods[l]
        ms1 = jnp.stack([g_attn[l] * (1.0 + mod[:, 1]), mod[:, 0]], axis=1)
        q, kv, u, b_gate, c_gate = _inproj(l, xa, ms1, w_in)
        g2 = jnp.stack([jnp.tile(q_norm_g[l] * HEAD_DIM ** -0.5, 2), jnp.tile(k_norm_g[l], 2)])
        q, k, v = _qkprep(q, kv, cos, sa, sb, g2, ones_bd)
        attn = _attention(sink[l], q, k, v)
        tab = jnp.stack([mod[:, 2], g_ffn[l] * (1.0 + mod[:, 4]), mod[:, 3]], axis=1)
        gout = jnp.stack([g_out_attn[l], g_out_conv[l]])
        xa, h2, logits = _outproj(l, attn, u, b_gate, c_gate, xa, tab, gout, conv_w[l], w_out, wr)

        lt = jnp.pad(logits[:, :N_EXPERTS].T, ((0, 0), (0, RT * LANES - R))).reshape(N_EXPERTS, RT, LANES)
        _, g_t, d_t, cnt = _router(router_bias, lt, upper, lower, ones)
        gates = g_t.reshape(2, RT * LANES)[:, :R].T
        dest = d_t.reshape(2, RT * LANES)[:, :R]
        dest3 = dest.reshape(2, NT, TM).transpose(1, 0, 2).reshape(NT, 1, 2 * TM)
        counts = cnt[:, 0].astype(i32)
        pend = jnp.cumsum((counts + EBLK - 1) // EBLK * EBLK)

        xs = _dispatch(pend, dest3, h2)
        ys = _experts(l, _expert_schedule(counts), xs, w_exp_gate, w_exp_up, w_exp_down)
        xa = _combine(1 if l == DEPTH - 1 else 0, dest3, ys, xa, gates, mod[:, 5][:, None, :])

    return xa[None]
```

```python
import functools

import jax
import jax.numpy as jnp
from jax import lax
from jax.experimental import pallas as pl
from jax.experimental.pallas import tpu as pltpu

f32 = jnp.float32
bf16 = jnp.bfloat16
i32 = jnp.int32

D = 2048
DEPTH = 4
SEQ = 8192
CTX = 256
R = CTX + SEQ
GRID_W = 64
ATTN_W = 1024
CONV_W = 1024
HEAD_DIM = 64
N_HEADS = 16
KV_W = 128
IN_COLS = ATTN_W + 2 * KV_W + 3 * CONV_W
ABLK = 128
WINDOW = 128
N_EXPERTS = 16
N_GROUPS = 4
EPG = 4
D_FF = 1024
EPS = 1e-6
NEG_INF = -1e30
ROPE_THETA = 10000.0

LANES = 128
SUBLANES = 8
TM = 256
NT = R // TM
EBLK = 256
EBLK_SHIFT = EBLK.bit_length() - 1
assert 1 << EBLK_SHIFT == EBLK
NBLK = -(-(2 * R + N_EXPERTS * (EBLK - 1)) // EBLK)
NROWS = NBLK * EBLK
RT = LANES
VMEM_LIMIT = 56 * 1024 * 1024


def _cparams(sem):
    return pltpu.CompilerParams(dimension_semantics=sem, vmem_limit_bytes=VMEM_LIMIT)


ADA_TK = 2048
ADA_TN = 1536


def _adaln_kernel(s_ref, w_ref, b_ref, o_ref, acc_ref):
    k = pl.program_id(2)

    @pl.when(k == 0)
    def _():
        acc_ref[...] = jnp.zeros_like(acc_ref)

    for j in range(ADA_TN // LANES):
        w = w_ref[0, :, j * LANES:(j + 1) * LANES]
        for v in range(2):
            p = (w * s_ref[v]).reshape(ADA_TK // SUBLANES, SUBLANES, LANES).sum(axis=0)
            acc_ref[v, :, j * LANES:(j + 1) * LANES] += p

    @pl.when(k == pl.num_programs(2) - 1)
    def _():
        for v in range(2):
            o_ref[0, v:v + 1, :] = acc_ref[v].sum(axis=0, keepdims=True) + b_ref[0]


def _adaln(s_bcast, w_ada, b_ada):
    return pl.pallas_call(
        _adaln_kernel,
        grid=(DEPTH, 6 * D // ADA_TN, D // ADA_TK),
        in_specs=[pl.BlockSpec((2, ADA_TK, LANES), lambda l, n, k: (0, k, 0)),
                  pl.BlockSpec((1, ADA_TK, ADA_TN), lambda l, n, k: (l, k, n)),
                  pl.BlockSpec((1, 1, ADA_TN), lambda l, n, k: (l, 0, n))],
        out_specs=pl.BlockSpec((1, 2, ADA_TN), lambda l, n, k: (l, 0, n)),
        out_shape=jax.ShapeDtypeStruct((DEPTH, 2, 6 * D), f32),
        scratch_shapes=[pltpu.VMEM((2, SUBLANES, ADA_TN), f32)],
        compiler_params=_cparams(("arbitrary", "arbitrary", "arbitrary")),
        name="adaln",
    )(s_bcast, w_ada, b_ada.reshape(DEPTH, 1, 6 * D))


def _rms_scale(x):
    return lax.rsqrt(jnp.mean(x * x, axis=-1, keepdims=True) + EPS)


WCH = 256


def _load_weight_bf16(w_hbm, wbuf, stage, sems):
    n = wbuf.shape[0] // WCH

    def chunk(c):
        return pltpu.make_async_copy(w_hbm.at[pl.ds(c * WCH, WCH)], stage.at[c % 2], sems.at[c % 2])

    chunk(0).start()
    for c in range(n):
        if c + 1 < n:
            chunk(c + 1).start()
        chunk(c).wait()
        wbuf[pl.ds(c * WCH, WCH), :] = stage[c % 2].astype(bf16)


Q0, K0, V0, U0, B0, C0 = 0, ATTN_W, ATTN_W + KV_W, ATTN_W + 2 * KV_W, ATTN_W + 2 * KV_W + CONV_W, \
    ATTN_W + 2 * KV_W + 2 * CONV_W


def _inproj_kernel(layer, x_ref, ms_ref, w_hbm, q_ref, kv_ref, u_ref, b_ref, c_ref, wbuf, stage, sems):
    @pl.when(pl.program_id(0) == 0)
    def _():
        _load_weight_bf16(w_hbm.at[layer], wbuf, stage, sems)

    stream = jnp.where(pl.program_id(0) == 0, 0, 1)
    x = x_ref[...]
    gain = ms_ref[stream, 0:1, :]
    shift = ms_ref[stream, 1:2, :]
    h = (x * _rms_scale(x) * gain + shift).astype(bf16)
    p = jnp.dot(h, wbuf[...], preferred_element_type=f32)
    q_ref[...] = p[:, Q0:K0]
    kv_ref[...] = p[:, K0:U0]
    u_ref[...] = p[:, U0:B0]
    b_ref[...] = p[:, B0:C0]
    c_ref[...] = p[:, C0:]


def _inproj(layer, x, ms, w_in):
    row = lambda i: (i, 0)
    widths = (ATTN_W, 2 * KV_W, CONV_W, CONV_W, CONV_W)
    return pl.pallas_call(
        functools.partial(_inproj_kernel, layer),
        grid=(NT,),
        in_specs=[pl.BlockSpec((TM, D), row),
                  pl.BlockSpec((2, 2, D), lambda i: (0, 0, 0)),
                  pl.BlockSpec(memory_space=pl.ANY)],
        out_specs=[pl.BlockSpec((TM, n), row) for n in widths],
        out_shape=[jax.ShapeDtypeStruct((R, n), f32) for n in widths],
        scratch_shapes=[pltpu.VMEM((D, IN_COLS), bf16), pltpu.VMEM((2, WCH, IN_COLS), f32),
                        pltpu.SemaphoreType.DMA((2,))],
        compiler_params=_cparams(("arbitrary",)),
        name="inproj",
    )(x, ms, w_in)


def _qkprep_kernel(q_ref, k_ref, v_ref, cos_ref, sa_ref, sb_ref, g_ref, ones_ref, qo_ref, ko_ref, vo_ref):
    cos, sa, sb = cos_ref[...], sa_ref[...], sb_ref[...]
    ones = ones_ref[...]

    def norm_rope(xc, g):
        sq = xc * xc
        hi = sq.astype(bf16)
        lo = (sq - hi.astype(f32)).astype(bf16)
        ss = (jnp.dot(hi, ones, preferred_element_type=f32)
              + jnp.dot(lo, ones, preferred_element_type=f32))
        y = xc * lax.rsqrt(ss * (1.0 / HEAD_DIM) + EPS) * g
        return y * cos + pltpu.roll(y, LANES - 16, 1) * sa + pltpu.roll(y, 16, 1) * sb

    for j in range(ATTN_W // LANES):
        sl = slice(j * LANES, (j + 1) * LANES)
        qo_ref[:, sl] = norm_rope(q_ref[:, sl], g_ref[0:1, :]).astype(bf16)
    ko_ref[...] = norm_rope(k_ref[...], g_ref[1:2, :]).astype(bf16)
    vo_ref[...] = v_ref[...].astype(bf16)


def _qkprep(q, kv, cos, sa, sb, g2, ones_bd):
    row = lambda i: (i, 0)
    return pl.pallas_call(
        _qkprep_kernel,
        grid=(NT,),
        in_specs=[pl.BlockSpec((TM, ATTN_W), row),
                  pl.BlockSpec((TM, KV_W), lambda i: (i, 0)),
                  pl.BlockSpec((TM, KV_W), lambda i: (i, 1)),
                  pl.BlockSpec((TM, LANES), row),
                  pl.BlockSpec((TM, LANES), row),
                  pl.BlockSpec((TM, LANES), row),
                  pl.BlockSpec((2, LANES), lambda i: (0, 0)),
                  pl.BlockSpec((LANES, LANES), lambda i: (0, 0))],
        out_specs=[pl.BlockSpec((TM, ATTN_W), row),
                   pl.BlockSpec((TM, KV_W), row),
                   pl.BlockSpec((TM, KV_W), row)],
        out_shape=[jax.ShapeDtypeStruct((R, ATTN_W), bf16),
                   jax.ShapeDtypeStruct((R, KV_W), bf16),
                   jax.ShapeDtypeStruct((R, KV_W), bf16)],
        compiler_params=_cparams(("arbitrary",)),
        name="qkprep",
    )(q, kv, kv, cos, sa, sb, g2, ones_bd)


NQB = R // ABLK
FIRST_LAT = CTX // ABLK
NKEYS = CTX + 3 * ABLK


def _attn_kernel(sink_ref, q_ref, kc_ref, kp_ref, kk_ref, kn_ref, vc_ref, vp_ref, vk_ref, vn_ref, o_ref):
    i = pl.program_id(0)
    kf = jnp.concatenate([kc_ref[...], kp_ref[...], kk_ref[...], kn_ref[...]], axis=0).astype(f32)
    vf = jnp.concatenate([vc_ref[...], vp_ref[...], vk_ref[...], vn_ref[...]], axis=0).astype(f32)
    lane = lax.broadcasted_iota(i32, (NKEYS, LANES), 1)
    low = lane < HEAD_DIM
    kr = pltpu.roll(kf, HEAD_DIM, 1)
    vr = pltpu.roll(vf, HEAD_DIM, 1)
    zero = jnp.zeros_like(kf)
    k_lo = (jnp.where(low, kf, zero), jnp.where(low, kr, zero))
    k_hi = (jnp.where(low, zero, kr), jnp.where(low, zero, kf))
    v_lo = (jnp.where(low, vf, zero), jnp.where(low, vr, zero))
    v_hi = (jnp.where(low, zero, vr), jnp.where(low, zero, vf))

    qi = lax.broadcasted_iota(i32, (ABLK, NKEYS), 0)
    col = lax.broadcasted_iota(i32, (ABLK, NKEYS), 1)
    kj = col - CTX
    in_window = jnp.abs(qi + ABLK - kj) <= WINDOW
    key_block = i - 1 + jnp.right_shift(kj, 7)
    first_ok = jnp.where(i >= FIRST_LAT, FIRST_LAT, NQB)
    mask = (col < CTX) | (in_window & (key_block >= first_ok) & (key_block < NQB))
    lane_q = lax.broadcasted_iota(i32, (ABLK, LANES), 1)

    contract_last = (((1,), (1,)), ((), ()))
    for h in range(2):
        qs = jnp.concatenate([q_ref[:, (h * 4 + p) * LANES:(h * 4 + p + 1) * LANES] for p in range(4)], axis=0)
        s_par = (lax.dot_general(qs, k_lo[h].astype(bf16), contract_last, preferred_element_type=f32),
                 lax.dot_general(qs, k_hi[h].astype(bf16), contract_last, preferred_element_type=f32))
        probs = ([], [])
        rden = ([], [])
        for par in range(2):
            for p in range(4):
                sink = sink_ref[h * 8 + 2 * p + par]
                s = jnp.where(mask, s_par[par][p * ABLK:(p + 1) * ABLK], NEG_INF)
                m = jnp.maximum(jnp.max(s, axis=-1, keepdims=True), sink)
                e = jnp.exp(s - m)
                den = jnp.sum(e, axis=-1, keepdims=True) + jnp.exp(sink - m)
                probs[par].append(e.astype(bf16))
                rden[par].append(1.0 / den)
        o = (jnp.dot(jnp.concatenate(probs[0], axis=0), v_lo[h].astype(bf16), preferred_element_type=f32)
             + jnp.dot(jnp.concatenate(probs[1], axis=0), v_hi[h].astype(bf16), preferred_element_type=f32))
        for p in range(4):
            scale = jnp.where(lane_q < HEAD_DIM, rden[0][p], rden[1][p])
            o_ref[:, (h * 4 + p) * LANES:(h * 4 + p + 1) * LANES] = o[p * ABLK:(p + 1) * ABLK] * scale


def _attention(sink, q, k, v):
    clip = lambda b: jnp.clip(b, FIRST_LAT, NQB - 1)
    kv_specs = [pl.BlockSpec((CTX, KV_W), lambda i: (0, 0)),
                pl.BlockSpec((ABLK, KV_W), lambda i: (clip(i - 1), 0)),
                pl.BlockSpec((ABLK, KV_W), lambda i: (i, 0)),
                pl.BlockSpec((ABLK, KV_W), lambda i: (clip(i + 1), 0))]
    return pl.pallas_call(
        _attn_kernel,
        grid=(NQB,),
        in_specs=[pl.BlockSpec(memory_space=pltpu.SMEM),
                  pl.BlockSpec((ABLK, ATTN_W), lambda i: (i, 0))] + kv_specs + kv_specs,
        out_specs=pl.BlockSpec((ABLK, ATTN_W), lambda i: (i, 0)),
        out_shape=jax.ShapeDtypeStruct((R, ATTN_W), f32),
        compiler_params=_cparams(("arbitrary",)),
        name="attention",
    )(sink, q, k, k, k, k, v, v, v, v)


def _outproj_kernel(layer, attn_ref, u_ref, b_ref, c_ref, up_ref, cp_ref, un_ref, cn_ref, x_ref, tab_ref, gout_ref,
                    cw_ref, w_hbm, wr_ref, xo_ref, h2_ref, lg_ref, wbuf, stage, sems):
    i = pl.program_id(0)

    @pl.when(i == 0)
    def _():
        _load_weight_bf16(w_hbm.at[layer], wbuf, stage, sems)

    stream = jnp.where(i == 0, 0, 1)
    prev_ok = (i >= 2).astype(f32)
    next_ok = ((i >= 1) & (i < pl.num_programs(0) - 1)).astype(f32)
    w = c_ref[...] * u_ref[...]
    w_before = cp_ref[SUBLANES - 1:SUBLANES, :] * up_ref[SUBLANES - 1:SUBLANES, :] * prev_ok
    w_after = cn_ref[0:1, :] * un_ref[0:1, :] * next_ok
    rows = lax.broadcasted_iota(i32, (TM, 1), 0)
    w_prev = jnp.where(rows == 0, w_before, pltpu.roll(w, 1, 0))
    w_next = jnp.where(rows == TM - 1, w_after, pltpu.roll(w, TM - 1, 0))
    conv = b_ref[...] * (cw_ref[0:1, :] * w_prev + cw_ref[1:2, :] * w + cw_ref[2:3, :] * w_next)

    attn = attn_ref[...]
    a_n = attn * _rms_scale(attn) * gout_ref[0:1, :]
    c_n = conv * _rms_scale(conv) * gout_ref[1:2, :]
    cat = jnp.concatenate([a_n, c_n], axis=1).astype(bf16)
    mixed = jnp.dot(cat, wbuf[...], preferred_element_type=f32)
    xn = x_ref[...] + tab_ref[stream, 0:1, :] * mixed
    xo_ref[...] = xn

    h2 = xn * _rms_scale(xn) * tab_ref[stream, 1:2, :] + tab_ref[stream, 2:3, :]
    h2_ref[...] = h2
    hi = h2.astype(bf16)
    lo = (h2 - hi.astype(f32)).astype(bf16)
    lg_ref[...] = (jnp.dot(hi, wr_ref[0], preferred_element_type=f32)
                   + jnp.dot(lo, wr_ref[0], preferred_element_type=f32)
                   + jnp.dot(hi, wr_ref[1], preferred_element_type=f32))


def _outproj(layer, attn, u, b, c, x, tab, gout, cw, w_out, wr):
    halo = TM // SUBLANES
    last8 = R // SUBLANES - 1
    row = lambda i: (i, 0)
    prev8 = lambda i: (jnp.maximum(i * halo - 1, 0), 0)
    next8 = lambda i: (jnp.minimum((i + 1) * halo, last8), 0)
    return pl.pallas_call(
        functools.partial(_outproj_kernel, layer),
        grid=(NT,),
        in_specs=[pl.BlockSpec((TM, ATTN_W), row),
                  pl.BlockSpec((TM, CONV_W), row),
                  pl.BlockSpec((TM, CONV_W), row),
                  pl.BlockSpec((TM, CONV_W), row),
                  pl.BlockSpec((SUBLANES, CONV_W), prev8),
                  pl.BlockSpec((SUBLANES, CONV_W), prev8),
                  pl.BlockSpec((SUBLANES, CONV_W), next8),
                  pl.BlockSpec((SUBLANES, CONV_W), next8),
                  pl.BlockSpec((TM, D), row),
                  pl.BlockSpec((2, 3, D), lambda i: (0, 0, 0)),
                  pl.BlockSpec((2, CONV_W), lambda i: (0, 0)),
                  pl.BlockSpec((3, CONV_W), lambda i: (0, 0)),
                  pl.BlockSpec(memory_space=pl.ANY),
                  pl.BlockSpec((2, D, LANES), lambda i: (0, 0, 0))],
        out_specs=[pl.BlockSpec((TM, D), row),
                   pl.BlockSpec((TM, D), row),
                   pl.BlockSpec((TM, LANES), row)],
        out_shape=[jax.ShapeDtypeStruct((R, D), f32),
                   jax.ShapeDtypeStruct((R, D), f32),
                   jax.ShapeDtypeStruct((R, LANES), f32)],
        scratch_shapes=[pltpu.VMEM((D, D), bf16), pltpu.VMEM((2, WCH, D), f32), pltpu.SemaphoreType.DMA((2,))],
        compiler_params=_cparams(("arbitrary",)),
        name="outproj",
    )(attn, u, b, c, u, c, u, c, x, tab, gout, cw, w_out, wr)


def _router_kernel(bias_ref, lt_ref, upper_ref, lower_ref, ones_ref, e_ref, g_ref, d_ref, cnt_ref):
    score = [1.0 / (1.0 + jnp.exp(-lt_ref[e])) for e in range(N_EXPERTS)]
    sel = [score[e] + bias_ref[e] for e in range(N_EXPERTS)]

    def top2_sum(a, b, c, d):
        p, q = jnp.maximum(a, b), jnp.minimum(a, b)
        r, s = jnp.maximum(c, d), jnp.minimum(c, d)
        return jnp.maximum(p, r) + jnp.maximum(jnp.minimum(p, r), jnp.maximum(q, s))

    gscore = [top2_sum(*sel[EPG * g:EPG * (g + 1)]) for g in range(N_GROUPS)]
    best, gidx = gscore[0], jnp.zeros(gscore[0].shape, i32)
    for g in range(1, N_GROUPS):
        take = gscore[g] > best
        best = jnp.where(take, gscore[g], best)
        gidx = jnp.where(take, g, gidx)

    def pick_group(vals):
        out = []
        for j in range(EPG):
            v = vals[j]
            for g in range(1, N_GROUPS):
                v = jnp.where(gidx == g, vals[EPG * g + j], v)
            out.append(v)
        return out

    in_sel = pick_group(sel)
    in_score = pick_group(score)

    def argmax_first(vals, excluded):
        bv, bi = None, None
        for j in range(EPG):
            v = vals[j] if excluded is None else jnp.where(excluded == j, -jnp.inf, vals[j])
            if bv is None:
                bv, bi = v, jnp.zeros(v.shape, i32)
            else:
                take = v > bv
                bv = jnp.where(take, v, bv)
                bi = jnp.where(take, j, bi)
        return bi

    i1 = argmax_first(in_sel, None)
    i2 = argmax_first(in_sel, i1)

    def pick_local(vals, idx):
        v = vals[0]
        for j in range(1, EPG):
            v = jnp.where(idx == j, vals[j], v)
        return v

    s1, s2 = pick_local(in_score, i1), pick_local(in_score, i2)
    tot = s1 + s2
    e1 = gidx * EPG + i1
    e2 = gidx * EPG + i2
    e_ref[0], e_ref[1] = e1, e2
    g_ref[0], g_ref[1] = s1 / tot, s2 / tot

    tok = (lax.broadcasted_iota(i32, (RT, LANES), 0) * LANES + lax.broadcasted_iota(i32, (RT, LANES), 1))
    valid = tok < R
    onehot = [(((e1 == e) | (e2 == e)) & valid).astype(f32) for e in range(N_EXPERTS)]
    stack = jnp.concatenate(onehot, axis=0).astype(bf16)
    within = jnp.dot(stack, upper_ref[...], preferred_element_type=f32)
    rowtot = jnp.dot(stack, ones_ref[...], preferred_element_type=f32)
    d1 = jnp.zeros((RT, LANES), i32)
    d2 = jnp.zeros((RT, LANES), i32)
    seg_start = jnp.zeros((1, LANES), i32)
    for e in range(N_EXPERTS):
        rt_e = rowtot[e * RT:(e + 1) * RT]
        before = jnp.dot(lower_ref[...], rt_e.astype(bf16), preferred_element_type=f32)
        rank_e = within[e * RT:(e + 1) * RT] + before
        count_e = (before + rt_e)[RT - 1:RT, :]
        cnt_ref[e:e + 1, :] = count_e
        slot = seg_start + rank_e.astype(i32)
        d1 = jnp.where(e1 == e, slot, d1)
        d2 = jnp.where(e2 == e, slot, d2)
        blocks_e = jnp.right_shift(count_e.astype(i32) + (EBLK - 1), EBLK_SHIFT)
        seg_start = seg_start + jnp.left_shift(blocks_e, EBLK_SHIFT)
    d_ref[0], d_ref[1] = d1, d2


def _router(bias, logits_t, upper, lower, ones):
    full = lambda *shape: pl.BlockSpec(shape, lambda: (0,) * len(shape))
    return pl.pallas_call(
        _router_kernel,
        in_specs=[pl.BlockSpec(memory_space=pltpu.SMEM),
                  full(N_EXPERTS, RT, LANES), full(LANES, LANES), full(RT, RT), full(LANES, LANES)],
        out_specs=[full(2, RT, LANES), full(2, RT, LANES), full(2, RT, LANES), full(N_EXPERTS, LANES)],
        out_shape=[jax.ShapeDtypeStruct((2, RT, LANES), i32),
                   jax.ShapeDtypeStruct((2, RT, LANES), f32),
                   jax.ShapeDtypeStruct((2, RT, LANES), i32),
                   jax.ShapeDtypeStruct((N_EXPERTS, LANES), f32)],
        compiler_params=pltpu.CompilerParams(vmem_limit_bytes=VMEM_LIMIT),
        name="router",
    )(bias, logits_t, upper, lower, ones)


ROW_UNROLL = TM


HSLOTS = 3


def _dispatch_kernel(pend_ref, dest_ref, h_hbm, xs_hbm, hbuf, zbuf, lsem, rsem, zsem):
    i = pl.program_id(0)
    last = pl.num_programs(0) - 1
    slot = lax.rem(i, HSLOTS)
    prev_slot = lax.rem(i + (HSLOTS - 1), HSLOTS)
    next_slot = lax.rem(i + 1, HSLOTS)

    def load(tile, s):
        rows = pl.ds(pl.multiple_of(tile * TM, TM), TM)
        return pltpu.make_async_copy(h_hbm.at[rows], hbuf.at[s], lsem.at[s])

    @pl.when(i == 0)
    def _():
        load(0, 0).start()
        zbuf[...] = jnp.zeros_like(zbuf)

        def zero_copy(e):
            first = pl.multiple_of(pend_ref[e] - EBLK, EBLK)
            return pltpu.make_async_copy(zbuf, xs_hbm.at[pl.ds(first, EBLK)], zsem)

        def nonempty(e):
            return pend_ref[e] > (pend_ref[e - 1] if e else 0)

        for e in range(N_EXPERTS):
            pl.when(nonempty(e))(lambda e=e: zero_copy(e).start())
        for e in range(N_EXPERTS):
            pl.when(nonempty(e))(lambda e=e: zero_copy(e).wait())

        def tail_copy(j):
            return pltpu.make_async_copy(zbuf, xs_hbm.at[pl.ds(pl.multiple_of(j * EBLK, EBLK), EBLK)], zsem)

        first_unused = jnp.right_shift(pend_ref[N_EXPERTS - 1], EBLK_SHIFT)
        lax.fori_loop(first_unused, NBLK, lambda j, c: (tail_copy(j).start(), c)[1], 0)
        lax.fori_loop(first_unused, NBLK, lambda j, c: (tail_copy(j).wait(), c)[1], 0)

    @pl.when(i < last)
    def _():
        load(i + 1, next_slot).start()

    load(i, slot).wait()

    def start(t, c):
        for k in range(2):
            pltpu.make_async_copy(hbuf.at[slot, pl.ds(t, 1)], xs_hbm.at[pl.ds(dest_ref[0, 0, k * TM + t], 1)],
                                  rsem.at[slot]).start(priority=k)
        return c

    def drain(s):
        def wait(t, c):
            for k in range(2):
                pltpu.make_async_copy(hbuf.at[s, pl.ds(0, 1)], xs_hbm.at[pl.ds(0, 1)], rsem.at[s]).wait()
            return c
        lax.fori_loop(0, TM, wait, 0, unroll=ROW_UNROLL)

    lax.fori_loop(0, TM, start, 0, unroll=ROW_UNROLL)
    pl.when(i > 0)(lambda: drain(prev_slot))
    pl.when(i == last)(lambda: drain(slot))


def _dispatch(pend, dest3, h2):
    return pl.pallas_call(
        _dispatch_kernel,
        grid_spec=pltpu.PrefetchScalarGridSpec(
            num_scalar_prefetch=1,
            grid=(NT,),
            in_specs=[pl.BlockSpec((1, 1, 2 * TM), lambda i, pend: (i, 0, 0), memory_space=pltpu.SMEM),
                      pl.BlockSpec(memory_space=pl.ANY)],
            out_specs=pl.BlockSpec(memory_space=pl.ANY),
            scratch_shapes=[pltpu.VMEM((HSLOTS, TM, D), f32), pltpu.VMEM((EBLK, D), f32),
                            pltpu.SemaphoreType.DMA((HSLOTS,)), pltpu.SemaphoreType.DMA((HSLOTS,)),
                            pltpu.SemaphoreType.DMA(())]),
        out_shape=jax.ShapeDtypeStruct((NROWS, D), f32),
        compiler_params=_cparams(("arbitrary",)),
        name="dispatch",
    )(pend, dest3, h2)


NCH = 4
FCH = D_FF // NCH
NSTEPS = NBLK + (N_EXPERTS + 1) * NCH
(T_COMP, T_XBLK, T_CONV, T_CEXP, T_CCH, T_CSLOT, T_SLOT, T_ZERO, T_OBLK) = range(9)
T_STRIDE = 16


def _expert_schedule(counts):
    experts = jnp.arange(N_EXPERTS, dtype=i32)
    nb = (counts + EBLK - 1) // EBLK
    bstart = jnp.cumsum(nb) - nb
    nblocks = jnp.sum(nb)
    steps = jnp.maximum(nb, NCH)
    send = NCH + jnp.cumsum(steps)
    total = send[-1]
    s = jnp.arange(NSTEPS, dtype=i32)
    e_s = jnp.minimum(jnp.sum((s[:, None] >= send[None, :]).astype(i32), axis=1), N_EXPERTS - 1)
    onehot = (e_s[:, None] == experts[None, :]).astype(i32)
    pick = lambda table: jnp.sum(onehot * table[None, :], axis=1)
    i_s = s - pick(send - steps)
    nb_s = pick(nb)
    pre = s < NCH
    active = (s >= NCH) & (s < total)
    done = s >= total
    comp = active & (i_s < nb_s)
    xblk = jnp.clip(pick(bstart) + jnp.minimum(i_s, nb_s - 1), 0, nblocks - 1)
    xblk = jnp.where(pre, 0, jnp.where(done, nblocks - 1, xblk))
    has_next = e_s < N_EXPERTS - 1
    conv = pre | (active & (i_s < NCH) & has_next)
    cexp = jnp.where(pre, 0, jnp.minimum(e_s + 1, N_EXPERTS - 1))
    cch = jnp.where(pre, s, jnp.where(has_next & active, jnp.minimum(i_s, NCH - 1), NCH - 1))
    zblk = nblocks + (s - total)
    zero = done & (zblk < NBLK)
    oblk = jnp.where(done, jnp.minimum(zblk, NBLK - 1), xblk)
    cols = [comp, xblk, conv, cexp, cch, cexp % 2, e_s % 2, zero, oblk]
    cols = [col.astype(i32) for col in cols] + [jnp.zeros_like(s)] * (T_STRIDE - 9)
    return jnp.stack(cols, axis=1).reshape(NSTEPS * T_STRIDE)


def _expert_kernel(tab_ref, x_ref, g_ref, u_ref, d_ref, o_ref, wg_s, wu_s, wd_s):
    base = pl.program_id(0) * T_STRIDE

    @pl.when(tab_ref[base + T_CONV] == 1)
    def _():
        slot, c = tab_ref[base + T_CSLOT], tab_ref[base + T_CCH]
        wg_s[slot, c] = g_ref[0, 0].astype(bf16)
        wu_s[slot, c] = u_ref[0, 0].astype(bf16)
        wd_s[slot, c] = d_ref[0, 0].astype(bf16)

    @pl.when(tab_ref[base + T_COMP] == 1)
    def _():
        slot = tab_ref[base + T_SLOT]
        x = x_ref[...].astype(bf16)
        y = None
        for c in range(NCH):
            a = jnp.dot(x, wg_s[slot, c], preferred_element_type=f32)
            b = jnp.dot(x, wu_s[slot, c], preferred_element_type=f32)
            h = (a / (1.0 + jnp.exp(-a)) * b).astype(bf16)
            yc = jnp.dot(h, wd_s[slot, c], preferred_element_type=f32)
            y = yc if y is None else y + yc
        o_ref[...] = y

    @pl.when(tab_ref[base + T_ZERO] == 1)
    def _():
        o_ref[...] = jnp.zeros_like(o_ref)


def _experts(layer, schedule, xs, w_gate, w_up, w_down):
    at = lambda s, tab, col: tab[s * T_STRIDE + col]
    up_spec = pl.BlockSpec((1, 1, D, FCH), lambda s, tab: (layer, at(s, tab, T_CEXP), 0, at(s, tab, T_CCH)))
    return pl.pallas_call(
        _expert_kernel,
        grid_spec=pltpu.PrefetchScalarGridSpec(
            num_scalar_prefetch=1,
            grid=(NSTEPS,),
            in_specs=[pl.BlockSpec((EBLK, D), lambda s, tab: (at(s, tab, T_XBLK), 0)),
                      up_spec, up_spec,
                      pl.BlockSpec((1, 1, FCH, D),
                                   lambda s, tab: (layer, at(s, tab, T_CEXP), at(s, tab, T_CCH), 0))],
            out_specs=pl.BlockSpec((EBLK, D), lambda s, tab: (at(s, tab, T_OBLK), 0)),
            scratch_shapes=[pltpu.VMEM((2, NCH, D, FCH), bf16), pltpu.VMEM((2, NCH, D, FCH), bf16),
                            pltpu.VMEM((2, NCH, FCH, D), bf16)]),
        out_shape=jax.ShapeDtypeStruct((NROWS, D), f32),
        compiler_params=_cparams(("arbitrary",)),
        name="experts",
    )(schedule, xs, w_gate, w_up, w_down)


def _combine_kernel(first, dest_ref, dnext_ref, ys_hbm, x_ref, gates_ref, gate_ref, o_ref, ybuf, sems):
    i = pl.program_id(0)
    slot = jnp.bitwise_and(i, 1)
    stream = jnp.where(i + first == 0, 0, 1)

    def start_tile(idx_ref, s):
        def body(t, c):
            for k in range(2):
                pltpu.make_async_copy(ys_hbm.at[pl.ds(idx_ref[0, 0, k * TM + t], 1)],
                                      ybuf.at[s, k, pl.ds(t, 1)], sems.at[s]).start(priority=k)
            return c
        lax.fori_loop(0, TM, body, 0, unroll=ROW_UNROLL)

    @pl.when(i == 0)
    def _():
        start_tile(dest_ref, 0)

    @pl.when(i + 1 < pl.num_programs(0))
    def _():
        start_tile(dnext_ref, 1 - slot)

    def wait(t, c):
        for k in range(2):
            pltpu.make_async_copy(ys_hbm.at[pl.ds(0, 1)], ybuf.at[slot, k, pl.ds(0, 1)], sems.at[slot]).wait()
        return c

    lax.fori_loop(0, TM, wait, 0, unroll=ROW_UNROLL)
    y = ybuf[slot, 0] * gates_ref[:, 0:1] + ybuf[slot, 1] * gates_ref[:, 1:2]
    o_ref[...] = x_ref[...] + gate_ref[stream] * y


def _combine(first, dest3, ys, x, gates, gate2):
    row = lambda i: (i + first, 0)
    return pl.pallas_call(
        functools.partial(_combine_kernel, first),
        grid=(NT - first,),
        in_specs=[pl.BlockSpec((1, 1, 2 * TM), lambda i: (i + first, 0, 0), memory_space=pltpu.SMEM),
                  pl.BlockSpec((1, 1, 2 * TM), lambda i: (jnp.minimum(i + first + 1, NT - 1), 0, 0),
                               memory_space=pltpu.SMEM),
                  pl.BlockSpec(memory_space=pl.ANY),
                  pl.BlockSpec((TM, D), row),
                  pl.BlockSpec((TM, 2), row),
                  pl.BlockSpec((2, 1, D), lambda i: (0, 0, 0))],
        out_specs=pl.BlockSpec((TM, D), lambda i: (i, 0)),
        out_shape=jax.ShapeDtypeStruct((R - first * TM, D), f32),
        scratch_shapes=[pltpu.VMEM((2, 2, TM, D), f32), pltpu.SemaphoreType.DMA((2,))],
        compiler_params=_cparams(("arbitrary",)),
        name="combine",
    )(dest3, dest3, ys, x, gates, gate2)


def _rope_tables():
    half = HEAD_DIM // 2
    inv_freq = ROPE_THETA ** (-jnp.arange(0, half, 2, dtype=f32) / half)
    t = jnp.arange(SEQ)
    pos = jnp.stack([(t // GRID_W).astype(f32), (t % GRID_W).astype(f32)], axis=1)
    d = jnp.arange(LANES) % HEAD_DIM
    axis, sub = d // half, d % half
    ang = pos[:, axis] * inv_freq[sub % (half // 2)][None, :]
    cos, sin = jnp.cos(ang), jnp.sin(ang)
    first = (sub < half // 2)[None, :]
    sa = jnp.where(first, -sin, 0.0)
    sb = jnp.where(first, 0.0, sin)
    ident = jnp.ones((CTX, LANES), f32)
    zeros = jnp.zeros((CTX, LANES), f32)
    return (jnp.concatenate([ident, cos]), jnp.concatenate([zeros, sa]), jnp.concatenate([zeros, sb]))


def kernel(x, c, ctx, c_ctx, w_ada, b_ada, g_attn, w_in, q_norm_g, k_norm_g, sink, conv_w, g_out_attn, g_out_conv, w_out, g_ffn, w_router, router_bias, w_exp_gate, w_exp_up, w_exp_down):
    assert x.shape == (1, SEQ, D) and ctx.shape == (1, CTX, D)
    xa = jnp.concatenate([ctx[0], x[0]], axis=0)

    silu = lambda t: t * jax.nn.sigmoid(t)
    s_vec = jnp.stack([silu(c_ctx), silu(c[0])])
    mods = _adaln(jnp.broadcast_to(s_vec[:, :, None], (2, D, LANES)), w_ada, b_ada).reshape(DEPTH, 2, 6, D)

    cos, sa, sb = _rope_tables()
    lane = jnp.arange(LANES)
    ones_bd = (lane[:, None] // HEAD_DIM == lane[None, :] // HEAD_DIM).astype(bf16)
    upper = (lane[:, None] < lane[None, :]).astype(bf16)
    ones = jnp.ones((LANES, LANES), bf16)
    rt = jnp.arange(RT)
    lower = (rt[None, :] < rt[:, None]).astype(bf16)
    wr_pad = jnp.pad(w_router, ((0, 0), (0, LANES - N_EXPERTS)))
    wr_hi = wr_pad.astype(bf16)
    wr = jnp.stack([wr_hi, (wr_pad - wr_hi.astype(f32)).astype(bf16)])

    for l in range(DEPTH):
        mod = mods[l]
        ms1 = jnp.stack([g_attn[l] * (1.0 + mod[:, 1]), mod[:, 0]], axis=1)
        q, kv, u, b_gate, c_gate = _inproj(l, xa, ms1, w_in)
        g2 = jnp.stack([jnp.tile(q_norm_g[l] * HEAD_DIM ** -0.5, 2), jnp.tile(k_norm_g[l], 2)])
        q, k, v = _qkprep(q, kv, cos, sa, sb, g2, ones_bd)
        attn = _attention(sink[l], q, k, v)
        tab = jnp.stack([mod[:, 2], g_ffn[l] * (1.0 + mod[:, 4]), mod[:, 3]], axis=1)
        gout = jnp.stack([g_out_attn[l], g_out_conv[l]])
        xa, h2, logits = _outproj(l, attn, u, b_gate, c_gate, xa, tab, gout, conv_w[l], w_out, wr)

        lt = jnp.pad(logits[:, :N_EXPERTS].T, ((0, 0), (0, RT * LANES - R))).reshape(N_EXPERTS, RT, LANES)
        _, g_t, d_t, cnt = _router(router_bias, lt, upper, lower, ones)
        gates = g_t.reshape(2, RT * LANES)[:, :R].T
        dest = d_t.reshape(2, RT * LANES)[:, :R]
        dest3 = dest.reshape(2, NT, TM).transpose(1, 0, 2).reshape(NT, 1, 2 * TM)
        counts = cnt[:, 0].astype(i32)
        pend = jnp.cumsum((counts + EBLK - 1) // EBLK * EBLK)

        xs = _dispatch(pend, dest3, h2)
        ys = _experts(l, _expert_schedule(counts), xs, w_exp_gate, w_exp_up, w_exp_down)
        xa = _combine(1 if l == DEPTH - 1 else 0, dest3, ys, xa, gates, mod[:, 5][:, None, :])

    return xa[None]
```

```python
import functools

import jax
import numpy as np
import jax.numpy as jnp
from jax import lax
from jax.experimental import pallas as pl
from jax.experimental.pallas import tpu as pltpu

f32 = jnp.float32
bf16 = jnp.bfloat16
i32 = jnp.int32

D = 2048
DEPTH = 4
SEQ = 8192
CTX = 256
R = CTX + SEQ
GRID_W = 64
ATTN_W = 1024
CONV_W = 1024
HEAD_DIM = 64
N_HEADS = 16
KV_W = 128
IN_COLS = ATTN_W + 2 * KV_W + 3 * CONV_W
ABLK = 128
WINDOW = 128
N_EXPERTS = 16
N_GROUPS = 4
EPG = 4
D_FF = 1024
EPS = 1e-6
NEG_INF = -1e30
ROPE_THETA = 10000.0

LANES = 128
SUBLANES = 8
TM = 256
NT = R // TM
EBLK = 256
EBLK_SHIFT = EBLK.bit_length() - 1
assert 1 << EBLK_SHIFT == EBLK
NBLK = -(-(2 * R + N_EXPERTS * (EBLK - 1)) // EBLK)
NROWS = NBLK * EBLK
RT = LANES
VMEM_LIMIT = 56 * 1024 * 1024


def _cparams(sem):
    return pltpu.CompilerParams(dimension_semantics=sem, vmem_limit_bytes=VMEM_LIMIT)


ADA_TK = 2048
ADA_TN = 1536


def _adaln_kernel(s_ref, w_ref, b_ref, o_ref, acc_ref):
    k = pl.program_id(2)

    @pl.when(k == 0)
    def _():
        acc_ref[...] = jnp.zeros_like(acc_ref)

    for j in range(ADA_TN // LANES):
        w = w_ref[0, :, j * LANES:(j + 1) * LANES]
        for v in range(2):
            p = (w * s_ref[v]).reshape(ADA_TK // SUBLANES, SUBLANES, LANES).sum(axis=0)
            acc_ref[v, :, j * LANES:(j + 1) * LANES] += p

    @pl.when(k == pl.num_programs(2) - 1)
    def _():
        for v in range(2):
            o_ref[0, v:v + 1, :] = acc_ref[v].sum(axis=0, keepdims=True) + b_ref[0]


def _adaln(s_bcast, w_ada, b_ada):
    return pl.pallas_call(
        _adaln_kernel,
        grid=(DEPTH, 6 * D // ADA_TN, D // ADA_TK),
        in_specs=[pl.BlockSpec((2, ADA_TK, LANES), lambda l, n, k: (0, k, 0)),
                  pl.BlockSpec((1, ADA_TK, ADA_TN), lambda l, n, k: (l, k, n)),
                  pl.BlockSpec((1, 1, ADA_TN), lambda l, n, k: (l, 0, n))],
        out_specs=pl.BlockSpec((1, 2, ADA_TN), lambda l, n, k: (l, 0, n)),
        out_shape=jax.ShapeDtypeStruct((DEPTH, 2, 6 * D), f32),
        scratch_shapes=[pltpu.VMEM((2, SUBLANES, ADA_TN), f32)],
        compiler_params=_cparams(("arbitrary", "arbitrary", "arbitrary")),
        name="adaln",
    )(s_bcast, w_ada, b_ada.reshape(DEPTH, 1, 6 * D))


def _rms_scale(x):
    return lax.rsqrt(jnp.mean(x * x, axis=-1, keepdims=True) + EPS)


WCH = 256


def _load_weight_bf16(w_hbm, wbuf, stage, sems):
    n = wbuf.shape[0] // WCH

    def chunk(c):
        return pltpu.make_async_copy(w_hbm.at[pl.ds(c * WCH, WCH)], stage.at[c % 2], sems.at[c % 2])

    chunk(0).start()
    for c in range(n):
        if c + 1 < n:
            chunk(c + 1).start()
        chunk(c).wait()
        wbuf[pl.ds(c * WCH, WCH), :] = stage[c % 2].astype(bf16)


Q0, K0, V0, U0, B0, C0 = 0, ATTN_W, ATTN_W + KV_W, ATTN_W + 2 * KV_W, ATTN_W + 2 * KV_W + CONV_W, \
    ATTN_W + 2 * KV_W + 2 * CONV_W


def _tile_rows(i, x_ref, ctx_ref):
    if ctx_ref is None:
        return x_ref[...]
    return jnp.where(i == 0, ctx_ref[...], x_ref[...])


def _row_specs(split):
    if split:
        return [pl.BlockSpec((TM, D), lambda i: (jnp.maximum(i - 1, 0), 0)), pl.BlockSpec((TM, D), lambda i: (0, 0))]
    return [pl.BlockSpec((TM, D), lambda i: (i, 0))]


def _inproj_kernel(layer, split, x_ref, *refs):
    ctx_ref, refs = (refs[0], refs[1:]) if split else (None, refs)
    ms_ref, w_hbm, q_ref, kv_ref, u_ref, b_ref, c_ref, wbuf, stage, sems = refs

    @pl.when(pl.program_id(0) == 0)
    def _():
        _load_weight_bf16(w_hbm.at[layer], wbuf, stage, sems)

    stream = jnp.where(pl.program_id(0) == 0, 0, 1)
    x = _tile_rows(pl.program_id(0), x_ref, ctx_ref)
    gain = ms_ref[stream, 0:1, :]
    shift = ms_ref[stream, 1:2, :]
    h = (x * _rms_scale(x) * gain + shift).astype(bf16)
    p = jnp.dot(h, wbuf[...], preferred_element_type=f32)
    q_ref[...] = p[:, Q0:K0]
    kv_ref[...] = p[:, K0:U0]
    u_ref[...] = p[:, U0:B0]
    b_ref[...] = p[:, B0:C0]
    c_ref[...] = p[:, C0:]


def _inproj(layer, xs, ms, w_in):
    row = lambda i: (i, 0)
    widths = (ATTN_W, 2 * KV_W, CONV_W, CONV_W, CONV_W)
    split = len(xs) == 2
    return pl.pallas_call(
        functools.partial(_inproj_kernel, layer, split),
        grid=(NT,),
        in_specs=_row_specs(split) + [pl.BlockSpec((2, 2, D), lambda i: (0, 0, 0)),
                                      pl.BlockSpec(memory_space=pl.ANY)],
        out_specs=[pl.BlockSpec((TM, n), row) for n in widths],
        out_shape=[jax.ShapeDtypeStruct((R, n), f32) for n in widths],
        scratch_shapes=[pltpu.VMEM((D, IN_COLS), bf16), pltpu.VMEM((2, WCH, IN_COLS), f32),
                        pltpu.SemaphoreType.DMA((2,))],
        compiler_params=_cparams(("arbitrary",)),
        name="inproj",
    )(*xs, ms, w_in)


def _qkprep_kernel(q_ref, k_ref, v_ref, cos_ref, sa_ref, sb_ref, g_ref, ones_ref, qo_ref, ko_ref, vo_ref):
    cos, sa, sb = cos_ref[...], sa_ref[...], sb_ref[...]
    ones = ones_ref[...]

    def norm_rope(xc, g):
        sq = xc * xc
        hi = sq.astype(bf16)
        lo = (sq - hi.astype(f32)).astype(bf16)
        ss = (jnp.dot(hi, ones, preferred_element_type=f32)
              + jnp.dot(lo, ones, preferred_element_type=f32))
        y = xc * lax.rsqrt(ss * (1.0 / HEAD_DIM) + EPS) * g
        return y * cos + pltpu.roll(y, LANES - 16, 1) * sa + pltpu.roll(y, 16, 1) * sb

    for j in range(ATTN_W // LANES):
        sl = slice(j * LANES, (j + 1) * LANES)
        qo_ref[:, sl] = norm_rope(q_ref[:, sl], g_ref[0:1, :]).astype(bf16)
    ko_ref[...] = norm_rope(k_ref[...], g_ref[1:2, :]).astype(bf16)
    vo_ref[...] = v_ref[...].astype(bf16)


def _qkprep(q, kv, cos, sa, sb, g2, ones_bd):
    row = lambda i: (i, 0)
    return pl.pallas_call(
        _qkprep_kernel,
        grid=(NT,),
        in_specs=[pl.BlockSpec((TM, ATTN_W), row),
                  pl.BlockSpec((TM, KV_W), lambda i: (i, 0)),
                  pl.BlockSpec((TM, KV_W), lambda i: (i, 1)),
                  pl.BlockSpec((TM, LANES), row),
                  pl.BlockSpec((TM, LANES), row),
                  pl.BlockSpec((TM, LANES), row),
                  pl.BlockSpec((2, LANES), lambda i: (0, 0)),
                  pl.BlockSpec((LANES, LANES), lambda i: (0, 0))],
        out_specs=[pl.BlockSpec((TM, ATTN_W), row),
                   pl.BlockSpec((TM, KV_W), row),
                   pl.BlockSpec((TM, KV_W), row)],
        out_shape=[jax.ShapeDtypeStruct((R, ATTN_W), bf16),
                   jax.ShapeDtypeStruct((R, KV_W), bf16),
                   jax.ShapeDtypeStruct((R, KV_W), bf16)],
        compiler_params=_cparams(("arbitrary",)),
        name="qkprep",
    )(q, kv, kv, cos, sa, sb, g2, ones_bd)


NQB = R // ABLK
FIRST_LAT = CTX // ABLK
NKEYS = CTX + 3 * ABLK


def _attn_kernel(sink_ref, q_ref, kc_ref, kp_ref, kk_ref, kn_ref, vc_ref, vp_ref, vk_ref, vn_ref, o_ref, p_s):
    i = pl.program_id(0)
    kf = jnp.concatenate([kc_ref[...], kp_ref[...], kk_ref[...], kn_ref[...]], axis=0).astype(f32)
    vf = jnp.concatenate([vc_ref[...], vp_ref[...], vk_ref[...], vn_ref[...]], axis=0).astype(f32)
    lane = lax.broadcasted_iota(i32, (NKEYS, LANES), 1)
    low = lane < HEAD_DIM
    kr = pltpu.roll(kf, HEAD_DIM, 1)
    vr = pltpu.roll(vf, HEAD_DIM, 1)
    zero = jnp.zeros_like(kf)
    k_lo = (jnp.where(low, kf, zero), jnp.where(low, kr, zero))
    k_hi = (jnp.where(low, zero, kr), jnp.where(low, zero, kf))
    v_lo = (jnp.where(low, vf, zero), jnp.where(low, vr, zero))
    v_hi = (jnp.where(low, zero, vr), jnp.where(low, zero, vf))

    qi = lax.broadcasted_iota(i32, (ABLK, NKEYS), 0)
    col = lax.broadcasted_iota(i32, (ABLK, NKEYS), 1)
    kj = col - CTX
    in_window = jnp.abs(qi + ABLK - kj) <= WINDOW
    key_block = i - 1 + jnp.right_shift(kj, 7)
    first_ok = jnp.where(i >= FIRST_LAT, FIRST_LAT, NQB)
    mask = (col < CTX) | (in_window & (key_block >= first_ok) & (key_block < NQB))
    lane_q = lax.broadcasted_iota(i32, (ABLK, LANES), 1)

    contract_last = (((1,), (1,)), ((), ()))
    for h in range(2):
        qs = jnp.concatenate([q_ref[:, (h * 4 + p) * LANES:(h * 4 + p + 1) * LANES] for p in range(4)], axis=0)
        s_par = (lax.dot_general(qs, k_lo[h].astype(bf16), contract_last, preferred_element_type=f32),
                 lax.dot_general(qs, k_hi[h].astype(bf16), contract_last, preferred_element_type=f32))
        rden = ([], [])
        for par in range(2):
            for p in range(4):
                sink = sink_ref[h * 8 + 2 * p + par]
                s = jnp.where(mask, s_par[par][p * ABLK:(p + 1) * ABLK], NEG_INF)
                m = jnp.maximum(jnp.max(s, axis=-1, keepdims=True), sink)
                e = jnp.exp(s - m)
                den = jnp.sum(e, axis=-1, keepdims=True) + jnp.exp(sink - m)
                p_s[par, p * ABLK:(p + 1) * ABLK, :] = e.astype(bf16)
                rden[par].append(1.0 / den)
        o = (jnp.dot(p_s[0], v_lo[h].astype(bf16), preferred_element_type=f32)
             + jnp.dot(p_s[1], v_hi[h].astype(bf16), preferred_element_type=f32))
        for p in range(4):
            scale = jnp.where(lane_q < HEAD_DIM, rden[0][p], rden[1][p])
            o_ref[:, (h * 4 + p) * LANES:(h * 4 + p + 1) * LANES] = o[p * ABLK:(p + 1) * ABLK] * scale


def _attention(sink, q, k, v):
    clip = lambda b: jnp.clip(b, FIRST_LAT, NQB - 1)
    kv_specs = [pl.BlockSpec((CTX, KV_W), lambda i: (0, 0)),
                pl.BlockSpec((ABLK, KV_W), lambda i: (clip(i - 1), 0)),
                pl.BlockSpec((ABLK, KV_W), lambda i: (i, 0)),
                pl.BlockSpec((ABLK, KV_W), lambda i: (clip(i + 1), 0))]
    return pl.pallas_call(
        _attn_kernel,
        grid=(NQB,),
        in_specs=[pl.BlockSpec(memory_space=pltpu.SMEM),
                  pl.BlockSpec((ABLK, ATTN_W), lambda i: (i, 0))] + kv_specs + kv_specs,
        out_specs=pl.BlockSpec((ABLK, ATTN_W), lambda i: (i, 0)),
        out_shape=jax.ShapeDtypeStruct((R, ATTN_W), f32),
        scratch_shapes=[pltpu.VMEM((2, 4 * ABLK, NKEYS), bf16)],
        compiler_params=_cparams(("arbitrary",)),
        name="attention",
    )(sink, q, k, k, k, k, v, v, v, v)


def _outproj_kernel(layer, split, attn_ref, u_ref, b_ref, c_ref, up_ref, cp_ref, un_ref, cn_ref, x_ref, *refs):
    ctx_ref, refs = (refs[0], refs[1:]) if split else (None, refs)
    tab_ref, gout_ref, cw_ref, w_hbm, wr_ref, xo_ref, h2_ref, lg_ref, wbuf, stage, sems = refs
    i = pl.program_id(0)

    @pl.when(i == 0)
    def _():
        _load_weight_bf16(w_hbm.at[layer], wbuf, stage, sems)

    stream = jnp.where(i == 0, 0, 1)
    prev_ok = (i >= 2).astype(f32)
    next_ok = ((i >= 1) & (i < pl.num_programs(0) - 1)).astype(f32)
    w = c_ref[...] * u_ref[...]
    w_before = cp_ref[SUBLANES - 1:SUBLANES, :] * up_ref[SUBLANES - 1:SUBLANES, :] * prev_ok
    w_after = cn_ref[0:1, :] * un_ref[0:1, :] * next_ok
    rows = lax.broadcasted_iota(i32, (TM, 1), 0)
    w_prev = jnp.where(rows == 0, w_before, pltpu.roll(w, 1, 0))
    w_next = jnp.where(rows == TM - 1, w_after, pltpu.roll(w, TM - 1, 0))
    conv = b_ref[...] * (cw_ref[0:1, :] * w_prev + cw_ref[1:2, :] * w + cw_ref[2:3, :] * w_next)

    attn = attn_ref[...]
    a_n = attn * _rms_scale(attn) * gout_ref[0:1, :]
    c_n = conv * _rms_scale(conv) * gout_ref[1:2, :]
    cat = jnp.concatenate([a_n, c_n], axis=1).astype(bf16)
    mixed = jnp.dot(cat, wbuf[...], preferred_element_type=f32)
    xn = _tile_rows(i, x_ref, ctx_ref) + tab_ref[stream, 0:1, :] * mixed
    xo_ref[...] = xn

    h2 = xn * _rms_scale(xn) * tab_ref[stream, 1:2, :] + tab_ref[stream, 2:3, :]
    h2_ref[...] = h2
    hi = h2.astype(bf16)
    lo = (h2 - hi.astype(f32)).astype(bf16)
    lg_ref[...] = (jnp.dot(hi, wr_ref[0], preferred_element_type=f32)
                   + jnp.dot(lo, wr_ref[0], preferred_element_type=f32)
                   + jnp.dot(hi, wr_ref[1], preferred_element_type=f32))


def _outproj(layer, attn, u, b, c, xs, tab, gout, cw, w_out, wr):
    halo = TM // SUBLANES
    last8 = R // SUBLANES - 1
    row = lambda i: (i, 0)
    prev8 = lambda i: (jnp.maximum(i * halo - 1, 0), 0)
    next8 = lambda i: (jnp.minimum((i + 1) * halo, last8), 0)
    split = len(xs) == 2
    return pl.pallas_call(
        functools.partial(_outproj_kernel, layer, split),
        grid=(NT,),
        in_specs=[pl.BlockSpec((TM, ATTN_W), row),
                  pl.BlockSpec((TM, CONV_W), row),
                  pl.BlockSpec((TM, CONV_W), row),
                  pl.BlockSpec((TM, CONV_W), row),
                  pl.BlockSpec((SUBLANES, CONV_W), prev8),
                  pl.BlockSpec((SUBLANES, CONV_W), prev8),
                  pl.BlockSpec((SUBLANES, CONV_W), next8),
                  pl.BlockSpec((SUBLANES, CONV_W), next8)]
                 + _row_specs(split)
                 + [pl.BlockSpec((2, 3, D), lambda i: (0, 0, 0)),
                    pl.BlockSpec((2, CONV_W), lambda i: (0, 0)),
                    pl.BlockSpec((3, CONV_W), lambda i: (0, 0)),
                    pl.BlockSpec(memory_space=pl.ANY),
                    pl.BlockSpec((2, D, LANES), lambda i: (0, 0, 0))],
        out_specs=[pl.BlockSpec((TM, D), row),
                   pl.BlockSpec((TM, D), row),
                   pl.BlockSpec((TM, LANES), row)],
        out_shape=[jax.ShapeDtypeStruct((R, D), f32),
                   jax.ShapeDtypeStruct((R, D), f32),
                   jax.ShapeDtypeStruct((R, LANES), f32)],
        scratch_shapes=[pltpu.VMEM((D, D), bf16), pltpu.VMEM((2, WCH, D), f32), pltpu.SemaphoreType.DMA((2,))],
        compiler_params=_cparams(("arbitrary",)),
        name="outproj",
    )(attn, u, b, c, u, c, u, c, *xs, tab, gout, cw, w_out, wr)


def _router_kernel(bias_ref, lt_ref, upper_ref, lower_ref, ones_ref, e_ref, g_ref, d_ref, cnt_ref):
    score = [1.0 / (1.0 + jnp.exp(-lt_ref[e])) for e in range(N_EXPERTS)]
    sel = [score[e] + bias_ref[e] for e in range(N_EXPERTS)]

    def top2_sum(a, b, c, d):
        p, q = jnp.maximum(a, b), jnp.minimum(a, b)
        r, s = jnp.maximum(c, d), jnp.minimum(c, d)
        return jnp.maximum(p, r) + jnp.maximum(jnp.minimum(p, r), jnp.maximum(q, s))

    gscore = [top2_sum(*sel[EPG * g:EPG * (g + 1)]) for g in range(N_GROUPS)]
    best, gidx = gscore[0], jnp.zeros(gscore[0].shape, i32)
    for g in range(1, N_GROUPS):
        take = gscore[g] > best
        best = jnp.where(take, gscore[g], best)
        gidx = jnp.where(take, g, gidx)

    def pick_group(vals):
        out = []
        for j in range(EPG):
            v = vals[j]
            for g in range(1, N_GROUPS):
                v = jnp.where(gidx == g, vals[EPG * g + j], v)
            out.append(v)
        return out

    in_sel = pick_group(sel)
    in_score = pick_group(score)

    def argmax_first(vals, excluded):
        bv, bi = None, None
        for j in range(EPG):
            v = vals[j] if excluded is None else jnp.where(excluded == j, -jnp.inf, vals[j])
            if bv is None:
                bv, bi = v, jnp.zeros(v.shape, i32)
            else:
                take = v > bv
                bv = jnp.where(take, v, bv)
                bi = jnp.where(take, j, bi)
        return bi

    i1 = argmax_first(in_sel, None)
    i2 = argmax_first(in_sel, i1)

    def pick_local(vals, idx):
        v = vals[0]
        for j in range(1, EPG):
            v = jnp.where(idx == j, vals[j], v)
        return v

    s1, s2 = pick_local(in_score, i1), pick_local(in_score, i2)
    tot = s1 + s2
    e1 = gidx * EPG + i1
    e2 = gidx * EPG + i2
    e_ref[0], e_ref[1] = e1, e2
    g_ref[0], g_ref[1] = s1 / tot, s2 / tot

    tok = (lax.broadcasted_iota(i32, (RT, LANES), 0) * LANES + lax.broadcasted_iota(i32, (RT, LANES), 1))
    valid = tok < R
    onehot = [(((e1 == e) | (e2 == e)) & valid).astype(f32) for e in range(N_EXPERTS)]
    stack = jnp.concatenate(onehot, axis=0).astype(bf16)
    within = jnp.dot(stack, upper_ref[...], preferred_element_type=f32)
    rowtot = jnp.dot(stack, ones_ref[...], preferred_element_type=f32)
    d1 = jnp.zeros((RT, LANES), i32)
    d2 = jnp.zeros((RT, LANES), i32)
    seg_start = jnp.zeros((1, LANES), i32)
    for e in range(N_EXPERTS):
        rt_e = rowtot[e * RT:(e + 1) * RT]
        before = jnp.dot(lower_ref[...], rt_e.astype(bf16), preferred_element_type=f32)
        rank_e = within[e * RT:(e + 1) * RT] + before
        count_e = (before + rt_e)[RT - 1:RT, :]
        cnt_ref[e:e + 1, :] = count_e
        slot = seg_start + rank_e.astype(i32)
        d1 = jnp.where(e1 == e, slot, d1)
        d2 = jnp.where(e2 == e, slot, d2)
        blocks_e = jnp.right_shift(count_e.astype(i32) + (EBLK - 1), EBLK_SHIFT)
        seg_start = seg_start + jnp.left_shift(blocks_e, EBLK_SHIFT)
    d_ref[0], d_ref[1] = d1, d2


def _router(bias, logits_t, upper, lower, ones):
    full = lambda *shape: pl.BlockSpec(shape, lambda: (0,) * len(shape))
    return pl.pallas_call(
        _router_kernel,
        in_specs=[pl.BlockSpec(memory_space=pltpu.SMEM),
                  full(N_EXPERTS, RT, LANES), full(LANES, LANES), full(RT, RT), full(LANES, LANES)],
        out_specs=[full(2, RT, LANES), full(2, RT, LANES), full(2, RT, LANES), full(N_EXPERTS, LANES)],
        out_shape=[jax.ShapeDtypeStruct((2, RT, LANES), i32),
                   jax.ShapeDtypeStruct((2, RT, LANES), f32),
                   jax.ShapeDtypeStruct((2, RT, LANES), i32),
                   jax.ShapeDtypeStruct((N_EXPERTS, LANES), f32)],
        compiler_params=pltpu.CompilerParams(vmem_limit_bytes=VMEM_LIMIT),
        name="router",
    )(bias, logits_t, upper, lower, ones)


ROW_UNROLL = TM


HSLOTS = 3


def _dispatch_kernel(pend_ref, dest_ref, h_hbm, xs_hbm, hbuf, zbuf, lsem, rsem, zsem):
    i = pl.program_id(0)
    last = pl.num_programs(0) - 1
    slot = lax.rem(i, HSLOTS)
    prev_slot = lax.rem(i + (HSLOTS - 1), HSLOTS)
    next_slot = lax.rem(i + 1, HSLOTS)

    def load(tile, s):
        rows = pl.ds(pl.multiple_of(tile * TM, TM), TM)
        return pltpu.make_async_copy(h_hbm.at[rows], hbuf.at[s], lsem.at[s])

    @pl.when(i == 0)
    def _():
        load(0, 0).start()
        zbuf[...] = jnp.zeros_like(zbuf)

        def zero_copy(e):
            first = pl.multiple_of(pend_ref[e] - EBLK, EBLK)
            return pltpu.make_async_copy(zbuf, xs_hbm.at[pl.ds(first, EBLK)], zsem)

        def nonempty(e):
            return pend_ref[e] > (pend_ref[e - 1] if e else 0)

        for e in range(N_EXPERTS):
            pl.when(nonempty(e))(lambda e=e: zero_copy(e).start())
        for e in range(N_EXPERTS):
            pl.when(nonempty(e))(lambda e=e: zero_copy(e).wait())

        def tail_copy(j):
            return pltpu.make_async_copy(zbuf, xs_hbm.at[pl.ds(pl.multiple_of(j * EBLK, EBLK), EBLK)], zsem)

        first_unused = jnp.right_shift(pend_ref[N_EXPERTS - 1], EBLK_SHIFT)
        lax.fori_loop(first_unused, NBLK, lambda j, c: (tail_copy(j).start(), c)[1], 0)
        lax.fori_loop(first_unused, NBLK, lambda j, c: (tail_copy(j).wait(), c)[1], 0)

    @pl.when(i < last)
    def _():
        load(i + 1, next_slot).start()

    load(i, slot).wait()

    def start(t, c):
        for k in range(2):
            pltpu.make_async_copy(hbuf.at[slot, pl.ds(t, 1)], xs_hbm.at[pl.ds(dest_ref[0, 0, k * TM + t], 1)],
                                  rsem.at[slot]).start(priority=k)
        return c

    def drain(s):
        def wait(t, c):
            for k in range(2):
                pltpu.make_async_copy(hbuf.at[s, pl.ds(0, 1)], xs_hbm.at[pl.ds(0, 1)], rsem.at[s]).wait()
            return c
        lax.fori_loop(0, TM, wait, 0, unroll=ROW_UNROLL)

    lax.fori_loop(0, TM, start, 0, unroll=ROW_UNROLL)
    pl.when(i > 0)(lambda: drain(prev_slot))
    pl.when(i == last)(lambda: drain(slot))


def _dispatch(pend, dest3, h2):
    return pl.pallas_call(
        _dispatch_kernel,
        grid_spec=pltpu.PrefetchScalarGridSpec(
            num_scalar_prefetch=1,
            grid=(NT,),
            in_specs=[pl.BlockSpec((1, 1, 2 * TM), lambda i, pend: (i, 0, 0), memory_space=pltpu.SMEM),
                      pl.BlockSpec(memory_space=pl.ANY)],
            out_specs=pl.BlockSpec(memory_space=pl.ANY),
            scratch_shapes=[pltpu.VMEM((HSLOTS, TM, D), f32), pltpu.VMEM((EBLK, D), f32),
                            pltpu.SemaphoreType.DMA((HSLOTS,)), pltpu.SemaphoreType.DMA((HSLOTS,)),
                            pltpu.SemaphoreType.DMA(())]),
        out_shape=jax.ShapeDtypeStruct((NROWS, D), f32),
        compiler_params=_cparams(("arbitrary",)),
        name="dispatch",
    )(pend, dest3, h2)


NCH = 4
FCH = D_FF // NCH
NSTEPS = NBLK + (N_EXPERTS + 1) * NCH
(T_COMP, T_XBLK, T_CONV, T_CEXP, T_CCH, T_CSLOT, T_SLOT, T_ZERO, T_OBLK) = range(9)
T_STRIDE = 16


def _expert_schedule(counts):
    experts = jnp.arange(N_EXPERTS, dtype=i32)
    nb = (counts + EBLK - 1) // EBLK
    bstart = jnp.cumsum(nb) - nb
    nblocks = jnp.sum(nb)
    steps = jnp.maximum(nb, NCH)
    send = NCH + jnp.cumsum(steps)
    total = send[-1]
    s = jnp.arange(NSTEPS, dtype=i32)
    e_s = jnp.minimum(jnp.sum((s[:, None] >= send[None, :]).astype(i32), axis=1), N_EXPERTS - 1)
    onehot = (e_s[:, None] == experts[None, :]).astype(i32)
    pick = lambda table: jnp.sum(onehot * table[None, :], axis=1)
    i_s = s - pick(send - steps)
    nb_s = pick(nb)
    pre = s < NCH
    active = (s >= NCH) & (s < total)
    done = s >= total
    comp = active & (i_s < nb_s)
    xblk = jnp.clip(pick(bstart) + jnp.minimum(i_s, nb_s - 1), 0, nblocks - 1)
    xblk = jnp.where(pre, 0, jnp.where(done, nblocks - 1, xblk))
    has_next = e_s < N_EXPERTS - 1
    conv = pre | (active & (i_s < NCH) & has_next)
    cexp = jnp.where(pre, 0, jnp.minimum(e_s + 1, N_EXPERTS - 1))
    cch = jnp.where(pre, s, jnp.where(has_next & active, jnp.minimum(i_s, NCH - 1), NCH - 1))
    zblk = nblocks + (s - total)
    zero = done & (zblk < NBLK)
    oblk = jnp.where(done, jnp.minimum(zblk, NBLK - 1), xblk)
    cols = [comp, xblk, conv, cexp, cch, cexp % 2, e_s % 2, zero, oblk]
    cols = [col.astype(i32) for col in cols] + [jnp.zeros_like(s)] * (T_STRIDE - 9)
    return jnp.stack(cols, axis=1).reshape(NSTEPS * T_STRIDE)


def _expert_kernel(tab_ref, x_ref, g_ref, u_ref, d_ref, o_ref, wg_s, wu_s, wd_s):
    base = pl.program_id(0) * T_STRIDE

    @pl.when(tab_ref[base + T_CONV] == 1)
    def _():
        slot, c = tab_ref[base + T_CSLOT], tab_ref[base + T_CCH]
        wg_s[slot, c] = g_ref[0, 0].astype(bf16)
        wu_s[slot, c] = u_ref[0, 0].astype(bf16)
        wd_s[slot, c] = d_ref[0, 0].astype(bf16)

    @pl.when(tab_ref[base + T_COMP] == 1)
    def _():
        slot = tab_ref[base + T_SLOT]
        x = x_ref[...].astype(bf16)
        y = None
        for c in range(NCH):
            a = jnp.dot(x, wg_s[slot, c], preferred_element_type=f32)
            b = jnp.dot(x, wu_s[slot, c], preferred_element_type=f32)
            h = (a / (1.0 + jnp.exp(-a)) * b).astype(bf16)
            yc = jnp.dot(h, wd_s[slot, c], preferred_element_type=f32)
            y = yc if y is None else y + yc
        o_ref[...] = y

    @pl.when(tab_ref[base + T_ZERO] == 1)
    def _():
        o_ref[...] = jnp.zeros_like(o_ref)


def _experts(layer, schedule, xs, w_gate, w_up, w_down):
    at = lambda s, tab, col: tab[s * T_STRIDE + col]
    up_spec = pl.BlockSpec((1, 1, D, FCH), lambda s, tab: (layer, at(s, tab, T_CEXP), 0, at(s, tab, T_CCH)))
    return pl.pallas_call(
        _expert_kernel,
        grid_spec=pltpu.PrefetchScalarGridSpec(
            num_scalar_prefetch=1,
            grid=(NSTEPS,),
            in_specs=[pl.BlockSpec((EBLK, D), lambda s, tab: (at(s, tab, T_XBLK), 0)),
                      up_spec, up_spec,
                      pl.BlockSpec((1, 1, FCH, D),
                                   lambda s, tab: (layer, at(s, tab, T_CEXP), at(s, tab, T_CCH), 0))],
            out_specs=pl.BlockSpec((EBLK, D), lambda s, tab: (at(s, tab, T_OBLK), 0)),
            scratch_shapes=[pltpu.VMEM((2, NCH, D, FCH), bf16), pltpu.VMEM((2, NCH, D, FCH), bf16),
                            pltpu.VMEM((2, NCH, FCH, D), bf16)]),
        out_shape=jax.ShapeDtypeStruct((NROWS, D), f32),
        compiler_params=_cparams(("arbitrary",)),
        name="experts",
    )(schedule, xs, w_gate, w_up, w_down)


def _combine_kernel(first, dest_ref, dnext_ref, ys_hbm, x_ref, gates_ref, gate_ref, o_ref, ybuf, sems):
    i = pl.program_id(0)
    slot = jnp.bitwise_and(i, 1)
    stream = jnp.where(i + first == 0, 0, 1)

    def start_tile(idx_ref, s):
        def body(t, c):
            for k in range(2):
                pltpu.make_async_copy(ys_hbm.at[pl.ds(idx_ref[0, 0, k * TM + t], 1)],
                                      ybuf.at[s, k, pl.ds(t, 1)], sems.at[s]).start(priority=k)
            return c
        lax.fori_loop(0, TM, body, 0, unroll=ROW_UNROLL)

    @pl.when(i == 0)
    def _():
        start_tile(dest_ref, 0)

    @pl.when(i + 1 < pl.num_programs(0))
    def _():
        start_tile(dnext_ref, 1 - slot)

    def wait(t, c):
        for k in range(2):
            pltpu.make_async_copy(ys_hbm.at[pl.ds(0, 1)], ybuf.at[slot, k, pl.ds(0, 1)], sems.at[slot]).wait()
        return c

    lax.fori_loop(0, TM, wait, 0, unroll=ROW_UNROLL)
    y = ybuf[slot, 0] * gates_ref[:, 0:1] + ybuf[slot, 1] * gates_ref[:, 1:2]
    o_ref[...] = x_ref[...] + gate_ref[stream] * y


def _combine(first, dest3, ys, x, gates, gate2):
    row = lambda i: (i + first, 0)
    return pl.pallas_call(
        functools.partial(_combine_kernel, first),
        grid=(NT - first,),
        in_specs=[pl.BlockSpec((1, 1, 2 * TM), lambda i: (i + first, 0, 0), memory_space=pltpu.SMEM),
                  pl.BlockSpec((1, 1, 2 * TM), lambda i: (jnp.minimum(i + first + 1, NT - 1), 0, 0),
                               memory_space=pltpu.SMEM),
                  pl.BlockSpec(memory_space=pl.ANY),
                  pl.BlockSpec((TM, D), row),
                  pl.BlockSpec((TM, 2), row),
                  pl.BlockSpec((2, 1, D), lambda i: (0, 0, 0))],
        out_specs=pl.BlockSpec((TM, D), lambda i: (i, 0)),
        out_shape=jax.ShapeDtypeStruct((R - first * TM, D), f32),
        scratch_shapes=[pltpu.VMEM((2, 2, TM, D), f32), pltpu.SemaphoreType.DMA((2,))],
        compiler_params=_cparams(("arbitrary",)),
        name="combine",
    )(dest3, dest3, ys, x, gates, gate2)


def _rope_tables():
    half = HEAD_DIM // 2
    inv_freq = (ROPE_THETA ** (-np.arange(0, half, 2, dtype=np.float32) / half)).astype(np.float32)
    t = np.arange(SEQ)
    pos = np.stack([(t // GRID_W).astype(np.float32), (t % GRID_W).astype(np.float32)], axis=1)
    d = np.arange(LANES) % HEAD_DIM
    axis, sub = d // half, d % half
    ang = (pos[:, axis] * inv_freq[sub % (half // 2)][None, :]).astype(np.float32)
    cos, sin = np.cos(ang).astype(np.float32), np.sin(ang).astype(np.float32)
    first = (sub < half // 2)[None, :]
    sa = np.where(first, -sin, 0.0).astype(np.float32)
    sb = np.where(first, 0.0, sin).astype(np.float32)
    ident = np.ones((CTX, LANES), np.float32)
    zeros = np.zeros((CTX, LANES), np.float32)
    return tuple(jnp.asarray(np.concatenate(parts)) for parts in ((ident, cos), (zeros, sa), (zeros, sb)))


def kernel(x, c, ctx, c_ctx, w_ada, b_ada, g_attn, w_in, q_norm_g, k_norm_g, sink, conv_w, g_out_attn, g_out_conv, w_out, g_ffn, w_router, router_bias, w_exp_gate, w_exp_up, w_exp_down):
    assert x.shape == (1, SEQ, D) and ctx.shape == (1, CTX, D)
    xs_in = (x[0], ctx[0])

    silu = lambda t: t * jax.nn.sigmoid(t)
    s_vec = jnp.stack([silu(c_ctx), silu(c[0])])
    mods = _adaln(jnp.broadcast_to(s_vec[:, :, None], (2, D, LANES)), w_ada, b_ada).reshape(DEPTH, 2, 6, D)

    cos, sa, sb = _rope_tables()
    lane = jnp.arange(LANES)
    ones_bd = (lane[:, None] // HEAD_DIM == lane[None, :] // HEAD_DIM).astype(bf16)
    upper = (lane[:, None] < lane[None, :]).astype(bf16)
    ones = jnp.ones((LANES, LANES), bf16)
    rt = jnp.arange(RT)
    lower = (rt[None, :] < rt[:, None]).astype(bf16)
    wr_pad = jnp.pad(w_router, ((0, 0), (0, LANES - N_EXPERTS)))
    wr_hi = wr_pad.astype(bf16)
    wr = jnp.stack([wr_hi, (wr_pad - wr_hi.astype(f32)).astype(bf16)])

    for l in range(DEPTH):
        mod = mods[l]
        ms1 = jnp.stack([g_attn[l] * (1.0 + mod[:, 1]), mod[:, 0]], axis=1)
        q, kv, u, b_gate, c_gate = _inproj(l, xs_in, ms1, w_in)
        g2 = jnp.stack([jnp.tile(q_norm_g[l] * HEAD_DIM ** -0.5, 2), jnp.tile(k_norm_g[l], 2)])
        q, k, v = _qkprep(q, kv, cos, sa, sb, g2, ones_bd)
        attn = _attention(sink[l], q, k, v)
        tab = jnp.stack([mod[:, 2], g_ffn[l] * (1.0 + mod[:, 4]), mod[:, 3]], axis=1)
        gout = jnp.stack([g_out_attn[l], g_out_conv[l]])
        xa, h2, logits = _outproj(l, attn, u, b_gate, c_gate, xs_in, tab, gout, conv_w[l], w_out, wr)

        lt = jnp.pad(logits[:, :N_EXPERTS].T, ((0, 0), (0, RT * LANES - R))).reshape(N_EXPERTS, RT, LANES)
        _, g_t, d_t, cnt = _router(router_bias, lt, upper, lower, ones)
        gates = g_t.reshape(2, RT * LANES)[:, :R].T
        dest = d_t.reshape(2, RT * LANES)[:, :R]
        dest3 = dest.reshape(2, NT, TM).transpose(1, 0, 2).reshape(NT, 1, 2 * TM)
        counts = cnt[:, 0].astype(i32)
        pend = jnp.cumsum((counts + EBLK - 1) // EBLK * EBLK)

        xs = _dispatch(pend, dest3, h2)
        ys = _experts(l, _expert_schedule(counts), xs, w_exp_gate, w_exp_up, w_exp_down)
        xa = _combine(1 if l == DEPTH - 1 else 0, dest3, ys, xa, gates, mod[:, 5][:, None, :])
        xs_in = (xa,)

    return xa[None]
```

```python
import functools

import jax
import numpy as np
import jax.numpy as jnp
from jax import lax
from jax.experimental import pallas as pl
from jax.experimental.pallas import tpu as pltpu

f32 = jnp.float32
bf16 = jnp.bfloat16
i32 = jnp.int32

D = 2048
DEPTH = 4
SEQ = 8192
CTX = 256
R = CTX + SEQ
GRID_W = 64
ATTN_W = 1024
CONV_W = 1024
HEAD_DIM = 64
N_HEADS = 16
KV_W = 128
IN_COLS = ATTN_W + 2 * KV_W + 3 * CONV_W
ABLK = 128
WINDOW = 128
N_EXPERTS = 16
N_GROUPS = 4
EPG = 4
D_FF = 1024
EPS = 1e-6
NEG_INF = -1e30
ROPE_THETA = 10000.0

LANES = 128
SUBLANES = 8
TM = 256
NT = R // TM
EBLK = 256
EBLK_SHIFT = EBLK.bit_length() - 1
assert 1 << EBLK_SHIFT == EBLK
NBLK = -(-(2 * R + N_EXPERTS * (EBLK - 1)) // EBLK)
NROWS = NBLK * EBLK
RT = LANES
VMEM_LIMIT = 56 * 1024 * 1024


def _cparams(sem):
    return pltpu.CompilerParams(dimension_semantics=sem, vmem_limit_bytes=VMEM_LIMIT)


ADA_TK = 2048
ADA_TN = 1536


def _adaln_kernel(s_ref, w_ref, b_ref, o_ref, acc_ref):
    k = pl.program_id(2)

    @pl.when(k == 0)
    def _():
        acc_ref[...] = jnp.zeros_like(acc_ref)

    for j in range(ADA_TN // LANES):
        w = w_ref[0, :, j * LANES:(j + 1) * LANES]
        for v in range(2):
            p = (w * s_ref[v]).reshape(ADA_TK // SUBLANES, SUBLANES, LANES).sum(axis=0)
            acc_ref[v, :, j * LANES:(j + 1) * LANES] += p

    @pl.when(k == pl.num_programs(2) - 1)
    def _():
        for v in range(2):
            o_ref[0, v:v + 1, :] = acc_ref[v].sum(axis=0, keepdims=True) + b_ref[0]


def _adaln(s_bcast, w_ada, b_ada):
    return pl.pallas_call(
        _adaln_kernel,
        grid=(DEPTH, 6 * D // ADA_TN, D // ADA_TK),
        in_specs=[pl.BlockSpec((2, ADA_TK, LANES), lambda l, n, k: (0, k, 0)),
                  pl.BlockSpec((1, ADA_TK, ADA_TN), lambda l, n, k: (l, k, n)),
                  pl.BlockSpec((1, 1, ADA_TN), lambda l, n, k: (l, 0, n))],
        out_specs=pl.BlockSpec((1, 2, ADA_TN), lambda l, n, k: (l, 0, n)),
        out_shape=jax.ShapeDtypeStruct((DEPTH, 2, 6 * D), f32),
        scratch_shapes=[pltpu.VMEM((2, SUBLANES, ADA_TN), f32)],
        compiler_params=_cparams(("arbitrary", "arbitrary", "arbitrary")),
        name="adaln",
    )(s_bcast, w_ada, b_ada.reshape(DEPTH, 1, 6 * D))


def _rms_scale(x):
    return lax.rsqrt(jnp.mean(x * x, axis=-1, keepdims=True) + EPS)


WCH = 256


def _load_weight_bf16(w_hbm, wbuf, stage, sems):
    n = wbuf.shape[0] // WCH

    def chunk(c):
        return pltpu.make_async_copy(w_hbm.at[pl.ds(c * WCH, WCH)], stage.at[c % 2], sems.at[c % 2])

    chunk(0).start()
    for c in range(n):
        if c + 1 < n:
            chunk(c + 1).start()
        chunk(c).wait()
        wbuf[pl.ds(c * WCH, WCH), :] = stage[c % 2].astype(bf16)


Q0, K0, V0, U0, B0, C0 = 0, ATTN_W, ATTN_W + KV_W, ATTN_W + 2 * KV_W, ATTN_W + 2 * KV_W + CONV_W, \
    ATTN_W + 2 * KV_W + 2 * CONV_W


def _tile_rows(i, x_ref, ctx_ref):
    if ctx_ref is None:
        return x_ref[...]
    return jnp.where(i == 0, ctx_ref[...], x_ref[...])


def _row_specs(split):
    if split:
        return [pl.BlockSpec((TM, D), lambda i: (jnp.maximum(i - 1, 0), 0)), pl.BlockSpec((TM, D), lambda i: (0, 0))]
    return [pl.BlockSpec((TM, D), lambda i: (i, 0))]


def _inproj_kernel(layer, split, x_ref, *refs):
    ctx_ref, refs = (refs[0], refs[1:]) if split else (None, refs)
    ms_ref, w_hbm, q_ref, kv_ref, u_ref, b_ref, c_ref, wbuf, stage, sems = refs

    @pl.when(pl.program_id(0) == 0)
    def _():
        _load_weight_bf16(w_hbm.at[layer], wbuf, stage, sems)

    stream = jnp.where(pl.program_id(0) == 0, 0, 1)
    x = _tile_rows(pl.program_id(0), x_ref, ctx_ref)
    gain = ms_ref[stream, 0:1, :]
    shift = ms_ref[stream, 1:2, :]
    h = (x * _rms_scale(x) * gain + shift).astype(bf16)
    p = jnp.dot(h, wbuf[...], preferred_element_type=f32)
    q_ref[...] = p[:, Q0:K0]
    kv_ref[...] = p[:, K0:U0]
    u_ref[...] = p[:, U0:B0]
    b_ref[...] = p[:, B0:C0]
    c_ref[...] = p[:, C0:]


def _inproj(layer, xs, ms, w_in):
    row = lambda i: (i, 0)
    widths = (ATTN_W, 2 * KV_W, CONV_W, CONV_W, CONV_W)
    split = len(xs) == 2
    return pl.pallas_call(
        functools.partial(_inproj_kernel, layer, split),
        grid=(NT,),
        in_specs=_row_specs(split) + [pl.BlockSpec((2, 2, D), lambda i: (0, 0, 0)),
                                      pl.BlockSpec(memory_space=pl.ANY)],
        out_specs=[pl.BlockSpec((TM, n), row) for n in widths],
        out_shape=[jax.ShapeDtypeStruct((R, n), f32) for n in widths],
        scratch_shapes=[pltpu.VMEM((D, IN_COLS), bf16), pltpu.VMEM((2, WCH, IN_COLS), f32),
                        pltpu.SemaphoreType.DMA((2,))],
        compiler_params=_cparams(("arbitrary",)),
        name="inproj",
    )(*xs, ms, w_in)


def _qkprep_kernel(q_ref, k_ref, v_ref, cos_ref, sa_ref, sb_ref, g_ref, ones_ref, qo_ref, ko_ref, vo_ref):
    cos, sa, sb = cos_ref[...], sa_ref[...], sb_ref[...]
    ones = ones_ref[...]

    def norm_rope(xc, g):
        sq = xc * xc
        hi = sq.astype(bf16)
        lo = (sq - hi.astype(f32)).astype(bf16)
        ss = (jnp.dot(hi, ones, preferred_element_type=f32)
              + jnp.dot(lo, ones, preferred_element_type=f32))
        y = xc * lax.rsqrt(ss * (1.0 / HEAD_DIM) + EPS) * g
        return y * cos + pltpu.roll(y, LANES - 16, 1) * sa + pltpu.roll(y, 16, 1) * sb

    for j in range(ATTN_W // LANES):
        sl = slice(j * LANES, (j + 1) * LANES)
        qo_ref[:, sl] = norm_rope(q_ref[:, sl], g_ref[0:1, :]).astype(bf16)
    ko_ref[...] = norm_rope(k_ref[...], g_ref[1:2, :]).astype(bf16)
    vo_ref[...] = v_ref[...].astype(bf16)


def _qkprep(q, kv, cos, sa, sb, g2, ones_bd):
    row = lambda i: (i, 0)
    return pl.pallas_call(
        _qkprep_kernel,
        grid=(NT,),
        in_specs=[pl.BlockSpec((TM, ATTN_W), row),
                  pl.BlockSpec((TM, KV_W), lambda i: (i, 0)),
                  pl.BlockSpec((TM, KV_W), lambda i: (i, 1)),
                  pl.BlockSpec((TM, LANES), row),
                  pl.BlockSpec((TM, LANES), row),
                  pl.BlockSpec((TM, LANES), row),
                  pl.BlockSpec((2, LANES), lambda i: (0, 0)),
                  pl.BlockSpec((LANES, LANES), lambda i: (0, 0))],
        out_specs=[pl.BlockSpec((TM, ATTN_W), row),
                   pl.BlockSpec((TM, KV_W), row),
                   pl.BlockSpec((TM, KV_W), row)],
        out_shape=[jax.ShapeDtypeStruct((R, ATTN_W), bf16),
                   jax.ShapeDtypeStruct((R, KV_W), bf16),
                   jax.ShapeDtypeStruct((R, KV_W), bf16)],
        compiler_params=_cparams(("arbitrary",)),
        name="qkprep",
    )(q, kv, kv, cos, sa, sb, g2, ones_bd)


NQB = R // ABLK
FIRST_LAT = CTX // ABLK
NKEYS = CTX + 3 * ABLK


def _attn_kernel(sink_ref, q_ref, kc_ref, kp_ref, kk_ref, kn_ref, vc_ref, vp_ref, vk_ref, vn_ref, o_ref):
    i = pl.program_id(0)
    kf = jnp.concatenate([kc_ref[...], kp_ref[...], kk_ref[...], kn_ref[...]], axis=0).astype(f32)
    vf = jnp.concatenate([vc_ref[...], vp_ref[...], vk_ref[...], vn_ref[...]], axis=0).astype(f32)
    lane = lax.broadcasted_iota(i32, (NKEYS, LANES), 1)
    low = lane < HEAD_DIM
    kr = pltpu.roll(kf, HEAD_DIM, 1)
    vr = pltpu.roll(vf, HEAD_DIM, 1)
    zero = jnp.zeros_like(kf)
    k_lo = (jnp.where(low, kf, zero), jnp.where(low, kr, zero))
    k_hi = (jnp.where(low, zero, kr), jnp.where(low, zero, kf))
    v_lo = (jnp.where(low, vf, zero), jnp.where(low, vr, zero))
    v_hi = (jnp.where(low, zero, vr), jnp.where(low, zero, vf))

    qi = lax.broadcasted_iota(i32, (ABLK, NKEYS), 0)
    col = lax.broadcasted_iota(i32, (ABLK, NKEYS), 1)
    kj = col - CTX
    in_window = jnp.abs(qi + ABLK - kj) <= WINDOW
    key_block = i - 1 + jnp.right_shift(kj, 7)
    first_ok = jnp.where(i >= FIRST_LAT, FIRST_LAT, NQB)
    mask = (col < CTX) | (in_window & (key_block >= first_ok) & (key_block < NQB))
    lane_q = lax.broadcasted_iota(i32, (ABLK, LANES), 1)

    contract_last = (((1,), (1,)), ((), ()))
    for h in range(2):
        qs = jnp.concatenate([q_ref[:, (h * 4 + p) * LANES:(h * 4 + p + 1) * LANES] for p in range(4)], axis=0)
        s_par = (lax.dot_general(qs, k_lo[h].astype(bf16), contract_last, preferred_element_type=f32),
                 lax.dot_general(qs, k_hi[h].astype(bf16), contract_last, preferred_element_type=f32))
        probs = ([], [])
        rden = ([], [])
        for par in range(2):
            for p in range(4):
                sink = sink_ref[h * 8 + 2 * p + par]
                s = jnp.where(mask, s_par[par][p * ABLK:(p + 1) * ABLK], NEG_INF)
                m = jnp.maximum(jnp.max(s, axis=-1, keepdims=True), sink)
                e = jnp.exp(s - m)
                den = jnp.sum(e, axis=-1, keepdims=True) + jnp.exp(sink - m)
                probs[par].append(e.astype(bf16))
                rden[par].append(1.0 / den)
        o = (jnp.dot(jnp.concatenate(probs[0], axis=0), v_lo[h].astype(bf16), preferred_element_type=f32)
             + jnp.dot(jnp.concatenate(probs[1], axis=0), v_hi[h].astype(bf16), preferred_element_type=f32))
        for p in range(4):
            scale = jnp.where(lane_q < HEAD_DIM, rden[0][p], rden[1][p])
            o_ref[:, (h * 4 + p) * LANES:(h * 4 + p + 1) * LANES] = o[p * ABLK:(p + 1) * ABLK] * scale


def _attention(sink, q, k, v):
    clip = lambda b: jnp.clip(b, FIRST_LAT, NQB - 1)
    kv_specs = [pl.BlockSpec((CTX, KV_W), lambda i: (0, 0)),
                pl.BlockSpec((ABLK, KV_W), lambda i: (clip(i - 1), 0)),
                pl.BlockSpec((ABLK, KV_W), lambda i: (i, 0)),
                pl.BlockSpec((ABLK, KV_W), lambda i: (clip(i + 1), 0))]
    return pl.pallas_call(
        _attn_kernel,
        grid=(NQB,),
        in_specs=[pl.BlockSpec(memory_space=pltpu.SMEM),
                  pl.BlockSpec((ABLK, ATTN_W), lambda i: (i, 0))] + kv_specs + kv_specs,
        out_specs=pl.BlockSpec((ABLK, ATTN_W), lambda i: (i, 0)),
        out_shape=jax.ShapeDtypeStruct((R, ATTN_W), f32),
        compiler_params=_cparams(("arbitrary",)),
        name="attention",
    )(sink, q, k, k, k, k, v, v, v, v)


def _outproj_kernel(layer, split, attn_ref, u_ref, b_ref, c_ref, up_ref, cp_ref, un_ref, cn_ref, x_ref, *refs):
    ctx_ref, refs = (refs[0], refs[1:]) if split else (None, refs)
    tab_ref, gout_ref, cw_ref, w_hbm, wr_ref, xo_ref, h2_ref, lg_ref, wbuf, stage, sems = refs
    i = pl.program_id(0)

    @pl.when(i == 0)
    def _():
        _load_weight_bf16(w_hbm.at[layer], wbuf, stage, sems)

    stream = jnp.where(i == 0, 0, 1)
    prev_ok = (i >= 2).astype(f32)
    next_ok = ((i >= 1) & (i < pl.num_programs(0) - 1)).astype(f32)
    w = c_ref[...] * u_ref[...]
    w_before = cp_ref[SUBLANES - 1:SUBLANES, :] * up_ref[SUBLANES - 1:SUBLANES, :] * prev_ok
    w_after = cn_ref[0:1, :] * un_ref[0:1, :] * next_ok
    rows = lax.broadcasted_iota(i32, (TM, 1), 0)
    w_prev = jnp.where(rows == 0, w_before, pltpu.roll(w, 1, 0))
    w_next = jnp.where(rows == TM - 1, w_after, pltpu.roll(w, TM - 1, 0))
    conv = b_ref[...] * (cw_ref[0:1, :] * w_prev + cw_ref[1:2, :] * w + cw_ref[2:3, :] * w_next)

    attn = attn_ref[...]
    a_n = attn * _rms_scale(attn) * gout_ref[0:1, :]
    c_n = conv * _rms_scale(conv) * gout_ref[1:2, :]
    cat = jnp.concatenate([a_n, c_n], axis=1).astype(bf16)
    mixed = jnp.dot(cat, wbuf[...], preferred_element_type=f32)
    xn = _tile_rows(i, x_ref, ctx_ref) + tab_ref[stream, 0:1, :] * mixed
    xo_ref[...] = xn

    h2 = xn * _rms_scale(xn) * tab_ref[stream, 1:2, :] + tab_ref[stream, 2:3, :]
    h2_ref[...] = h2
    hi = h2.astype(bf16)
    lo = (h2 - hi.astype(f32)).astype(bf16)
    lg_ref[...] = (jnp.dot(hi, wr_ref[0], preferred_element_type=f32)
                   + jnp.dot(lo, wr_ref[0], preferred_element_type=f32)
                   + jnp.dot(hi, wr_ref[1], preferred_element_type=f32))


def _outproj(layer, attn, u, b, c, xs, tab, gout, cw, w_out, wr):
    halo = TM // SUBLANES
    last8 = R // SUBLANES - 1
    row = lambda i: (i, 0)
    prev8 = lambda i: (jnp.maximum(i * halo - 1, 0), 0)
    next8 = lambda i: (jnp.minimum((i + 1) * halo, last8), 0)
    split = len(xs) == 2
    return pl.pallas_call(
        functools.partial(_outproj_kernel, layer, split),
        grid=(NT,),
        in_specs=[pl.BlockSpec((TM, ATTN_W), row),
                  pl.BlockSpec((TM, CONV_W), row),
                  pl.BlockSpec((TM, CONV_W), row),
                  pl.BlockSpec((TM, CONV_W), row),
                  pl.BlockSpec((SUBLANES, CONV_W), prev8),
                  pl.BlockSpec((SUBLANES, CONV_W), prev8),
                  pl.BlockSpec((SUBLANES, CONV_W), next8),
                  pl.BlockSpec((SUBLANES, CONV_W), next8)]
                 + _row_specs(split)
                 + [pl.BlockSpec((2, 3, D), lambda i: (0, 0, 0)),
                    pl.BlockSpec((2, CONV_W), lambda i: (0, 0)),
                    pl.BlockSpec((3, CONV_W), lambda i: (0, 0)),
                    pl.BlockSpec(memory_space=pl.ANY),
                    pl.BlockSpec((2, D, LANES), lambda i: (0, 0, 0))],
        out_specs=[pl.BlockSpec((TM, D), row),
                   pl.BlockSpec((TM, D), row),
                   pl.BlockSpec((TM, LANES), row)],
        out_shape=[jax.ShapeDtypeStruct((R, D), f32),
                   jax.ShapeDtypeStruct((R, D), f32),
                   jax.ShapeDtypeStruct((R, LANES), f32)],
        scratch_shapes=[pltpu.VMEM((D, D), bf16), pltpu.VMEM((2, WCH, D), f32), pltpu.SemaphoreType.DMA((2,))],
        compiler_params=_cparams(("arbitrary",)),
        name="outproj",
    )(attn, u, b, c, u, c, u, c, *xs, tab, gout, cw, w_out, wr)


def _router_kernel(bias_ref, lt_ref, upper_ref, lower_ref, ones_ref, e_ref, g_ref, d_ref, cnt_ref):
    score = [1.0 / (1.0 + jnp.exp(-lt_ref[e])) for e in range(N_EXPERTS)]
    sel = [score[e] + bias_ref[e] for e in range(N_EXPERTS)]

    def top2_sum(a, b, c, d):
        p, q = jnp.maximum(a, b), jnp.minimum(a, b)
        r, s = jnp.maximum(c, d), jnp.minimum(c, d)
        return jnp.maximum(p, r) + jnp.maximum(jnp.minimum(p, r), jnp.maximum(q, s))

    gscore = [top2_sum(*sel[EPG * g:EPG * (g + 1)]) for g in range(N_GROUPS)]
    best, gidx = gscore[0], jnp.zeros(gscore[0].shape, i32)
    for g in range(1, N_GROUPS):
        take = gscore[g] > best
        best = jnp.where(take, gscore[g], best)
        gidx = jnp.where(take, g, gidx)

    def pick_group(vals):
        out = []
        for j in range(EPG):
            v = vals[j]
            for g in range(1, N_GROUPS):
                v = jnp.where(gidx == g, vals[EPG * g + j], v)
            out.append(v)
        return out

    in_sel = pick_group(sel)
    in_score = pick_group(score)

    def argmax_first(vals, excluded):
        bv, bi = None, None
        for j in range(EPG):
            v = vals[j] if excluded is None else jnp.where(excluded == j, -jnp.inf, vals[j])
            if bv is None:
                bv, bi = v, jnp.zeros(v.shape, i32)
            else:
                take = v > bv
                bv = jnp.where(take, v, bv)
                bi = jnp.where(take, j, bi)
        return bi

    i1 = argmax_first(in_sel, None)
    i2 = argmax_first(in_sel, i1)

    def pick_local(vals, idx):
        v = vals[0]
        for j in range(1, EPG):
            v = jnp.where(idx == j, vals[j], v)
        return v

    s1, s2 = pick_local(in_score, i1), pick_local(in_score, i2)
    tot = s1 + s2
    e1 = gidx * EPG + i1
    e2 = gidx * EPG + i2
    e_ref[0], e_ref[1] = e1, e2
    g_ref[0], g_ref[1] = s1 / tot, s2 / tot

    tok = (lax.broadcasted_iota(i32, (RT, LANES), 0) * LANES + lax.broadcasted_iota(i32, (RT, LANES), 1))
    valid = tok < R
    onehot = [(((e1 == e) | (e2 == e)) & valid).astype(f32) for e in range(N_EXPERTS)]
    stack = jnp.concatenate(onehot, axis=0).astype(bf16)
    within = jnp.dot(stack, upper_ref[...], preferred_element_type=f32)
    rowtot = jnp.dot(stack, ones_ref[...], preferred_element_type=f32)
    d1 = jnp.zeros((RT, LANES), i32)
    d2 = jnp.zeros((RT, LANES), i32)
    seg_start = jnp.zeros((1, LANES), i32)
    for e in range(N_EXPERTS):
        rt_e = rowtot[e * RT:(e + 1) * RT]
        before = jnp.dot(lower_ref[...], rt_e.astype(bf16), preferred_element_type=f32)
        rank_e = within[e * RT:(e + 1) * RT] + before
        count_e = (before + rt_e)[RT - 1:RT, :]
        cnt_ref[e:e + 1, :] = count_e
        slot = seg_start + rank_e.astype(i32)
        d1 = jnp.where(e1 == e, slot, d1)
        d2 = jnp.where(e2 == e, slot, d2)
        blocks_e = jnp.right_shift(count_e.astype(i32) + (EBLK - 1), EBLK_SHIFT)
        seg_start = seg_start + jnp.left_shift(blocks_e, EBLK_SHIFT)
    d_ref[0], d_ref[1] = d1, d2


def _router(bias, logits_t, upper, lower, ones):
    full = lambda *shape: pl.BlockSpec(shape, lambda: (0,) * len(shape))
    return pl.pallas_call(
        _router_kernel,
        in_specs=[pl.BlockSpec(memory_space=pltpu.SMEM),
                  full(N_EXPERTS, RT, LANES), full(LANES, LANES), full(RT, RT), full(LANES, LANES)],
        out_specs=[full(2, RT, LANES), full(2, RT, LANES), full(2, RT, LANES), full(N_EXPERTS, LANES)],
        out_shape=[jax.ShapeDtypeStruct((2, RT, LANES), i32),
                   jax.ShapeDtypeStruct((2, RT, LANES), f32),
                   jax.ShapeDtypeStruct((2, RT, LANES), i32),
                   jax.ShapeDtypeStruct((N_EXPERTS, LANES), f32)],
        compiler_params=pltpu.CompilerParams(vmem_limit_bytes=VMEM_LIMIT),
        name="router",
    )(bias, logits_t, upper, lower, ones)


ROW_UNROLL = TM


HSLOTS = 3


def _dispatch_kernel(pend_ref, dest_ref, h_hbm, xs_hbm, hbuf, zbuf, lsem, rsem, zsem):
    i = pl.program_id(0)
    last = pl.num_programs(0) - 1
    slot = lax.rem(i, HSLOTS)
    prev_slot = lax.rem(i + (HSLOTS - 1), HSLOTS)
    next_slot = lax.rem(i + 1, HSLOTS)

    def load(tile, s):
        rows = pl.ds(pl.multiple_of(tile * TM, TM), TM)
        return pltpu.make_async_copy(h_hbm.at[rows], hbuf.at[s], lsem.at[s])

    @pl.when(i == 0)
    def _():
        load(0, 0).start()
        zbuf[...] = jnp.zeros_like(zbuf)

        def zero_copy(e):
            first = pl.multiple_of(pend_ref[e] - EBLK, EBLK)
            return pltpu.make_async_copy(zbuf, xs_hbm.at[pl.ds(first, EBLK)], zsem)

        def nonempty(e):
            return pend_ref[e] > (pend_ref[e - 1] if e else 0)

        for e in range(N_EXPERTS):
            pl.when(nonempty(e))(lambda e=e: zero_copy(e).start())
        for e in range(N_EXPERTS):
            pl.when(nonempty(e))(lambda e=e: zero_copy(e).wait())

        def tail_copy(j):
            return pltpu.make_async_copy(zbuf, xs_hbm.at[pl.ds(pl.multiple_of(j * EBLK, EBLK), EBLK)], zsem)

        first_unused = jnp.right_shift(pend_ref[N_EXPERTS - 1], EBLK_SHIFT)
        lax.fori_loop(first_unused, NBLK, lambda j, c: (tail_copy(j).start(), c)[1], 0)
        lax.fori_loop(first_unused, NBLK, lambda j, c: (tail_copy(j).wait(), c)[1], 0)

    @pl.when(i < last)
    def _():
        load(i + 1, next_slot).start()

    load(i, slot).wait()

    def start(t, c):
        for k in range(2):
            pltpu.make_async_copy(hbuf.at[slot, pl.ds(t, 1)], xs_hbm.at[pl.ds(dest_ref[0, 0, k * TM + t], 1)],
                                  rsem.at[slot]).start(priority=k)
        return c

    def drain(s):
        def wait(t, c):
            for k in range(2):
                pltpu.make_async_copy(hbuf.at[s, pl.ds(0, 1)], xs_hbm.at[pl.ds(0, 1)], rsem.at[s]).wait()
            return c
        lax.fori_loop(0, TM, wait, 0, unroll=ROW_UNROLL)

    lax.fori_loop(0, TM, start, 0, unroll=ROW_UNROLL)
    pl.when(i > 0)(lambda: drain(prev_slot))
    pl.when(i == last)(lambda: drain(slot))


def _dispatch(pend, dest3, h2):
    return pl.pallas_call(
        _dispatch_kernel,
        grid_spec=pltpu.PrefetchScalarGridSpec(
            num_scalar_prefetch=1,
            grid=(NT,),
            in_specs=[pl.BlockSpec((1, 1, 2 * TM), lambda i, pend: (i, 0, 0), memory_space=pltpu.SMEM),
                      pl.BlockSpec(memory_space=pl.ANY)],
            out_specs=pl.BlockSpec(memory_space=pl.ANY),
            scratch_shapes=[pltpu.VMEM((HSLOTS, TM, D), f32), pltpu.VMEM((EBLK, D), f32),
                            pltpu.SemaphoreType.DMA((HSLOTS,)), pltpu.SemaphoreType.DMA((HSLOTS,)),
                            pltpu.SemaphoreType.DMA(())]),
        out_shape=jax.ShapeDtypeStruct((NROWS, D), f32),
        compiler_params=_cparams(("arbitrary",)),
        name="dispatch",
    )(pend, dest3, h2)


NCH = 4
FCH = D_FF // NCH
NSTEPS = NBLK + (N_EXPERTS + 1) * NCH
(T_COMP, T_XBLK, T_CONV, T_CEXP, T_CCH, T_CSLOT, T_SLOT, T_ZERO, T_OBLK) = range(9)
T_STRIDE = 16


def _expert_schedule(counts):
    experts = jnp.arange(N_EXPERTS, dtype=i32)
    nb = (counts + EBLK - 1) // EBLK
    bstart = jnp.cumsum(nb) - nb
    nblocks = jnp.sum(nb)
    steps = jnp.maximum(nb, NCH)
    send = NCH + jnp.cumsum(steps)
    total = send[-1]
    s = jnp.arange(NSTEPS, dtype=i32)
    e_s = jnp.minimum(jnp.sum((s[:, None] >= send[None, :]).astype(i32), axis=1), N_EXPERTS - 1)
    onehot = (e_s[:, None] == experts[None, :]).astype(i32)
    pick = lambda table: jnp.sum(onehot * table[None, :], axis=1)
    i_s = s - pick(send - steps)
    nb_s = pick(nb)
    pre = s < NCH
    active = (s >= NCH) & (s < total)
    done = s >= total
    comp = active & (i_s < nb_s)
    xblk = jnp.clip(pick(bstart) + jnp.minimum(i_s, nb_s - 1), 0, nblocks - 1)
    xblk = jnp.where(pre, 0, jnp.where(done, nblocks - 1, xblk))
    has_next = e_s < N_EXPERTS - 1
    conv = pre | (active & (i_s < NCH) & has_next)
    cexp = jnp.where(pre, 0, jnp.minimum(e_s + 1, N_EXPERTS - 1))
    cch = jnp.where(pre, s, jnp.where(has_next & active, jnp.minimum(i_s, NCH - 1), NCH - 1))
    zblk = nblocks + (s - total)
    zero = done & (zblk < NBLK)
    oblk = jnp.where(done, jnp.minimum(zblk, NBLK - 1), xblk)
    cols = [comp, xblk, conv, cexp, cch, cexp % 2, e_s % 2, zero, oblk]
    cols = [col.astype(i32) for col in cols] + [jnp.zeros_like(s)] * (T_STRIDE - 9)
    return jnp.stack(cols, axis=1).reshape(NSTEPS * T_STRIDE)


def _expert_kernel(tab_ref, x_ref, g_ref, u_ref, d_ref, o_ref, wg_s, wu_s, wd_s):
    base = pl.program_id(0) * T_STRIDE

    @pl.when(tab_ref[base + T_CONV] == 1)
    def _():
        slot, c = tab_ref[base + T_CSLOT], tab_ref[base + T_CCH]
        wg_s[slot, c] = g_ref[0, 0].astype(bf16)
        wu_s[slot, c] = u_ref[0, 0].astype(bf16)
        wd_s[slot, c] = d_ref[0, 0].astype(bf16)

    @pl.when(tab_ref[base + T_COMP] == 1)
    def _():
        slot = tab_ref[base + T_SLOT]
        x = x_ref[...].astype(bf16)
        y = None
        for c in range(NCH):
            a = jnp.dot(x, wg_s[slot, c], preferred_element_type=f32)
            b = jnp.dot(x, wu_s[slot, c], preferred_element_type=f32)
            h = (a / (1.0 + jnp.exp(-a)) * b).astype(bf16)
            yc = jnp.dot(h, wd_s[slot, c], preferred_element_type=f32)
            y = yc if y is None else y + yc
        o_ref[...] = y

    @pl.when(tab_ref[base + T_ZERO] == 1)
    def _():
        o_ref[...] = jnp.zeros_like(o_ref)


def _experts(layer, schedule, xs, w_gate, w_up, w_down):
    at = lambda s, tab, col: tab[s * T_STRIDE + col]
    up_spec = pl.BlockSpec((1, 1, D, FCH), lambda s, tab: (layer, at(s, tab, T_CEXP), 0, at(s, tab, T_CCH)))
    return pl.pallas_call(
        _expert_kernel,
        grid_spec=pltpu.PrefetchScalarGridSpec(
            num_scalar_prefetch=1,
            grid=(NSTEPS,),
            in_specs=[pl.BlockSpec((EBLK, D), lambda s, tab: (at(s, tab, T_XBLK), 0)),
                      up_spec, up_spec,
                      pl.BlockSpec((1, 1, FCH, D),
                                   lambda s, tab: (layer, at(s, tab, T_CEXP), at(s, tab, T_CCH), 0))],
            out_specs=pl.BlockSpec((EBLK, D), lambda s, tab: (at(s, tab, T_OBLK), 0)),
            scratch_shapes=[pltpu.VMEM((2, NCH, D, FCH), bf16), pltpu.VMEM((2, NCH, D, FCH), bf16),
                            pltpu.VMEM((2, NCH, FCH, D), bf16)]),
        out_shape=jax.ShapeDtypeStruct((NROWS, D), f32),
        compiler_params=_cparams(("arbitrary",)),
        name="experts",
    )(schedule, xs, w_gate, w_up, w_down)


def _combine_kernel(first, dest_ref, dnext_ref, ys_hbm, x_ref, gates_ref, gate_ref, o_ref, ybuf, sems):
    i = pl.program_id(0)
    slot = jnp.bitwise_and(i, 1)
    stream = jnp.where(i + first == 0, 0, 1)

    def start_tile(idx_ref, s):
        def body(t, c):
            for k in range(2):
                pltpu.make_async_copy(ys_hbm.at[pl.ds(idx_ref[0, 0, k * TM + t], 1)],
                                      ybuf.at[s, k, pl.ds(t, 1)], sems.at[s]).start(priority=k)
            return c
        lax.fori_loop(0, TM, body, 0, unroll=ROW_UNROLL)

    @pl.when(i == 0)
    def _():
        start_tile(dest_ref, 0)

    @pl.when(i + 1 < pl.num_programs(0))
    def _():
        start_tile(dnext_ref, 1 - slot)

    def wait(t, c):
        for k in range(2):
            pltpu.make_async_copy(ys_hbm.at[pl.ds(0, 1)], ybuf.at[slot, k, pl.ds(0, 1)], sems.at[slot]).wait()
        return c

    lax.fori_loop(0, TM, wait, 0, unroll=ROW_UNROLL)
    y = ybuf[slot, 0] * gates_ref[:, 0:1] + ybuf[slot, 1] * gates_ref[:, 1:2]
    o_ref[...] = x_ref[...] + gate_ref[stream] * y


def _combine(first, dest3, ys, x, gates, gate2):
    row = lambda i: (i + first, 0)
    return pl.pallas_call(
        functools.partial(_combine_kernel, first),
        grid=(NT - first,),
        in_specs=[pl.BlockSpec((1, 1, 2 * TM), lambda i: (i + first, 0, 0), memory_space=pltpu.SMEM),
                  pl.BlockSpec((1, 1, 2 * TM), lambda i: (jnp.minimum(i + first + 1, NT - 1), 0, 0),
                               memory_space=pltpu.SMEM),
                  pl.BlockSpec(memory_space=pl.ANY),
                  pl.BlockSpec((TM, D), row),
                  pl.BlockSpec((TM, 2), row),
                  pl.BlockSpec((2, 1, D), lambda i: (0, 0, 0))],
        out_specs=pl.BlockSpec((TM, D), lambda i: (i, 0)),
        out_shape=jax.ShapeDtypeStruct((R - first * TM, D), f32),
        scratch_shapes=[pltpu.VMEM((2, 2, TM, D), f32), pltpu.SemaphoreType.DMA((2,))],
        compiler_params=_cparams(("arbitrary",)),
        name="combine",
    )(dest3, dest3, ys, x, gates, gate2)


def _rope_tables():
    half = HEAD_DIM // 2
    inv_freq = (ROPE_THETA ** (-np.arange(0, half, 2, dtype=np.float32) / half)).astype(np.float32)
    t = np.arange(SEQ)
    pos = np.stack([(t // GRID_W).astype(np.float32), (t % GRID_W).astype(np.float32)], axis=1)
    d = np.arange(LANES) % HEAD_DIM
    axis, sub = d // half, d % half
    ang = (pos[:, axis] * inv_freq[sub % (half // 2)][None, :]).astype(np.float32)
    cos, sin = np.cos(ang).astype(np.float32), np.sin(ang).astype(np.float32)
    first = (sub < half // 2)[None, :]
    sa = np.where(first, -sin, 0.0).astype(np.float32)
    sb = np.where(first, 0.0, sin).astype(np.float32)
    ident = np.ones((CTX, LANES), np.float32)
    zeros = np.zeros((CTX, LANES), np.float32)
    return tuple(jnp.asarray(np.concatenate(parts)) for parts in ((ident, cos), (zeros, sa), (zeros, sb)))


def kernel(x, c, ctx, c_ctx, w_ada, b_ada, g_attn, w_in, q_norm_g, k_norm_g, sink, conv_w, g_out_attn, g_out_conv, w_out, g_ffn, w_router, router_bias, w_exp_gate, w_exp_up, w_exp_down):
    assert x.shape == (1, SEQ, D) and ctx.shape == (1, CTX, D)
    xs_in = (x[0], ctx[0])

    silu = lambda t: t * jax.nn.sigmoid(t)
    s_vec = jnp.stack([silu(c_ctx), silu(c[0])])
    mods = _adaln(jnp.broadcast_to(s_vec[:, :, None], (2, D, LANES)), w_ada, b_ada).reshape(DEPTH, 2, 6, D)

    cos, sa, sb = _rope_tables()
    lane = jnp.arange(LANES)
    ones_bd = (lane[:, None] // HEAD_DIM == lane[None, :] // HEAD_DIM).astype(bf16)
    upper = (lane[:, None] < lane[None, :]).astype(bf16)
    ones = jnp.ones((LANES, LANES), bf16)
    rt = jnp.arange(RT)
    lower = (rt[None, :] < rt[:, None]).astype(bf16)
    wr_pad = jnp.pad(w_router, ((0, 0), (0, LANES - N_EXPERTS)))
    wr_hi = wr_pad.astype(bf16)
    wr = jnp.stack([wr_hi, (wr_pad - wr_hi.astype(f32)).astype(bf16)])

    for l in range(DEPTH):
        mod = mods[l]
        ms1 = jnp.stack([g_attn[l] * (1.0 + mod[:, 1]), mod[:, 0]], axis=1)
        q, kv, u, b_gate, c_gate = _inproj(l, xs_in, ms1, w_in)
        g2 = jnp.stack([jnp.tile(q_norm_g[l] * HEAD_DIM ** -0.5, 2), jnp.tile(k_norm_g[l], 2)])
        q, k, v = _qkprep(q, kv, cos, sa, sb, g2, ones_bd)
        attn = _attention(sink[l], q, k, v)
        tab = jnp.stack([mod[:, 2], g_ffn[l] * (1.0 + mod[:, 4]), mod[:, 3]], axis=1)
        gout = jnp.stack([g_out_attn[l], g_out_conv[l]])
        xa, h2, logits = _outproj(l, attn, u, b_gate, c_gate, xs_in, tab, gout, conv_w[l], w_out, wr)

        lt = jnp.pad(logits[:, :N_EXPERTS].T, ((0, 0), (0, RT * LANES - R))).reshape(N_EXPERTS, RT, LANES)
        _, g_t, d_t, cnt = _router(router_bias, lt, upper, lower, ones)
        gates = g_t.reshape(2, RT * LANES)[:, :R].T
        dest = d_t.reshape(2, RT * LANES)[:, :R]
        dest3 = dest.reshape(2, NT, TM).transpose(1, 0, 2).reshape(NT, 1, 2 * TM)
        counts = cnt[:, 0].astype(i32)
        pend = jnp.cumsum((counts + EBLK - 1) // EBLK * EBLK)

        xs = _dispatch(pend, dest3, h2)
        ys = _experts(l, _expert_schedule(counts), xs, w_exp_gate, w_exp_up, w_exp_down)
        xa = _combine(1 if l == DEPTH - 1 else 0, dest3, ys, xa, gates, mod[:, 5][:, None, :])
        xs_in = (xa,)

    return xa[None]
```

```python
import functools

import jax
import numpy as np
import jax.numpy as jnp
from jax import lax
from jax.experimental import pallas as pl
from jax.experimental.pallas import tpu as pltpu

f32 = jnp.float32
bf16 = jnp.bfloat16
i32 = jnp.int32

D = 2048
DEPTH = 4
SEQ = 8192
CTX = 256
R = CTX + SEQ
GRID_W = 64
ATTN_W = 1024
CONV_W = 1024
HEAD_DIM = 64
N_HEADS = 16
KV_W = 128
IN_COLS = ATTN_W + 2 * KV_W + 3 * CONV_W
ABLK = 128
WINDOW = 128
N_EXPERTS = 16
N_GROUPS = 4
EPG = 4
D_FF = 1024
EPS = 1e-6
NEG_INF = -1e30
ROPE_THETA = 10000.0

LANES = 128
SUBLANES = 8
TM = 256
NT = R // TM
EBLK = 256
EBLK_SHIFT = EBLK.bit_length() - 1
assert 1 << EBLK_SHIFT == EBLK
NBLK = -(-(2 * R + N_EXPERTS * (EBLK - 1)) // EBLK)
NROWS = NBLK * EBLK
RT = LANES
VMEM_LIMIT = 56 * 1024 * 1024


def _cparams(sem):
    return pltpu.CompilerParams(dimension_semantics=sem, vmem_limit_bytes=VMEM_LIMIT)


ADA_TK = 2048
ADA_TN = 1536


def _adaln_kernel(s_ref, w_ref, b_ref, o_ref, acc_ref):
    k = pl.program_id(2)

    @pl.when(k == 0)
    def _():
        acc_ref[...] = jnp.zeros_like(acc_ref)

    for j in range(ADA_TN // LANES):
        w = w_ref[0, :, j * LANES:(j + 1) * LANES]
        for v in range(2):
            p = (w * s_ref[v]).reshape(ADA_TK // SUBLANES, SUBLANES, LANES).sum(axis=0)
            acc_ref[v, :, j * LANES:(j + 1) * LANES] += p

    @pl.when(k == pl.num_programs(2) - 1)
    def _():
        for v in range(2):
            o_ref[0, v:v + 1, :] = acc_ref[v].sum(axis=0, keepdims=True) + b_ref[0]


def _adaln(s_bcast, w_ada, b_ada):
    return pl.pallas_call(
        _adaln_kernel,
        grid=(DEPTH, 6 * D // ADA_TN, D // ADA_TK),
        in_specs=[pl.BlockSpec((2, ADA_TK, LANES), lambda l, n, k: (0, k, 0)),
                  pl.BlockSpec((1, ADA_TK, ADA_TN), lambda l, n, k: (l, k, n)),
                  pl.BlockSpec((1, 1, ADA_TN), lambda l, n, k: (l, 0, n))],
        out_specs=pl.BlockSpec((1, 2, ADA_TN), lambda l, n, k: (l, 0, n)),
        out_shape=jax.ShapeDtypeStruct((DEPTH, 2, 6 * D), f32),
        scratch_shapes=[pltpu.VMEM((2, SUBLANES, ADA_TN), f32)],
        compiler_params=_cparams(("arbitrary", "arbitrary", "arbitrary")),
        name="adaln",
    )(s_bcast, w_ada, b_ada.reshape(DEPTH, 1, 6 * D))


def _rms_scale(x):
    return lax.rsqrt(jnp.mean(x * x, axis=-1, keepdims=True) + EPS)


WCH = 256


def _load_weight_bf16(w_hbm, wbuf, stage, sems):
    n = wbuf.shape[0] // WCH

    def chunk(c):
        return pltpu.make_async_copy(w_hbm.at[pl.ds(c * WCH, WCH)], stage.at[c % 2], sems.at[c % 2])

    chunk(0).start()
    for c in range(n):
        if c + 1 < n:
            chunk(c + 1).start()
        chunk(c).wait()
        wbuf[pl.ds(c * WCH, WCH), :] = stage[c % 2].astype(bf16)


Q0, K0, V0, U0, B0, C0 = 0, ATTN_W, ATTN_W + KV_W, ATTN_W + 2 * KV_W, ATTN_W + 2 * KV_W + CONV_W, \
    ATTN_W + 2 * KV_W + 2 * CONV_W


def _tile_rows(i, x_ref, ctx_ref):
    if ctx_ref is None:
        return x_ref[...]
    return jnp.where(i == 0, ctx_ref[...], x_ref[...])


def _row_specs(split):
    if split:
        return [pl.BlockSpec((TM, D), lambda i: (jnp.maximum(i - 1, 0), 0)), pl.BlockSpec((TM, D), lambda i: (0, 0))]
    return [pl.BlockSpec((TM, D), lambda i: (i, 0))]


def _inproj_kernel(layer, split, x_ref, *refs):
    ctx_ref, refs = (refs[0], refs[1:]) if split else (None, refs)
    ms_ref, w_hbm, q_ref, kv_ref, u_ref, b_ref, c_ref, wbuf, stage, sems = refs

    @pl.when(pl.program_id(0) == 0)
    def _():
        _load_weight_bf16(w_hbm.at[layer], wbuf, stage, sems)

    stream = jnp.where(pl.program_id(0) == 0, 0, 1)
    x = _tile_rows(pl.program_id(0), x_ref, ctx_ref)
    gain = ms_ref[stream, 0:1, :]
    shift = ms_ref[stream, 1:2, :]
    h = (x * _rms_scale(x) * gain + shift).astype(bf16)
    p = jnp.dot(h, wbuf[...], preferred_element_type=f32)
    q_ref[...] = p[:, Q0:K0]
    kv_ref[...] = p[:, K0:U0]
    u_ref[...] = p[:, U0:B0]
    b_ref[...] = p[:, B0:C0]
    c_ref[...] = p[:, C0:]


def _inproj(layer, xs, ms, w_in):
    row = lambda i: (i, 0)
    widths = (ATTN_W, 2 * KV_W, CONV_W, CONV_W, CONV_W)
    split = len(xs) == 2
    return pl.pallas_call(
        functools.partial(_inproj_kernel, layer, split),
        grid=(NT,),
        in_specs=_row_specs(split) + [pl.BlockSpec((2, 2, D), lambda i: (0, 0, 0)),
                                      pl.BlockSpec(memory_space=pl.ANY)],
        out_specs=[pl.BlockSpec((TM, n), row) for n in widths],
        out_shape=[jax.ShapeDtypeStruct((R, n), f32) for n in widths],
        scratch_shapes=[pltpu.VMEM((D, IN_COLS), bf16), pltpu.VMEM((2, WCH, IN_COLS), f32),
                        pltpu.SemaphoreType.DMA((2,))],
        compiler_params=_cparams(("arbitrary",)),
        name="inproj",
    )(*xs, ms, w_in)


def _qkprep_kernel(q_ref, k_ref, v_ref, cos_ref, sa_ref, sb_ref, g_ref, ones_ref, qo_ref, ko_ref, vo_ref):
    cos, sa, sb = cos_ref[...], sa_ref[...], sb_ref[...]
    ones = ones_ref[...]

    def norm_rope(xc, g):
        sq = xc * xc
        hi = sq.astype(bf16)
        lo = (sq - hi.astype(f32)).astype(bf16)
        ss = (jnp.dot(hi, ones, preferred_element_type=f32)
              + jnp.dot(lo, ones, preferred_element_type=f32))
        y = xc * lax.rsqrt(ss * (1.0 / HEAD_DIM) + EPS) * g
        return y * cos + pltpu.roll(y, LANES - 16, 1) * sa + pltpu.roll(y, 16, 1) * sb

    for j in range(ATTN_W // LANES):
        sl = slice(j * LANES, (j + 1) * LANES)
        qo_ref[:, sl] = norm_rope(q_ref[:, sl], g_ref[0:1, :]).astype(bf16)
    ko_ref[...] = norm_rope(k_ref[...], g_ref[1:2, :]).astype(bf16)
    vo_ref[...] = v_ref[...].astype(bf16)


def _qkprep(q, kv, cos, sa, sb, g2, ones_bd):
    row = lambda i: (i, 0)
    return pl.pallas_call(
        _qkprep_kernel,
        grid=(NT,),
        in_specs=[pl.BlockSpec((TM, ATTN_W), row),
                  pl.BlockSpec((TM, KV_W), lambda i: (i, 0)),
                  pl.BlockSpec((TM, KV_W), lambda i: (i, 1)),
                  pl.BlockSpec((TM, LANES), row),
                  pl.BlockSpec((TM, LANES), row),
                  pl.BlockSpec((TM, LANES), row),
                  pl.BlockSpec((2, LANES), lambda i: (0, 0)),
                  pl.BlockSpec((LANES, LANES), lambda i: (0, 0))],
        out_specs=[pl.BlockSpec((TM, ATTN_W), row),
                   pl.BlockSpec((TM, KV_W), row),
                   pl.BlockSpec((TM, KV_W), row)],
        out_shape=[jax.ShapeDtypeStruct((R, ATTN_W), bf16),
                   jax.ShapeDtypeStruct((R, KV_W), bf16),
                   jax.ShapeDtypeStruct((R, KV_W), bf16)],
        compiler_params=_cparams(("arbitrary",)),
        name="qkprep",
    )(q, kv, kv, cos, sa, sb, g2, ones_bd)


NQB = R // ABLK
FIRST_LAT = CTX // ABLK
NKEYS = CTX + 3 * ABLK


def _attn_kernel(sink_ref, q_ref, kc_ref, kp_ref, kk_ref, kn_ref, vc_ref, vp_ref, vk_ref, vn_ref, o_ref):
    i = pl.program_id(0)
    kf = jnp.concatenate([kc_ref[...], kp_ref[...], kk_ref[...], kn_ref[...]], axis=0).astype(f32)
    vf = jnp.concatenate([vc_ref[...], vp_ref[...], vk_ref[...], vn_ref[...]], axis=0).astype(f32)
    lane = lax.broadcasted_iota(i32, (NKEYS, LANES), 1)
    low = lane < HEAD_DIM
    kr = pltpu.roll(kf, HEAD_DIM, 1)
    vr = pltpu.roll(vf, HEAD_DIM, 1)
    zero = jnp.zeros_like(kf)
    k_lo = (jnp.where(low, kf, zero), jnp.where(low, kr, zero))
    k_hi = (jnp.where(low, zero, kr), jnp.where(low, zero, kf))
    v_lo = (jnp.where(low, vf, zero), jnp.where(low, vr, zero))
    v_hi = (jnp.where(low, zero, vr), jnp.where(low, zero, vf))

    qi = lax.broadcasted_iota(i32, (ABLK, NKEYS), 0)
    col = lax.broadcasted_iota(i32, (ABLK, NKEYS), 1)
    kj = col - CTX
    in_window = jnp.abs(qi + ABLK - kj) <= WINDOW
    key_block = i - 1 + jnp.right_shift(kj, 7)
    first_ok = jnp.where(i >= FIRST_LAT, FIRST_LAT, NQB)
    mask = (col < CTX) | (in_window & (key_block >= first_ok) & (key_block < NQB))
    lane_q = lax.broadcasted_iota(i32, (ABLK, LANES), 1)

    contract_last = (((1,), (1,)), ((), ()))
    for h in range(2):
        qs = jnp.concatenate([q_ref[:, (h * 4 + p) * LANES:(h * 4 + p + 1) * LANES] for p in range(4)], axis=0)
        s_par = (lax.dot_general(qs, k_lo[h].astype(bf16), contract_last, preferred_element_type=f32),
                 lax.dot_general(qs, k_hi[h].astype(bf16), contract_last, preferred_element_type=f32))
        probs = ([], [])
        rden = ([], [])
        for par in range(2):
            for p in range(4):
                sink = sink_ref[h * 8 + 2 * p + par]
                s = jnp.where(mask, s_par[par][p * ABLK:(p + 1) * ABLK], NEG_INF)
                m = jnp.maximum(jnp.max(s, axis=-1, keepdims=True), sink)
                e = jnp.exp(s - m)
                den = jnp.sum(e, axis=-1, keepdims=True) + jnp.exp(sink - m)
                probs[par].append(e.astype(bf16))
                rden[par].append(1.0 / den)
        o = (jnp.dot(jnp.concatenate(probs[0], axis=0), v_lo[h].astype(bf16), preferred_element_type=f32)
             + jnp.dot(jnp.concatenate(probs[1], axis=0), v_hi[h].astype(bf16), preferred_element_type=f32))
        for p in range(4):
            scale = jnp.where(lane_q < HEAD_DIM, rden[0][p], rden[1][p])
            o_ref[:, (h * 4 + p) * LANES:(h * 4 + p + 1) * LANES] = o[p * ABLK:(p + 1) * ABLK] * scale


def _attention(sink, q, k, v):
    clip = lambda b: jnp.clip(b, FIRST_LAT, NQB - 1)
    kv_specs = [pl.BlockSpec((CTX, KV_W), lambda i: (0, 0)),
                pl.BlockSpec((ABLK, KV_W), lambda i: (clip(i - 1), 0)),
                pl.BlockSpec((ABLK, KV_W), lambda i: (i, 0)),
                pl.BlockSpec((ABLK, KV_W), lambda i: (clip(i + 1), 0))]
    return pl.pallas_call(
        _attn_kernel,
        grid=(NQB,),
        in_specs=[pl.BlockSpec(memory_space=pltpu.SMEM),
                  pl.BlockSpec((ABLK, ATTN_W), lambda i: (i, 0))] + kv_specs + kv_specs,
        out_specs=pl.BlockSpec((ABLK, ATTN_W), lambda i: (i, 0)),
        out_shape=jax.ShapeDtypeStruct((R, ATTN_W), f32),
        compiler_params=_cparams(("arbitrary",)),
        name="attention",
    )(sink, q, k, k, k, k, v, v, v, v)


DP = D // 2
HIGH_HALF = 0xFFFF0000


def _pack_bf16_pairs(t):
    lo = lax.bitcast_convert_type(t[:, :DP].astype(bf16).astype(f32), jnp.uint32)
    hi = lax.bitcast_convert_type(t[:, DP:].astype(bf16).astype(f32), jnp.uint32)
    return jnp.right_shift(lo, jnp.uint32(16)) | (hi & jnp.uint32(HIGH_HALF))


def _unpack_bf16_pairs(w):
    lo = lax.bitcast_convert_type(jnp.left_shift(w, jnp.uint32(16)), f32)
    hi = lax.bitcast_convert_type(w & jnp.uint32(HIGH_HALF), f32)
    return jnp.concatenate([lo.astype(bf16), hi.astype(bf16)], axis=1)


def _outproj_kernel(layer, split, attn_ref, u_ref, b_ref, c_ref, up_ref, cp_ref, un_ref, cn_ref, x_ref, *refs):
    ctx_ref, refs = (refs[0], refs[1:]) if split else (None, refs)
    tab_ref, gout_ref, cw_ref, w_hbm, wr_ref, xo_ref, h2_ref, lg_ref, wbuf, stage, sems = refs
    i = pl.program_id(0)

    @pl.when(i == 0)
    def _():
        _load_weight_bf16(w_hbm.at[layer], wbuf, stage, sems)

    stream = jnp.where(i == 0, 0, 1)
    prev_ok = (i >= 2).astype(f32)
    next_ok = ((i >= 1) & (i < pl.num_programs(0) - 1)).astype(f32)
    w = c_ref[...] * u_ref[...]
    w_before = cp_ref[SUBLANES - 1:SUBLANES, :] * up_ref[SUBLANES - 1:SUBLANES, :] * prev_ok
    w_after = cn_ref[0:1, :] * un_ref[0:1, :] * next_ok
    rows = lax.broadcasted_iota(i32, (TM, 1), 0)
    w_prev = jnp.where(rows == 0, w_before, pltpu.roll(w, 1, 0))
    w_next = jnp.where(rows == TM - 1, w_after, pltpu.roll(w, TM - 1, 0))
    conv = b_ref[...] * (cw_ref[0:1, :] * w_prev + cw_ref[1:2, :] * w + cw_ref[2:3, :] * w_next)

    attn = attn_ref[...]
    a_n = attn * _rms_scale(attn) * gout_ref[0:1, :]
    c_n = conv * _rms_scale(conv) * gout_ref[1:2, :]
    cat = jnp.concatenate([a_n, c_n], axis=1).astype(bf16)
    mixed = jnp.dot(cat, wbuf[...], preferred_element_type=f32)
    xn = _tile_rows(i, x_ref, ctx_ref) + tab_ref[stream, 0:1, :] * mixed
    xo_ref[...] = xn

    h2 = xn * _rms_scale(xn) * tab_ref[stream, 1:2, :] + tab_ref[stream, 2:3, :]
    h2_ref[...] = _pack_bf16_pairs(h2)
    hi = h2.astype(bf16)
    lo = (h2 - hi.astype(f32)).astype(bf16)
    lg_ref[...] = (jnp.dot(hi, wr_ref[0], preferred_element_type=f32)
                   + jnp.dot(lo, wr_ref[0], preferred_element_type=f32)
                   + jnp.dot(hi, wr_ref[1], preferred_element_type=f32))


def _outproj(layer, attn, u, b, c, xs, tab, gout, cw, w_out, wr):
    halo = TM // SUBLANES
    last8 = R // SUBLANES - 1
    row = lambda i: (i, 0)
    prev8 = lambda i: (jnp.maximum(i * halo - 1, 0), 0)
    next8 = lambda i: (jnp.minimum((i + 1) * halo, last8), 0)
    split = len(xs) == 2
    return pl.pallas_call(
        functools.partial(_outproj_kernel, layer, split),
        grid=(NT,),
        in_specs=[pl.BlockSpec((TM, ATTN_W), row),
                  pl.BlockSpec((TM, CONV_W), row),
                  pl.BlockSpec((TM, CONV_W), row),
                  pl.BlockSpec((TM, CONV_W), row),
                  pl.BlockSpec((SUBLANES, CONV_W), prev8),
                  pl.BlockSpec((SUBLANES, CONV_W), prev8),
                  pl.BlockSpec((SUBLANES, CONV_W), next8),
                  pl.BlockSpec((SUBLANES, CONV_W), next8)]
                 + _row_specs(split)
                 + [pl.BlockSpec((2, 3, D), lambda i: (0, 0, 0)),
                    pl.BlockSpec((2, CONV_W), lambda i: (0, 0)),
                    pl.BlockSpec((3, CONV_W), lambda i: (0, 0)),
                    pl.BlockSpec(memory_space=pl.ANY),
                    pl.BlockSpec((2, D, LANES), lambda i: (0, 0, 0))],
        out_specs=[pl.BlockSpec((TM, D), row),
                   pl.BlockSpec((TM, DP), row),
                   pl.BlockSpec((TM, LANES), row)],
        out_shape=[jax.ShapeDtypeStruct((R, D), f32),
                   jax.ShapeDtypeStruct((R, DP), jnp.uint32),
                   jax.ShapeDtypeStruct((R, LANES), f32)],
        scratch_shapes=[pltpu.VMEM((D, D), bf16), pltpu.VMEM((2, WCH, D), f32), pltpu.SemaphoreType.DMA((2,))],
        compiler_params=_cparams(("arbitrary",)),
        name="outproj",
    )(attn, u, b, c, u, c, u, c, *xs, tab, gout, cw, w_out, wr)


def _router_kernel(bias_ref, lt_ref, upper_ref, lower_ref, ones_ref, e_ref, g_ref, d_ref, cnt_ref):
    score = [1.0 / (1.0 + jnp.exp(-lt_ref[e])) for e in range(N_EXPERTS)]
    sel = [score[e] + bias_ref[e] for e in range(N_EXPERTS)]

    def top2_sum(a, b, c, d):
        p, q = jnp.maximum(a, b), jnp.minimum(a, b)
        r, s = jnp.maximum(c, d), jnp.minimum(c, d)
        return jnp.maximum(p, r) + jnp.maximum(jnp.minimum(p, r), jnp.maximum(q, s))

    gscore = [top2_sum(*sel[EPG * g:EPG * (g + 1)]) for g in range(N_GROUPS)]
    best, gidx = gscore[0], jnp.zeros(gscore[0].shape, i32)
    for g in range(1, N_GROUPS):
        take = gscore[g] > best
        best = jnp.where(take, gscore[g], best)
        gidx = jnp.where(take, g, gidx)

    def pick_group(vals):
        out = []
        for j in range(EPG):
            v = vals[j]
            for g in range(1, N_GROUPS):
                v = jnp.where(gidx == g, vals[EPG * g + j], v)
            out.append(v)
        return out

    in_sel = pick_group(sel)
    in_score = pick_group(score)

    def argmax_first(vals, excluded):
        bv, bi = None, None
        for j in range(EPG):
            v = vals[j] if excluded is None else jnp.where(excluded == j, -jnp.inf, vals[j])
            if bv is None:
                bv, bi = v, jnp.zeros(v.shape, i32)
            else:
                take = v > bv
                bv = jnp.where(take, v, bv)
                bi = jnp.where(take, j, bi)
        return bi

    i1 = argmax_first(in_sel, None)
    i2 = argmax_first(in_sel, i1)

    def pick_local(vals, idx):
        v = vals[0]
        for j in range(1, EPG):
            v = jnp.where(idx == j, vals[j], v)
        return v

    s1, s2 = pick_local(in_score, i1), pick_local(in_score, i2)
    tot = s1 + s2
    e1 = gidx * EPG + i1
    e2 = gidx * EPG + i2
    e_ref[0], e_ref[1] = e1, e2
    g_ref[0], g_ref[1] = s1 / tot, s2 / tot

    tok = (lax.broadcasted_iota(i32, (RT, LANES), 0) * LANES + lax.broadcasted_iota(i32, (RT, LANES), 1))
    valid = tok < R
    onehot = [(((e1 == e) | (e2 == e)) & valid).astype(f32) for e in range(N_EXPERTS)]
    stack = jnp.concatenate(onehot, axis=0).astype(bf16)
    within = jnp.dot(stack, upper_ref[...], preferred_element_type=f32)
    rowtot = jnp.dot(stack, ones_ref[...], preferred_element_type=f32)
    d1 = jnp.zeros((RT, LANES), i32)
    d2 = jnp.zeros((RT, LANES), i32)
    seg_start = jnp.zeros((1, LANES), i32)
    for e in range(N_EXPERTS):
        rt_e = rowtot[e * RT:(e + 1) * RT]
        before = jnp.dot(lower_ref[...], rt_e.astype(bf16), preferred_element_type=f32)
        rank_e = within[e * RT:(e + 1) * RT] + before
        count_e = (before + rt_e)[RT - 1:RT, :]
        cnt_ref[e:e + 1, :] = count_e
        slot = seg_start + rank_e.astype(i32)
        d1 = jnp.where(e1 == e, slot, d1)
        d2 = jnp.where(e2 == e, slot, d2)
        blocks_e = jnp.right_shift(count_e.astype(i32) + (EBLK - 1), EBLK_SHIFT)
        seg_start = seg_start + jnp.left_shift(blocks_e, EBLK_SHIFT)
    d_ref[0], d_ref[1] = d1, d2


def _router(bias, logits_t, upper, lower, ones):
    full = lambda *shape: pl.BlockSpec(shape, lambda: (0,) * len(shape))
    return pl.pallas_call(
        _router_kernel,
        in_specs=[pl.BlockSpec(memory_space=pltpu.SMEM),
                  full(N_EXPERTS, RT, LANES), full(LANES, LANES), full(RT, RT), full(LANES, LANES)],
        out_specs=[full(2, RT, LANES), full(2, RT, LANES), full(2, RT, LANES), full(N_EXPERTS, LANES)],
        out_shape=[jax.ShapeDtypeStruct((2, RT, LANES), i32),
                   jax.ShapeDtypeStruct((2, RT, LANES), f32),
                   jax.ShapeDtypeStruct((2, RT, LANES), i32),
                   jax.ShapeDtypeStruct((N_EXPERTS, LANES), f32)],
        compiler_params=pltpu.CompilerParams(vmem_limit_bytes=VMEM_LIMIT),
        name="router",
    )(bias, logits_t, upper, lower, ones)


ROW_UNROLL = TM


HSLOTS = 3


def _dispatch_kernel(pend_ref, dest_ref, h_hbm, xs_hbm, hbuf, zbuf, lsem, rsem, zsem):
    i = pl.program_id(0)
    last = pl.num_programs(0) - 1
    slot = lax.rem(i, HSLOTS)
    prev_slot = lax.rem(i + (HSLOTS - 1), HSLOTS)
    next_slot = lax.rem(i + 1, HSLOTS)

    def load(tile, s):
        rows = pl.ds(pl.multiple_of(tile * TM, TM), TM)
        return pltpu.make_async_copy(h_hbm.at[rows], hbuf.at[s], lsem.at[s])

    @pl.when(i == 0)
    def _():
        load(0, 0).start()
        zbuf[...] = jnp.zeros_like(zbuf)

        def zero_copy(e):
            first = pl.multiple_of(pend_ref[e] - EBLK, EBLK)
            return pltpu.make_async_copy(zbuf, xs_hbm.at[pl.ds(first, EBLK)], zsem)

        def nonempty(e):
            return pend_ref[e] > (pend_ref[e - 1] if e else 0)

        for e in range(N_EXPERTS):
            pl.when(nonempty(e))(lambda e=e: zero_copy(e).start())
        for e in range(N_EXPERTS):
            pl.when(nonempty(e))(lambda e=e: zero_copy(e).wait())

        def tail_copy(j):
            return pltpu.make_async_copy(zbuf, xs_hbm.at[pl.ds(pl.multiple_of(j * EBLK, EBLK), EBLK)], zsem)

        first_unused = jnp.right_shift(pend_ref[N_EXPERTS - 1], EBLK_SHIFT)
        lax.fori_loop(first_unused, NBLK, lambda j, c: (tail_copy(j).start(), c)[1], 0)
        lax.fori_loop(first_unused, NBLK, lambda j, c: (tail_copy(j).wait(), c)[1], 0)

    @pl.when(i < last)
    def _():
        load(i + 1, next_slot).start()

    load(i, slot).wait()

    def start(t, c):
        for k in range(2):
            pltpu.make_async_copy(hbuf.at[slot, pl.ds(t, 1)], xs_hbm.at[pl.ds(dest_ref[0, 0, k * TM + t], 1)],
                                  rsem.at[slot]).start(priority=k)
        return c

    def drain(s):
        def wait(t, c):
            for k in range(2):
                pltpu.make_async_copy(hbuf.at[s, pl.ds(0, 1)], xs_hbm.at[pl.ds(0, 1)], rsem.at[s]).wait()
            return c
        lax.fori_loop(0, TM, wait, 0, unroll=ROW_UNROLL)

    lax.fori_loop(0, TM, start, 0, unroll=ROW_UNROLL)
    pl.when(i > 0)(lambda: drain(prev_slot))
    pl.when(i == last)(lambda: drain(slot))


def _dispatch(pend, dest3, h2):
    return pl.pallas_call(
        _dispatch_kernel,
        grid_spec=pltpu.PrefetchScalarGridSpec(
            num_scalar_prefetch=1,
            grid=(NT,),
            in_specs=[pl.BlockSpec((1, 1, 2 * TM), lambda i, pend: (i, 0, 0), memory_space=pltpu.SMEM),
                      pl.BlockSpec(memory_space=pl.ANY)],
            out_specs=pl.BlockSpec(memory_space=pl.ANY),
            scratch_shapes=[pltpu.VMEM((HSLOTS, TM, DP), jnp.uint32), pltpu.VMEM((EBLK, DP), jnp.uint32),
                            pltpu.SemaphoreType.DMA((HSLOTS,)), pltpu.SemaphoreType.DMA((HSLOTS,)),
                            pltpu.SemaphoreType.DMA(())]),
        out_shape=jax.ShapeDtypeStruct((NROWS, DP), jnp.uint32),
        compiler_params=_cparams(("arbitrary",)),
        name="dispatch",
    )(pend, dest3, h2)


NCH = 4
FCH = D_FF // NCH
NSTEPS = NBLK + (N_EXPERTS + 1) * NCH
(T_COMP, T_XBLK, T_CONV, T_CEXP, T_CCH, T_CSLOT, T_SLOT, T_ZERO, T_OBLK) = range(9)
T_STRIDE = 16


def _expert_schedule(counts):
    experts = jnp.arange(N_EXPERTS, dtype=i32)
    nb = (counts + EBLK - 1) // EBLK
    bstart = jnp.cumsum(nb) - nb
    nblocks = jnp.sum(nb)
    steps = jnp.maximum(nb, NCH)
    send = NCH + jnp.cumsum(steps)
    total = send[-1]
    s = jnp.arange(NSTEPS, dtype=i32)
    e_s = jnp.minimum(jnp.sum((s[:, None] >= send[None, :]).astype(i32), axis=1), N_EXPERTS - 1)
    onehot = (e_s[:, None] == experts[None, :]).astype(i32)
    pick = lambda table: jnp.sum(onehot * table[None, :], axis=1)
    i_s = s - pick(send - steps)
    nb_s = pick(nb)
    pre = s < NCH
    active = (s >= NCH) & (s < total)
    done = s >= total
    comp = active & (i_s < nb_s)
    xblk = jnp.clip(pick(bstart) + jnp.minimum(i_s, nb_s - 1), 0, nblocks - 1)
    xblk = jnp.where(pre, 0, jnp.where(done, nblocks - 1, xblk))
    has_next = e_s < N_EXPERTS - 1
    conv = pre | (active & (i_s < NCH) & has_next)
    cexp = jnp.where(pre, 0, jnp.minimum(e_s + 1, N_EXPERTS - 1))
    cch = jnp.where(pre, s, jnp.where(has_next & active, jnp.minimum(i_s, NCH - 1), NCH - 1))
    zblk = nblocks + (s - total)
    zero = done & (zblk < NBLK)
    oblk = jnp.where(done, jnp.minimum(zblk, NBLK - 1), xblk)
    cols = [comp, xblk, conv, cexp, cch, cexp % 2, e_s % 2, zero, oblk]
    cols = [col.astype(i32) for col in cols] + [jnp.zeros_like(s)] * (T_STRIDE - 9)
    return jnp.stack(cols, axis=1).reshape(NSTEPS * T_STRIDE)


def _expert_kernel(tab_ref, x_ref, g_ref, u_ref, d_ref, o_ref, wg_s, wu_s, wd_s):
    base = pl.program_id(0) * T_STRIDE

    @pl.when(tab_ref[base + T_CONV] == 1)
    def _():
        slot, c = tab_ref[base + T_CSLOT], tab_ref[base + T_CCH]
        wg_s[slot, c] = g_ref[0, 0].astype(bf16)
        wu_s[slot, c] = u_ref[0, 0].astype(bf16)
        wd_s[slot, c] = d_ref[0, 0].astype(bf16)

    @pl.when(tab_ref[base + T_COMP] == 1)
    def _():
        slot = tab_ref[base + T_SLOT]
        x = _unpack_bf16_pairs(x_ref[...])
        y = None
        for c in range(NCH):
            a = jnp.dot(x, wg_s[slot, c], preferred_element_type=f32)
            b = jnp.dot(x, wu_s[slot, c], preferred_element_type=f32)
            h = (a / (1.0 + jnp.exp(-a)) * b).astype(bf16)
            yc = jnp.dot(h, wd_s[slot, c], preferred_element_type=f32)
            y = yc if y is None else y + yc
        o_ref[...] = y

    @pl.when(tab_ref[base + T_ZERO] == 1)
    def _():
        o_ref[...] = jnp.zeros_like(o_ref)


def _experts(layer, schedule, xs, w_gate, w_up, w_down):
    at = lambda s, tab, col: tab[s * T_STRIDE + col]
    up_spec = pl.BlockSpec((1, 1, D, FCH), lambda s, tab: (layer, at(s, tab, T_CEXP), 0, at(s, tab, T_CCH)))
    return pl.pallas_call(
        _expert_kernel,
        grid_spec=pltpu.PrefetchScalarGridSpec(
            num_scalar_prefetch=1,
            grid=(NSTEPS,),
            in_specs=[pl.BlockSpec((EBLK, DP), lambda s, tab: (at(s, tab, T_XBLK), 0)),
                      up_spec, up_spec,
                      pl.BlockSpec((1, 1, FCH, D),
                                   lambda s, tab: (layer, at(s, tab, T_CEXP), at(s, tab, T_CCH), 0))],
            out_specs=pl.BlockSpec((EBLK, D), lambda s, tab: (at(s, tab, T_OBLK), 0)),
            scratch_shapes=[pltpu.VMEM((2, NCH, D, FCH), bf16), pltpu.VMEM((2, NCH, D, FCH), bf16),
                            pltpu.VMEM((2, NCH, FCH, D), bf16)]),
        out_shape=jax.ShapeDtypeStruct((NROWS, D), f32),
        compiler_params=_cparams(("arbitrary",)),
        name="experts",
    )(schedule, xs, w_gate, w_up, w_down)


def _combine_kernel(first, dest_ref, dnext_ref, ys_hbm, x_ref, gates_ref, gate_ref, o_ref, ybuf, sems):
    i = pl.program_id(0)
    slot = jnp.bitwise_and(i, 1)
    stream = jnp.where(i + first == 0, 0, 1)

    def start_tile(idx_ref, s):
        def body(t, c):
            for k in range(2):
                pltpu.make_async_copy(ys_hbm.at[pl.ds(idx_ref[0, 0, k * TM + t], 1)],
                                      ybuf.at[s, k, pl.ds(t, 1)], sems.at[s]).start(priority=k)
            return c
        lax.fori_loop(0, TM, body, 0, unroll=ROW_UNROLL)

    @pl.when(i == 0)
    def _():
        start_tile(dest_ref, 0)

    @pl.when(i + 1 < pl.num_programs(0))
    def _():
        start_tile(dnext_ref, 1 - slot)

    def wait(t, c):
        for k in range(2):
            pltpu.make_async_copy(ys_hbm.at[pl.ds(0, 1)], ybuf.at[slot, k, pl.ds(0, 1)], sems.at[slot]).wait()
        return c

    lax.fori_loop(0, TM, wait, 0, unroll=ROW_UNROLL)
    y = ybuf[slot, 0] * gates_ref[:, 0:1] + ybuf[slot, 1] * gates_ref[:, 1:2]
    o_ref[...] = x_ref[...] + gate_ref[stream] * y


def _combine(first, dest3, ys, x, gates, gate2):
    row = lambda i: (i + first, 0)
    return pl.pallas_call(
        functools.partial(_combine_kernel, first),
        grid=(NT - first,),
        in_specs=[pl.BlockSpec((1, 1, 2 * TM), lambda i: (i + first, 0, 0), memory_space=pltpu.SMEM),
                  pl.BlockSpec((1, 1, 2 * TM), lambda i: (jnp.minimum(i + first + 1, NT - 1), 0, 0),
                               memory_space=pltpu.SMEM),
                  pl.BlockSpec(memory_space=pl.ANY),
                  pl.BlockSpec((TM, D), row),
                  pl.BlockSpec((TM, 2), row),
                  pl.BlockSpec((2, 1, D), lambda i: (0, 0, 0))],
        out_specs=pl.BlockSpec((TM, D), lambda i: (i, 0)),
        out_shape=jax.ShapeDtypeStruct((R - first * TM, D), f32),
        scratch_shapes=[pltpu.VMEM((2, 2, TM, D), f32), pltpu.SemaphoreType.DMA((2,))],
        compiler_params=_cparams(("arbitrary",)),
        name="combine",
    )(dest3, dest3, ys, x, gates, gate2)


def _rope_tables():
    half = HEAD_DIM // 2
    inv_freq = (ROPE_THETA ** (-np.arange(0, half, 2, dtype=np.float32) / half)).astype(np.float32)
    t = np.arange(SEQ)
    pos = np.stack([(t // GRID_W).astype(np.float32), (t % GRID_W).astype(np.float32)], axis=1)
    d = np.arange(LANES) % HEAD_DIM
    axis, sub = d // half, d % half
    ang = (pos[:, axis] * inv_freq[sub % (half // 2)][None, :]).astype(np.float32)
    cos, sin = np.cos(ang).astype(np.float32), np.sin(ang).astype(np.float32)
    first = (sub < half // 2)[None, :]
    sa = np.where(first, -sin, 0.0).astype(np.float32)
    sb = np.where(first, 0.0, sin).astype(np.float32)
    ident = np.ones((CTX, LANES), np.float32)
    zeros = np.zeros((CTX, LANES), np.float32)
    return tuple(jnp.asarray(np.concatenate(parts)) for parts in ((ident, cos), (zeros, sa), (zeros, sb)))


def kernel(x, c, ctx, c_ctx, w_ada, b_ada, g_attn, w_in, q_norm_g, k_norm_g, sink, conv_w, g_out_attn, g_out_conv, w_out, g_ffn, w_router, router_bias, w_exp_gate, w_exp_up, w_exp_down):
    assert x.shape == (1, SEQ, D) and ctx.shape == (1, CTX, D)
    xs_in = (x[0], ctx[0])

    silu = lambda t: t * jax.nn.sigmoid(t)
    s_vec = jnp.stack([silu(c_ctx), silu(c[0])])
    mods = _adaln(jnp.broadcast_to(s_vec[:, :, None], (2, D, LANES)), w_ada, b_ada).reshape(DEPTH, 2, 6, D)

    cos, sa, sb = _rope_tables()
    lane = jnp.arange(LANES)
    ones_bd = (lane[:, None] // HEAD_DIM == lane[None, :] // HEAD_DIM).astype(bf16)
    upper = (lane[:, None] < lane[None, :]).astype(bf16)
    ones = jnp.ones((LANES, LANES), bf16)
    rt = jnp.arange(RT)
    lower = (rt[None, :] < rt[:, None]).astype(bf16)
    wr_pad = jnp.pad(w_router, ((0, 0), (0, LANES - N_EXPERTS)))
    wr_hi = wr_pad.astype(bf16)
    wr = jnp.stack([wr_hi, (wr_pad - wr_hi.astype(f32)).astype(bf16)])

    for l in range(DEPTH):
        mod = mods[l]
        ms1 = jnp.stack([g_attn[l] * (1.0 + mod[:, 1]), mod[:, 0]], axis=1)
        q, kv, u, b_gate, c_gate = _inproj(l, xs_in, ms1, w_in)
        g2 = jnp.stack([jnp.tile(q_norm_g[l] * HEAD_DIM ** -0.5, 2), jnp.tile(k_norm_g[l], 2)])
        q, k, v = _qkprep(q, kv, cos, sa, sb, g2, ones_bd)
        attn = _attention(sink[l], q, k, v)
        tab = jnp.stack([mod[:, 2], g_ffn[l] * (1.0 + mod[:, 4]), mod[:, 3]], axis=1)
        gout = jnp.stack([g_out_attn[l], g_out_conv[l]])
        xa, h2, logits = _outproj(l, attn, u, b_gate, c_gate, xs_in, tab, gout, conv_w[l], w_out, wr)

        lt = jnp.pad(logits[:, :N_EXPERTS].T, ((0, 0), (0, RT * LANES - R))).reshape(N_EXPERTS, RT, LANES)
        _, g_t, d_t, cnt = _router(router_bias, lt, upper, lower, ones)
        gates = g_t.reshape(2, RT * LANES)[:, :R].T
        dest = d_t.reshape(2, RT * LANES)[:, :R]
        dest3 = dest.reshape(2, NT, TM).transpose(1, 0, 2).reshape(NT, 1, 2 * TM)
        counts = cnt[:, 0].astype(i32)
        pend = jnp.cumsum((counts + EBLK - 1) // EBLK * EBLK)

        xs = _dispatch(pend, dest3, h2)
        ys = _experts(l, _expert_schedule(counts), xs, w_exp_gate, w_exp_up, w_exp_down)
        xa = _combine(1 if l == DEPTH - 1 else 0, dest3, ys, xa, gates, mod[:, 5][:, None, :])
        xs_in = (xa,)

    return xa[None]
```

```python
import functools

import jax
import numpy as np
import jax.numpy as jnp
from jax import lax
from jax.experimental import pallas as pl
from jax.experimental.pallas import tpu as pltpu

f32 = jnp.float32
bf16 = jnp.bfloat16
i32 = jnp.int32

D = 2048
DEPTH = 4
SEQ = 8192
CTX = 256
R = CTX + SEQ
GRID_W = 64
ATTN_W = 1024
CONV_W = 1024
HEAD_DIM = 64
N_HEADS = 16
KV_W = 128
IN_COLS = ATTN_W + 2 * KV_W + 3 * CONV_W
ABLK = 128
WINDOW = 128
N_EXPERTS = 16
N_GROUPS = 4
EPG = 4
D_FF = 1024
EPS = 1e-6
NEG_INF = -1e30
ROPE_THETA = 10000.0

LANES = 128
SUBLANES = 8
TM = 256
NT = R // TM
EBLK = 256
EBLK_SHIFT = EBLK.bit_length() - 1
assert 1 << EBLK_SHIFT == EBLK
NBLK = -(-(2 * R + N_EXPERTS * (EBLK - 1)) // EBLK)
NROWS = NBLK * EBLK
RT = LANES
VMEM_LIMIT = 56 * 1024 * 1024


def _cparams(sem):
    return pltpu.CompilerParams(dimension_semantics=sem, vmem_limit_bytes=VMEM_LIMIT)


ADA_TK = 2048
ADA_TN = 1536


def _adaln_kernel(s_ref, w_ref, b_ref, o_ref, acc_ref):
    k = pl.program_id(2)

    @pl.when(k == 0)
    def _():
        acc_ref[...] = jnp.zeros_like(acc_ref)

    for j in range(ADA_TN // LANES):
        w = w_ref[0, :, j * LANES:(j + 1) * LANES]
        for v in range(2):
            p = (w * s_ref[v]).reshape(ADA_TK // SUBLANES, SUBLANES, LANES).sum(axis=0)
            acc_ref[v, :, j * LANES:(j + 1) * LANES] += p

    @pl.when(k == pl.num_programs(2) - 1)
    def _():
        for v in range(2):
            o_ref[0, v:v + 1, :] = acc_ref[v].sum(axis=0, keepdims=True) + b_ref[0]


def _adaln(s_bcast, w_ada, b_ada):
    return pl.pallas_call(
        _adaln_kernel,
        grid=(DEPTH, 6 * D // ADA_TN, D // ADA_TK),
        in_specs=[pl.BlockSpec((2, ADA_TK, LANES), lambda l, n, k: (0, k, 0)),
                  pl.BlockSpec((1, ADA_TK, ADA_TN), lambda l, n, k: (l, k, n)),
                  pl.BlockSpec((1, 1, ADA_TN), lambda l, n, k: (l, 0, n))],
        out_specs=pl.BlockSpec((1, 2, ADA_TN), lambda l, n, k: (l, 0, n)),
        out_shape=jax.ShapeDtypeStruct((DEPTH, 2, 6 * D), f32),
        scratch_shapes=[pltpu.VMEM((2, SUBLANES, ADA_TN), f32)],
        compiler_params=_cparams(("arbitrary", "arbitrary", "arbitrary")),
        name="adaln",
    )(s_bcast, w_ada, b_ada.reshape(DEPTH, 1, 6 * D))


def _rms_scale(x):
    return lax.rsqrt(jnp.mean(x * x, axis=-1, keepdims=True) + EPS)


WCH = 256


def _load_weight_bf16(w_hbm, wbuf, stage, sems):
    n = wbuf.shape[0] // WCH

    def chunk(c):
        return pltpu.make_async_copy(w_hbm.at[pl.ds(c * WCH, WCH)], stage.at[c % 2], sems.at[c % 2])

    chunk(0).start()
    for c in range(n):
        if c + 1 < n:
            chunk(c + 1).start()
        chunk(c).wait()
        wbuf[pl.ds(c * WCH, WCH), :] = stage[c % 2].astype(bf16)


Q0, K0, V0, U0, B0, C0 = 0, ATTN_W, ATTN_W + KV_W, ATTN_W + 2 * KV_W, ATTN_W + 2 * KV_W + CONV_W, \
    ATTN_W + 2 * KV_W + 2 * CONV_W


def _tile_rows(i, x_ref, ctx_ref):
    if ctx_ref is None:
        return x_ref[...]
    return jnp.where(i == 0, ctx_ref[...], x_ref[...])


def _row_specs(split):
    if split:
        return [pl.BlockSpec((TM, D), lambda i: (jnp.maximum(i - 1, 0), 0)), pl.BlockSpec((TM, D), lambda i: (0, 0))]
    return [pl.BlockSpec((TM, D), lambda i: (i, 0))]


def _inproj_kernel(layer, split, x_ref, *refs):
    ctx_ref, refs = (refs[0], refs[1:]) if split else (None, refs)
    (ms_ref, cos_ref, sa_ref, sb_ref, g_ref, ones_ref, w_hbm,
     q_ref, k_ref, v_ref, u_ref, b_ref, c_ref, wbuf, stage, sems) = refs

    @pl.when(pl.program_id(0) == 0)
    def _():
        _load_weight_bf16(w_hbm.at[layer], wbuf, stage, sems)

    stream = jnp.where(pl.program_id(0) == 0, 0, 1)
    x = _tile_rows(pl.program_id(0), x_ref, ctx_ref)
    gain = ms_ref[stream, 0:1, :]
    shift = ms_ref[stream, 1:2, :]
    h = (x * _rms_scale(x) * gain + shift).astype(bf16)
    p = jnp.dot(h, wbuf[...], preferred_element_type=f32)
    u_ref[...] = p[:, U0:B0]
    b_ref[...] = p[:, B0:C0]
    c_ref[...] = p[:, C0:]

    cos, sa, sb = cos_ref[...], sa_ref[...], sb_ref[...]
    ones = ones_ref[...]

    def norm_rope(xc, g):
        sq = xc * xc
        hi = sq.astype(bf16)
        lo = (sq - hi.astype(f32)).astype(bf16)
        ss = (jnp.dot(hi, ones, preferred_element_type=f32)
              + jnp.dot(lo, ones, preferred_element_type=f32))
        y = xc * lax.rsqrt(ss * (1.0 / HEAD_DIM) + EPS) * g
        return y * cos + pltpu.roll(y, LANES - 16, 1) * sa + pltpu.roll(y, 16, 1) * sb

    for j in range(ATTN_W // LANES):
        sl = slice(j * LANES, (j + 1) * LANES)
        q_ref[:, sl] = norm_rope(p[:, sl], g_ref[0:1, :]).astype(bf16)
    k_ref[...] = norm_rope(p[:, K0:V0], g_ref[1:2, :]).astype(bf16)
    v_ref[...] = p[:, V0:U0].astype(bf16)


def _inproj(layer, xs, ms, w_in, cos, sa, sb, g2, ones_bd):
    row = lambda i: (i, 0)
    outs = ((ATTN_W, bf16), (KV_W, bf16), (KV_W, bf16), (CONV_W, f32), (CONV_W, f32), (CONV_W, f32))
    split = len(xs) == 2
    return pl.pallas_call(
        functools.partial(_inproj_kernel, layer, split),
        grid=(NT,),
        in_specs=_row_specs(split) + [pl.BlockSpec((2, 2, D), lambda i: (0, 0, 0)),
                                      pl.BlockSpec((TM, LANES), row),
                                      pl.BlockSpec((TM, LANES), row),
                                      pl.BlockSpec((TM, LANES), row),
                                      pl.BlockSpec((2, LANES), lambda i: (0, 0)),
                                      pl.BlockSpec((LANES, LANES), lambda i: (0, 0)),
                                      pl.BlockSpec(memory_space=pl.ANY)],
        out_specs=[pl.BlockSpec((TM, n), row) for n, _ in outs],
        out_shape=[jax.ShapeDtypeStruct((R, n), dt) for n, dt in outs],
        scratch_shapes=[pltpu.VMEM((D, IN_COLS), bf16), pltpu.VMEM((2, WCH, IN_COLS), f32),
                        pltpu.SemaphoreType.DMA((2,))],
        compiler_params=_cparams(("arbitrary",)),
        name="inproj",
    )(*xs, ms, cos, sa, sb, g2, ones_bd, w_in)


NQB = R // ABLK
FIRST_LAT = CTX // ABLK
NKEYS = CTX + 3 * ABLK


def _attn_kernel(sink_ref, q_ref, kc_ref, kp_ref, kk_ref, kn_ref, vc_ref, vp_ref, vk_ref, vn_ref, o_ref):
    i = pl.program_id(0)
    kf = jnp.concatenate([kc_ref[...], kp_ref[...], kk_ref[...], kn_ref[...]], axis=0).astype(f32)
    vf = jnp.concatenate([vc_ref[...], vp_ref[...], vk_ref[...], vn_ref[...]], axis=0).astype(f32)
    lane = lax.broadcasted_iota(i32, (NKEYS, LANES), 1)
    low = lane < HEAD_DIM
    kr = pltpu.roll(kf, HEAD_DIM, 1)
    vr = pltpu.roll(vf, HEAD_DIM, 1)
    zero = jnp.zeros_like(kf)
    k_lo = (jnp.where(low, kf, zero), jnp.where(low, kr, zero))
    k_hi = (jnp.where(low, zero, kr), jnp.where(low, zero, kf))
    v_lo = (jnp.where(low, vf, zero), jnp.where(low, vr, zero))
    v_hi = (jnp.where(low, zero, vr), jnp.where(low, zero, vf))

    qi = lax.broadcasted_iota(i32, (ABLK, NKEYS), 0)
    col = lax.broadcasted_iota(i32, (ABLK, NKEYS), 1)
    kj = col - CTX
    in_window = jnp.abs(qi + ABLK - kj) <= WINDOW
    key_block = i - 1 + jnp.right_shift(kj, 7)
    first_ok = jnp.where(i >= FIRST_LAT, FIRST_LAT, NQB)
    mask = (col < CTX) | (in_window & (key_block >= first_ok) & (key_block < NQB))
    lane_q = lax.broadcasted_iota(i32, (ABLK, LANES), 1)

    contract_last = (((1,), (1,)), ((), ()))
    for h in range(2):
        qs = jnp.concatenate([q_ref[:, (h * 4 + p) * LANES:(h * 4 + p + 1) * LANES] for p in range(4)], axis=0)
        s_par = (lax.dot_general(qs, k_lo[h].astype(bf16), contract_last, preferred_element_type=f32),
                 lax.dot_general(qs, k_hi[h].astype(bf16), contract_last, preferred_element_type=f32))
        probs = ([], [])
        rden = ([], [])
        for par in range(2):
            for p in range(4):
                sink = sink_ref[h * 8 + 2 * p + par]
                s = jnp.where(mask, s_par[par][p * ABLK:(p + 1) * ABLK], NEG_INF)
                m = jnp.maximum(jnp.max(s, axis=-1, keepdims=True), sink)
                e = jnp.exp(s - m)
                den = jnp.sum(e, axis=-1, keepdims=True) + jnp.exp(sink - m)
                probs[par].append(e.astype(bf16))
                rden[par].append(1.0 / den)
        o = (jnp.dot(jnp.concatenate(probs[0], axis=0), v_lo[h].astype(bf16), preferred_element_type=f32)
             + jnp.dot(jnp.concatenate(probs[1], axis=0), v_hi[h].astype(bf16), preferred_element_type=f32))
        for p in range(4):
            scale = jnp.where(lane_q < HEAD_DIM, rden[0][p], rden[1][p])
            o_ref[:, (h * 4 + p) * LANES:(h * 4 + p + 1) * LANES] = o[p * ABLK:(p + 1) * ABLK] * scale


def _attention(sink, q, k, v):
    clip = lambda b: jnp.clip(b, FIRST_LAT, NQB - 1)
    kv_specs = [pl.BlockSpec((CTX, KV_W), lambda i: (0, 0)),
                pl.BlockSpec((ABLK, KV_W), lambda i: (clip(i - 1), 0)),
                pl.BlockSpec((ABLK, KV_W), lambda i: (i, 0)),
                pl.BlockSpec((ABLK, KV_W), lambda i: (clip(i + 1), 0))]
    return pl.pallas_call(
        _attn_kernel,
        grid=(NQB,),
        in_specs=[pl.BlockSpec(memory_space=pltpu.SMEM),
                  pl.BlockSpec((ABLK, ATTN_W), lambda i: (i, 0))] + kv_specs + kv_specs,
        out_specs=pl.BlockSpec((ABLK, ATTN_W), lambda i: (i, 0)),
        out_shape=jax.ShapeDtypeStruct((R, ATTN_W), f32),
        compiler_params=_cparams(("arbitrary",)),
        name="attention",
    )(sink, q, k, k, k, k, v, v, v, v)


DP = D // 2
HIGH_HALF = 0xFFFF0000


def _pack_bf16_pairs(t):
    lo = lax.bitcast_convert_type(t[:, :DP].astype(bf16).astype(f32), jnp.uint32)
    hi = lax.bitcast_convert_type(t[:, DP:].astype(bf16).astype(f32), jnp.uint32)
    return jnp.right_shift(lo, jnp.uint32(16)) | (hi & jnp.uint32(HIGH_HALF))


def _unpack_bf16_pairs(w):
    lo = lax.bitcast_convert_type(jnp.left_shift(w, jnp.uint32(16)), f32)
    hi = lax.bitcast_convert_type(w & jnp.uint32(HIGH_HALF), f32)
    return jnp.concatenate([lo.astype(bf16), hi.astype(bf16)], axis=1)


def _outproj_kernel(layer, split, attn_ref, u_ref, b_ref, c_ref, up_ref, cp_ref, un_ref, cn_ref, x_ref, *refs):
    ctx_ref, refs = (refs[0], refs[1:]) if split else (None, refs)
    tab_ref, gout_ref, cw_ref, w_hbm, wr_ref, xo_ref, h2_ref, lg_ref, wbuf, stage, sems = refs
    i = pl.program_id(0)

    @pl.when(i == 0)
    def _():
        _load_weight_bf16(w_hbm.at[layer], wbuf, stage, sems)

    stream = jnp.where(i == 0, 0, 1)
    prev_ok = (i >= 2).astype(f32)
    next_ok = ((i >= 1) & (i < pl.num_programs(0) - 1)).astype(f32)
    w = c_ref[...] * u_ref[...]
    w_before = cp_ref[SUBLANES - 1:SUBLANES, :] * up_ref[SUBLANES - 1:SUBLANES, :] * prev_ok
    w_after = cn_ref[0:1, :] * un_ref[0:1, :] * next_ok
    rows = lax.broadcasted_iota(i32, (TM, 1), 0)
    w_prev = jnp.where(rows == 0, w_before, pltpu.roll(w, 1, 0))
    w_next = jnp.where(rows == TM - 1, w_after, pltpu.roll(w, TM - 1, 0))
    conv = b_ref[...] * (cw_ref[0:1, :] * w_prev + cw_ref[1:2, :] * w + cw_ref[2:3, :] * w_next)

    attn = attn_ref[...]
    a_n = attn * _rms_scale(attn) * gout_ref[0:1, :]
    c_n = conv * _rms_scale(conv) * gout_ref[1:2, :]
    cat = jnp.concatenate([a_n, c_n], axis=1).astype(bf16)
    mixed = jnp.dot(cat, wbuf[...], preferred_element_type=f32)
    xn = _tile_rows(i, x_ref, ctx_ref) + tab_ref[stream, 0:1, :] * mixed
    xo_ref[...] = xn

    h2 = xn * _rms_scale(xn) * tab_ref[stream, 1:2, :] + tab_ref[stream, 2:3, :]
    h2_ref[...] = _pack_bf16_pairs(h2)
    hi = h2.astype(bf16)
    lo = (h2 - hi.astype(f32)).astype(bf16)
    lg_ref[...] = (jnp.dot(hi, wr_ref[0], preferred_element_type=f32)
                   + jnp.dot(lo, wr_ref[0], preferred_element_type=f32)
                   + jnp.dot(hi, wr_ref[1], preferred_element_type=f32))


def _outproj(layer, attn, u, b, c, xs, tab, gout, cw, w_out, wr):
    halo = TM // SUBLANES
    last8 = R // SUBLANES - 1
    row = lambda i: (i, 0)
    prev8 = lambda i: (jnp.maximum(i * halo - 1, 0), 0)
    next8 = lambda i: (jnp.minimum((i + 1) * halo, last8), 0)
    split = len(xs) == 2
    return pl.pallas_call(
        functools.partial(_outproj_kernel, layer, split),
        grid=(NT,),
        in_specs=[pl.BlockSpec((TM, ATTN_W), row),
                  pl.BlockSpec((TM, CONV_W), row),
                  pl.BlockSpec((TM, CONV_W), row),
                  pl.BlockSpec((TM, CONV_W), row),
                  pl.BlockSpec((SUBLANES, CONV_W), prev8),
                  pl.BlockSpec((SUBLANES, CONV_W), prev8),
                  pl.BlockSpec((SUBLANES, CONV_W), next8),
                  pl.BlockSpec((SUBLANES, CONV_W), next8)]
                 + _row_specs(split)
                 + [pl.BlockSpec((2, 3, D), lambda i: (0, 0, 0)),
                    pl.BlockSpec((2, CONV_W), lambda i: (0, 0)),
                    pl.BlockSpec((3, CONV_W), lambda i: (0, 0)),
                    pl.BlockSpec(memory_space=pl.ANY),
                    pl.BlockSpec((2, D, LANES), lambda i: (0, 0, 0))],
        out_specs=[pl.BlockSpec((TM, D), row),
                   pl.BlockSpec((TM, DP), row),
                   pl.BlockSpec((TM, LANES), row)],
        out_shape=[jax.ShapeDtypeStruct((R, D), f32),
                   jax.ShapeDtypeStruct((R, DP), jnp.uint32),
                   jax.ShapeDtypeStruct((R, LANES), f32)],
        scratch_shapes=[pltpu.VMEM((D, D), bf16), pltpu.VMEM((2, WCH, D), f32), pltpu.SemaphoreType.DMA((2,))],
        compiler_params=_cparams(("arbitrary",)),
        name="outproj",
    )(attn, u, b, c, u, c, u, c, *xs, tab, gout, cw, w_out, wr)


def _router_kernel(bias_ref, lt_ref, upper_ref, lower_ref, ones_ref, e_ref, g_ref, d_ref, cnt_ref):
    score = [1.0 / (1.0 + jnp.exp(-lt_ref[e])) for e in range(N_EXPERTS)]
    sel = [score[e] + bias_ref[e] for e in range(N_EXPERTS)]

    def top2_sum(a, b, c, d):
        p, q = jnp.maximum(a, b), jnp.minimum(a, b)
        r, s = jnp.maximum(c, d), jnp.minimum(c, d)
        return jnp.maximum(p, r) + jnp.maximum(jnp.minimum(p, r), jnp.maximum(q, s))

    gscore = [top2_sum(*sel[EPG * g:EPG * (g + 1)]) for g in range(N_GROUPS)]
    best, gidx = gscore[0], jnp.zeros(gscore[0].shape, i32)
    for g in range(1, N_GROUPS):
        take = gscore[g] > best
        best = jnp.where(take, gscore[g], best)
        gidx = jnp.where(take, g, gidx)

    def pick_group(vals):
        out = []
        for j in range(EPG):
            v = vals[j]
            for g in range(1, N_GROUPS):
                v = jnp.where(gidx == g, vals[EPG * g + j], v)
            out.append(v)
        return out

    in_sel = pick_group(sel)
    in_score = pick_group(score)

    def argmax_first(vals, excluded):
        bv, bi = None, None
        for j in range(EPG):
            v = vals[j] if excluded is None else jnp.where(excluded == j, -jnp.inf, vals[j])
            if bv is None:
                bv, bi = v, jnp.zeros(v.shape, i32)
            else:
                take = v > bv
                bv = jnp.where(take, v, bv)
                bi = jnp.where(take, j, bi)
        return bi

    i1 = argmax_first(in_sel, None)
    i2 = argmax_first(in_sel, i1)

    def pick_local(vals, idx):
        v = vals[0]
        for j in range(1, EPG):
            v = jnp.where(idx == j, vals[j], v)
        return v

    s1, s2 = pick_local(in_score, i1), pick_local(in_score, i2)
    tot = s1 + s2
    e1 = gidx * EPG + i1
    e2 = gidx * EPG + i2
    e_ref[0], e_ref[1] = e1, e2
    g_ref[0], g_ref[1] = s1 / tot, s2 / tot

    tok = (lax.broadcasted_iota(i32, (RT, LANES), 0) * LANES + lax.broadcasted_iota(i32, (RT, LANES), 1))
    valid = tok < R
    onehot = [(((e1 == e) | (e2 == e)) & valid).astype(f32) for e in range(N_EXPERTS)]
    stack = jnp.concatenate(onehot, axis=0).astype(bf16)
    within = jnp.dot(stack, upper_ref[...], preferred_element_type=f32)
    rowtot = jnp.dot(stack, ones_ref[...], preferred_element_type=f32)
    d1 = jnp.zeros((RT, LANES), i32)
    d2 = jnp.zeros((RT, LANES), i32)
    seg_start = jnp.zeros((1, LANES), i32)
    for e in range(N_EXPERTS):
        rt_e = rowtot[e * RT:(e + 1) * RT]
        before = jnp.dot(lower_ref[...], rt_e.astype(bf16), preferred_element_type=f32)
        rank_e = within[e * RT:(e + 1) * RT] + before
        count_e = (before + rt_e)[RT - 1:RT, :]
        cnt_ref[e:e + 1, :] = count_e
        slot = seg_start + rank_e.astype(i32)
        d1 = jnp.where(e1 == e, slot, d1)
        d2 = jnp.where(e2 == e, slot, d2)
        blocks_e = jnp.right_shift(count_e.astype(i32) + (EBLK - 1), EBLK_SHIFT)
        seg_start = seg_start + jnp.left_shift(blocks_e, EBLK_SHIFT)
    d_ref[0], d_ref[1] = d1, d2


def _router(bias, logits_t, upper, lower, ones):
    full = lambda *shape: pl.BlockSpec(shape, lambda: (0,) * len(shape))
    return pl.pallas_call(
        _router_kernel,
        in_specs=[pl.BlockSpec(memory_space=pltpu.SMEM),
                  full(N_EXPERTS, RT, LANES), full(LANES, LANES), full(RT, RT), full(LANES, LANES)],
        out_specs=[full(2, RT, LANES), full(2, RT, LANES), full(2, RT, LANES), full(N_EXPERTS, LANES)],
        out_shape=[jax.ShapeDtypeStruct((2, RT, LANES), i32),
                   jax.ShapeDtypeStruct((2, RT, LANES), f32),
                   jax.ShapeDtypeStruct((2, RT, LANES), i32),
                   jax.ShapeDtypeStruct((N_EXPERTS, LANES), f32)],
        compiler_params=pltpu.CompilerParams(vmem_limit_bytes=VMEM_LIMIT),
        name="router",
    )(bias, logits_t, upper, lower, ones)


ROW_UNROLL = TM


HSLOTS = 3


def _dispatch_kernel(pend_ref, dest_ref, h_hbm, xs_hbm, hbuf, zbuf, lsem, rsem, zsem):
    i = pl.program_id(0)
    last = pl.num_programs(0) - 1
    slot = lax.rem(i, HSLOTS)
    prev_slot = lax.rem(i + (HSLOTS - 1), HSLOTS)
    next_slot = lax.rem(i + 1, HSLOTS)

    def load(tile, s):
        rows = pl.ds(pl.multiple_of(tile * TM, TM), TM)
        return pltpu.make_async_copy(h_hbm.at[rows], hbuf.at[s], lsem.at[s])

    @pl.when(i == 0)
    def _():
        load(0, 0).start()
        zbuf[...] = jnp.zeros_like(zbuf)

        def zero_copy(e):
            first = pl.multiple_of(pend_ref[e] - EBLK, EBLK)
            return pltpu.make_async_copy(zbuf, xs_hbm.at[pl.ds(first, EBLK)], zsem)

        def nonempty(e):
            return pend_ref[e] > (pend_ref[e - 1] if e else 0)

        for e in range(N_EXPERTS):
            pl.when(nonempty(e))(lambda e=e: zero_copy(e).start())
        for e in range(N_EXPERTS):
            pl.when(nonempty(e))(lambda e=e: zero_copy(e).wait())

        def tail_copy(j):
            return pltpu.make_async_copy(zbuf, xs_hbm.at[pl.ds(pl.multiple_of(j * EBLK, EBLK), EBLK)], zsem)

        first_unused = jnp.right_shift(pend_ref[N_EXPERTS - 1], EBLK_SHIFT)
        lax.fori_loop(first_unused, NBLK, lambda j, c: (tail_copy(j).start(), c)[1], 0)
        lax.fori_loop(first_unused, NBLK, lambda j, c: (tail_copy(j).wait(), c)[1], 0)

    @pl.when(i < last)
    def _():
        load(i + 1, next_slot).start()

    load(i, slot).wait()

    def start(t, c):
        for k in range(2):
            pltpu.make_async_copy(hbuf.at[slot, pl.ds(t, 1)], xs_hbm.at[pl.ds(dest_ref[0, 0, k * TM + t], 1)],
                                  rsem.at[slot]).start(priority=k)
        return c

    def drain(s):
        def wait(t, c):
            for k in range(2):
                pltpu.make_async_copy(hbuf.at[s, pl.ds(0, 1)], xs_hbm.at[pl.ds(0, 1)], rsem.at[s]).wait()
            return c
        lax.fori_loop(0, TM, wait, 0, unroll=ROW_UNROLL)

    lax.fori_loop(0, TM, start, 0, unroll=ROW_UNROLL)
    pl.when(i > 0)(lambda: drain(prev_slot))
    pl.when(i == last)(lambda: drain(slot))


def _dispatch(pend, dest3, h2):
    return pl.pallas_call(
        _dispatch_kernel,
        grid_spec=pltpu.PrefetchScalarGridSpec(
            num_scalar_prefetch=1,
            grid=(NT,),
            in_specs=[pl.BlockSpec((1, 1, 2 * TM), lambda i, pend: (i, 0, 0), memory_space=pltpu.SMEM),
                      pl.BlockSpec(memory_space=pl.ANY)],
            out_specs=pl.BlockSpec(memory_space=pl.ANY),
            scratch_shapes=[pltpu.VMEM((HSLOTS, TM, DP), jnp.uint32), pltpu.VMEM((EBLK, DP), jnp.uint32),
                            pltpu.SemaphoreType.DMA((HSLOTS,)), pltpu.SemaphoreType.DMA((HSLOTS,)),
                            pltpu.SemaphoreType.DMA(())]),
        out_shape=jax.ShapeDtypeStruct((NROWS, DP), jnp.uint32),
        compiler_params=_cparams(("arbitrary",)),
        name="dispatch",
    )(pend, dest3, h2)


NCH = 4
FCH = D_FF // NCH
NSTEPS = NBLK + (N_EXPERTS + 1) * NCH
(T_COMP, T_XBLK, T_CONV, T_CEXP, T_CCH, T_CSLOT, T_SLOT, T_ZERO, T_OBLK) = range(9)
T_STRIDE = 16


def _expert_schedule(counts):
    experts = jnp.arange(N_EXPERTS, dtype=i32)
    nb = (counts + EBLK - 1) // EBLK
    bstart = jnp.cumsum(nb) - nb
    nblocks = jnp.sum(nb)
    steps = jnp.maximum(nb, NCH)
    send = NCH + jnp.cumsum(steps)
    total = send[-1]
    s = jnp.arange(NSTEPS, dtype=i32)
    e_s = jnp.minimum(jnp.sum((s[:, None] >= send[None, :]).astype(i32), axis=1), N_EXPERTS - 1)
    onehot = (e_s[:, None] == experts[None, :]).astype(i32)
    pick = lambda table: jnp.sum(onehot * table[None, :], axis=1)
    i_s = s - pick(send - steps)
    nb_s = pick(nb)
    pre = s < NCH
    active = (s >= NCH) & (s < total)
    done = s >= total
    comp = active & (i_s < nb_s)
    xblk = jnp.clip(pick(bstart) + jnp.minimum(i_s, nb_s - 1), 0, nblocks - 1)
    xblk = jnp.where(pre, 0, jnp.where(done, nblocks - 1, xblk))
    has_next = e_s < N_EXPERTS - 1
    conv = pre | (active & (i_s < NCH) & has_next)
    cexp = jnp.where(pre, 0, jnp.minimum(e_s + 1, N_EXPERTS - 1))
    cch = jnp.where(pre, s, jnp.where(has_next & active, jnp.minimum(i_s, NCH - 1), NCH - 1))
    zblk = nblocks + (s - total)
    zero = done & (zblk < NBLK)
    oblk = jnp.where(done, jnp.minimum(zblk, NBLK - 1), xblk)
    cols = [comp, xblk, conv, cexp, cch, cexp % 2, e_s % 2, zero, oblk]
    cols = [col.astype(i32) for col in cols] + [jnp.zeros_like(s)] * (T_STRIDE - 9)
    return jnp.stack(cols, axis=1).reshape(NSTEPS * T_STRIDE)


def _expert_kernel(tab_ref, x_ref, g_ref, u_ref, d_ref, o_ref, wg_s, wu_s, wd_s):
    base = pl.program_id(0) * T_STRIDE

    @pl.when(tab_ref[base + T_CONV] == 1)
    def _():
        slot, c = tab_ref[base + T_CSLOT], tab_ref[base + T_CCH]
        wg_s[slot, c] = g_ref[0, 0].astype(bf16)
        wu_s[slot, c] = u_ref[0, 0].astype(bf16)
        wd_s[slot, c] = d_ref[0, 0].astype(bf16)

    @pl.when(tab_ref[base + T_COMP] == 1)
    def _():
        slot = tab_ref[base + T_SLOT]
        x = _unpack_bf16_pairs(x_ref[...])
        y = None
        for c in range(NCH):
            a = jnp.dot(x, wg_s[slot, c], preferred_element_type=f32)
            b = jnp.dot(x, wu_s[slot, c], preferred_element_type=f32)
            h = (a / (1.0 + jnp.exp(-a)) * b).astype(bf16)
            yc = jnp.dot(h, wd_s[slot, c], preferred_element_type=f32)
            y = yc if y is None else y + yc
        o_ref[...] = y

    @pl.when(tab_ref[base + T_ZERO] == 1)
    def _():
        o_ref[...] = jnp.zeros_like(o_ref)


def _experts(layer, schedule, xs, w_gate, w_up, w_down):
    at = lambda s, tab, col: tab[s * T_STRIDE + col]
    up_spec = pl.BlockSpec((1, 1, D, FCH), lambda s, tab: (layer, at(s, tab, T_CEXP), 0, at(s, tab, T_CCH)))
    return pl.pallas_call(
        _expert_kernel,
        grid_spec=pltpu.PrefetchScalarGridSpec(
            num_scalar_prefetch=1,
            grid=(NSTEPS,),
            in_specs=[pl.BlockSpec((EBLK, DP), lambda s, tab: (at(s, tab, T_XBLK), 0)),
                      up_spec, up_spec,
                      pl.BlockSpec((1, 1, FCH, D),
                                   lambda s, tab: (layer, at(s, tab, T_CEXP), at(s, tab, T_CCH), 0))],
            out_specs=pl.BlockSpec((EBLK, D), lambda s, tab: (at(s, tab, T_OBLK), 0)),
            scratch_shapes=[pltpu.VMEM((2, NCH, D, FCH), bf16), pltpu.VMEM((2, NCH, D, FCH), bf16),
                            pltpu.VMEM((2, NCH, FCH, D), bf16)]),
        out_shape=jax.ShapeDtypeStruct((NROWS, D), f32),
        compiler_params=_cparams(("arbitrary",)),
        name="experts",
    )(schedule, xs, w_gate, w_up, w_down)


def _combine_kernel(first, dest_ref, dnext_ref, ys_hbm, x_ref, gates_ref, gate_ref, o_ref, ybuf, sems):
    i = pl.program_id(0)
    slot = jnp.bitwise_and(i, 1)
    stream = jnp.where(i + first == 0, 0, 1)

    def start_tile(idx_ref, s):
        def body(t, c):
            for k in range(2):
                pltpu.make_async_copy(ys_hbm.at[pl.ds(idx_ref[0, 0, k * TM + t], 1)],
                                      ybuf.at[s, k, pl.ds(t, 1)], sems.at[s]).start(priority=k)
            return c
        lax.fori_loop(0, TM, body, 0, unroll=ROW_UNROLL)

    @pl.when(i == 0)
    def _():
        start_tile(dest_ref, 0)

    @pl.when(i + 1 < pl.num_programs(0))
    def _():
        start_tile(dnext_ref, 1 - slot)

    def wait(t, c):
        for k in range(2):
            pltpu.make_async_copy(ys_hbm.at[pl.ds(0, 1)], ybuf.at[slot, k, pl.ds(0, 1)], sems.at[slot]).wait()
        return c

    lax.fori_loop(0, TM, wait, 0, unroll=ROW_UNROLL)
    y = ybuf[slot, 0] * gates_ref[:, 0:1] + ybuf[slot, 1] * gates_ref[:, 1:2]
    o_ref[...] = x_ref[...] + gate_ref[stream] * y


def _combine(first, dest3, ys, x, gates, gate2):
    row = lambda i: (i + first, 0)
    return pl.pallas_call(
        functools.partial(_combine_kernel, first),
        grid=(NT - first,),
        in_specs=[pl.BlockSpec((1, 1, 2 * TM), lambda i: (i + first, 0, 0), memory_space=pltpu.SMEM),
                  pl.BlockSpec((1, 1, 2 * TM), lambda i: (jnp.minimum(i + first + 1, NT - 1), 0, 0),
                               memory_space=pltpu.SMEM),
                  pl.BlockSpec(memory_space=pl.ANY),
                  pl.BlockSpec((TM, D), row),
                  pl.BlockSpec((TM, 2), row),
                  pl.BlockSpec((2, 1, D), lambda i: (0, 0, 0))],
        out_specs=pl.BlockSpec((TM, D), lambda i: (i, 0)),
        out_shape=jax.ShapeDtypeStruct((R - first * TM, D), f32),
        scratch_shapes=[pltpu.VMEM((2, 2, TM, D), f32), pltpu.SemaphoreType.DMA((2,))],
        compiler_params=_cparams(("arbitrary",)),
        name="combine",
    )(dest3, dest3, ys, x, gates, gate2)


def _rope_tables():
    half = HEAD_DIM // 2
    inv_freq = (ROPE_THETA ** (-np.arange(0, half, 2, dtype=np.float32) / half)).astype(np.float32)
    t = np.arange(SEQ)
    pos = np.stack([(t // GRID_W).astype(np.float32), (t % GRID_W).astype(np.float32)], axis=1)
    d = np.arange(LANES) % HEAD_DIM
    axis, sub = d // half, d % half
    ang = (pos[:, axis] * inv_freq[sub % (half // 2)][None, :]).astype(np.float32)
    cos, sin = np.cos(ang).astype(np.float32), np.sin(ang).astype(np.float32)
    first = (sub < half // 2)[None, :]
    sa = np.where(first, -sin, 0.0).astype(np.float32)
    sb = np.where(first, 0.0, sin).astype(np.float32)
    ident = np.ones((CTX, LANES), np.float32)
    zeros = np.zeros((CTX, LANES), np.float32)
    return tuple(jnp.asarray(np.concatenate(parts)) for parts in ((ident, cos), (zeros, sa), (zeros, sb)))


def kernel(x, c, ctx, c_ctx, w_ada, b_ada, g_attn, w_in, q_norm_g, k_norm_g, sink, conv_w, g_out_attn, g_out_conv, w_out, g_ffn, w_router, router_bias, w_exp_gate, w_exp_up, w_exp_down):
    assert x.shape == (1, SEQ, D) and ctx.shape == (1, CTX, D)
    xs_in = (x[0], ctx[0])

    silu = lambda t: t * jax.nn.sigmoid(t)
    s_vec = jnp.stack([silu(c_ctx), silu(c[0])])
    mods = _adaln(jnp.broadcast_to(s_vec[:, :, None], (2, D, LANES)), w_ada, b_ada).reshape(DEPTH, 2, 6, D)

    cos, sa, sb = _rope_tables()
    lane = jnp.arange(LANES)
    ones_bd = (lane[:, None] // HEAD_DIM == lane[None, :] // HEAD_DIM).astype(bf16)
    upper = (lane[:, None] < lane[None, :]).astype(bf16)
    ones = jnp.ones((LANES, LANES), bf16)
    rt = jnp.arange(RT)
    lower = (rt[None, :] < rt[:, None]).astype(bf16)
    wr_pad = jnp.pad(w_router, ((0, 0), (0, LANES - N_EXPERTS)))
    wr_hi = wr_pad.astype(bf16)
    wr = jnp.stack([wr_hi, (wr_pad - wr_hi.astype(f32)).astype(bf16)])

    for l in range(DEPTH):
        mod = mods[l]
        ms1 = jnp.stack([g_attn[l] * (1.0 + mod[:, 1]), mod[:, 0]], axis=1)
        g2 = jnp.stack([jnp.tile(q_norm_g[l] * HEAD_DIM ** -0.5, 2), jnp.tile(k_norm_g[l], 2)])
        q, k, v, u, b_gate, c_gate = _inproj(l, xs_in, ms1, w_in, cos, sa, sb, g2, ones_bd)
        attn = _attention(sink[l], q, k, v)
        tab = jnp.stack([mod[:, 2], g_ffn[l] * (1.0 + mod[:, 4]), mod[:, 3]], axis=1)
        gout = jnp.stack([g_out_attn[l], g_out_conv[l]])
        xa, h2, logits = _outproj(l, attn, u, b_gate, c_gate, xs_in, tab, gout, conv_w[l], w_out, wr)

        lt = jnp.pad(logits[:, :N_EXPERTS].T, ((0, 0), (0, RT * LANES - R))).reshape(N_EXPERTS, RT, LANES)
        _, g_t, d_t, cnt = _router(router_bias, lt, upper, lower, ones)
        gates = g_t.reshape(2, RT * LANES)[:, :R].T
        dest = d_t.reshape(2, RT * LANES)[:, :R]
        dest3 = dest.reshape(2, NT, TM).transpose(1, 0, 2).reshape(NT, 1, 2 * TM)
        counts = cnt[:, 0].astype(i32)
        pend = jnp.cumsum((counts + EBLK - 1) // EBLK * EBLK)

        xs = _dispatch(pend, dest3, h2)
        ys = _experts(l, _expert_schedule(counts), xs, w_exp_gate, w_exp_up, w_exp_down)
        xa = _combine(1 if l == DEPTH - 1 else 0, dest3, ys, xa, gates, mod[:, 5][:, None, :])
        xs_in = (xa,)

    return xa[None]
```

```python
import functools

import jax
import numpy as np
import jax.numpy as jnp
from jax import lax
from jax.experimental import pallas as pl
from jax.experimental.pallas import tpu as pltpu

f32 = jnp.float32
bf16 = jnp.bfloat16
i32 = jnp.int32

D = 2048
DEPTH = 4
SEQ = 8192
CTX = 256
R = CTX + SEQ
GRID_W = 64
ATTN_W = 1024
CONV_W = 1024
HEAD_DIM = 64
N_HEADS = 16
KV_W = 128
IN_COLS = ATTN_W + 2 * KV_W + 3 * CONV_W
ABLK = 128
WINDOW = 128
N_EXPERTS = 16
N_GROUPS = 4
EPG = 4
D_FF = 1024
EPS = 1e-6
NEG_INF = -1e30
ROPE_THETA = 10000.0

LANES = 128
SUBLANES = 8
TM = 256
NT = R // TM
EBLK = 256
EBLK_SHIFT = EBLK.bit_length() - 1
assert 1 << EBLK_SHIFT == EBLK
NBLK = -(-(2 * R + N_EXPERTS * (EBLK - 1)) // EBLK)
NROWS = NBLK * EBLK
RT = LANES
VMEM_LIMIT = 56 * 1024 * 1024


def _cparams(sem):
    return pltpu.CompilerParams(dimension_semantics=sem, vmem_limit_bytes=VMEM_LIMIT)


ADA_TK = 2048
ADA_TN = 1536


def _adaln_kernel(s_ref, w_ref, b_ref, o_ref, acc_ref):
    k = pl.program_id(2)

    @pl.when(k == 0)
    def _():
        acc_ref[...] = jnp.zeros_like(acc_ref)

    for j in range(ADA_TN // LANES):
        w = w_ref[0, :, j * LANES:(j + 1) * LANES]
        for v in range(2):
            p = (w * s_ref[v]).reshape(ADA_TK // SUBLANES, SUBLANES, LANES).sum(axis=0)
            acc_ref[v, :, j * LANES:(j + 1) * LANES] += p

    @pl.when(k == pl.num_programs(2) - 1)
    def _():
        for v in range(2):
            o_ref[0, v:v + 1, :] = acc_ref[v].sum(axis=0, keepdims=True) + b_ref[0]


def _adaln(s_bcast, w_ada, b_ada):
    return pl.pallas_call(
        _adaln_kernel,
        grid=(DEPTH, 6 * D // ADA_TN, D // ADA_TK),
        in_specs=[pl.BlockSpec((2, ADA_TK, LANES), lambda l, n, k: (0, k, 0)),
                  pl.BlockSpec((1, ADA_TK, ADA_TN), lambda l, n, k: (l, k, n)),
                  pl.BlockSpec((1, 1, ADA_TN), lambda l, n, k: (l, 0, n))],
        out_specs=pl.BlockSpec((1, 2, ADA_TN), lambda l, n, k: (l, 0, n)),
        out_shape=jax.ShapeDtypeStruct((DEPTH, 2, 6 * D), f32),
        scratch_shapes=[pltpu.VMEM((2, SUBLANES, ADA_TN), f32)],
        compiler_params=_cparams(("arbitrary", "arbitrary", "arbitrary")),
        name="adaln",
    )(s_bcast, w_ada, b_ada.reshape(DEPTH, 1, 6 * D))


def _rms_scale(x):
    return lax.rsqrt(jnp.mean(x * x, axis=-1, keepdims=True) + EPS)


WCH = 256


def _load_weight_bf16(w_hbm, wbuf, stage, sems):
    n = wbuf.shape[0] // WCH

    def chunk(c):
        return pltpu.make_async_copy(w_hbm.at[pl.ds(c * WCH, WCH)], stage.at[c % 2], sems.at[c % 2])

    chunk(0).start()
    for c in range(n):
        if c + 1 < n:
            chunk(c + 1).start()
        chunk(c).wait()
        wbuf[pl.ds(c * WCH, WCH), :] = stage[c % 2].astype(bf16)


Q0, K0, V0, U0, B0, C0 = 0, ATTN_W, ATTN_W + KV_W, ATTN_W + 2 * KV_W, ATTN_W + 2 * KV_W + CONV_W, \
    ATTN_W + 2 * KV_W + 2 * CONV_W


def _tile_rows(i, x_ref, ctx_ref):
    if ctx_ref is None:
        return x_ref[...]
    return jnp.where(i == 0, ctx_ref[...], x_ref[...])


def _row_specs(split):
    if split:
        return [pl.BlockSpec((TM, D), lambda i: (jnp.maximum(i - 1, 0), 0)), pl.BlockSpec((TM, D), lambda i: (0, 0))]
    return [pl.BlockSpec((TM, D), lambda i: (i, 0))]


def _inproj_kernel(layer, split, x_ref, *refs):
    ctx_ref, refs = (refs[0], refs[1:]) if split else (None, refs)
    (ms_ref, cos_ref, sa_ref, sb_ref, g_ref, ones_ref, w_hbm,
     q_ref, k_ref, v_ref, cu_ref, b_ref, wbuf, stage, sems) = refs

    @pl.when(pl.program_id(0) == 0)
    def _():
        _load_weight_bf16(w_hbm.at[layer], wbuf, stage, sems)

    stream = jnp.where(pl.program_id(0) == 0, 0, 1)
    x = _tile_rows(pl.program_id(0), x_ref, ctx_ref)
    gain = ms_ref[stream, 0:1, :]
    shift = ms_ref[stream, 1:2, :]
    h = (x * _rms_scale(x) * gain + shift).astype(bf16)
    p = jnp.dot(h, wbuf[...], preferred_element_type=f32)
    cu_ref[...] = p[:, C0:] * p[:, U0:B0]
    b_ref[...] = p[:, B0:C0]

    cos, sa, sb = cos_ref[...], sa_ref[...], sb_ref[...]
    ones = ones_ref[...]

    def norm_rope(xc, g):
        sq = xc * xc
        hi = sq.astype(bf16)
        lo = (sq - hi.astype(f32)).astype(bf16)
        ss = (jnp.dot(hi, ones, preferred_element_type=f32)
              + jnp.dot(lo, ones, preferred_element_type=f32))
        y = xc * lax.rsqrt(ss * (1.0 / HEAD_DIM) + EPS) * g
        return y * cos + pltpu.roll(y, LANES - 16, 1) * sa + pltpu.roll(y, 16, 1) * sb

    for j in range(ATTN_W // LANES):
        sl = slice(j * LANES, (j + 1) * LANES)
        q_ref[:, sl] = norm_rope(p[:, sl], g_ref[0:1, :]).astype(bf16)
    k_ref[...] = norm_rope(p[:, K0:V0], g_ref[1:2, :]).astype(bf16)
    v_ref[...] = p[:, V0:U0].astype(bf16)


def _inproj(layer, xs, ms, w_in, cos, sa, sb, g2, ones_bd):
    row = lambda i: (i, 0)
    outs = ((ATTN_W, bf16), (KV_W, bf16), (KV_W, bf16), (CONV_W, f32), (CONV_W, f32))
    split = len(xs) == 2
    return pl.pallas_call(
        functools.partial(_inproj_kernel, layer, split),
        grid=(NT,),
        in_specs=_row_specs(split) + [pl.BlockSpec((2, 2, D), lambda i: (0, 0, 0)),
                                      pl.BlockSpec((TM, LANES), row),
                                      pl.BlockSpec((TM, LANES), row),
                                      pl.BlockSpec((TM, LANES), row),
                                      pl.BlockSpec((2, LANES), lambda i: (0, 0)),
                                      pl.BlockSpec((LANES, LANES), lambda i: (0, 0)),
                                      pl.BlockSpec(memory_space=pl.ANY)],
        out_specs=[pl.BlockSpec((TM, n), row) for n, _ in outs],
        out_shape=[jax.ShapeDtypeStruct((R, n), dt) for n, dt in outs],
        scratch_shapes=[pltpu.VMEM((D, IN_COLS), bf16), pltpu.VMEM((2, WCH, IN_COLS), f32),
                        pltpu.SemaphoreType.DMA((2,))],
        compiler_params=_cparams(("arbitrary",)),
        name="inproj",
    )(*xs, ms, cos, sa, sb, g2, ones_bd, w_in)


NQB = R // ABLK
FIRST_LAT = CTX // ABLK
NKEYS = CTX + 3 * ABLK


def _attn_kernel(sink_ref, q_ref, kc_ref, kp_ref, kk_ref, kn_ref, vc_ref, vp_ref, vk_ref, vn_ref, o_ref):
    i = pl.program_id(0)
    kf = jnp.concatenate([kc_ref[...], kp_ref[...], kk_ref[...], kn_ref[...]], axis=0).astype(f32)
    vf = jnp.concatenate([vc_ref[...], vp_ref[...], vk_ref[...], vn_ref[...]], axis=0).astype(f32)
    lane = lax.broadcasted_iota(i32, (NKEYS, LANES), 1)
    low = lane < HEAD_DIM
    kr = pltpu.roll(kf, HEAD_DIM, 1)
    vr = pltpu.roll(vf, HEAD_DIM, 1)
    zero = jnp.zeros_like(kf)
    k_lo = (jnp.where(low, kf, zero), jnp.where(low, kr, zero))
    k_hi = (jnp.where(low, zero, kr), jnp.where(low, zero, kf))
    v_lo = (jnp.where(low, vf, zero), jnp.where(low, vr, zero))
    v_hi = (jnp.where(low, zero, vr), jnp.where(low, zero, vf))

    qi = lax.broadcasted_iota(i32, (ABLK, NKEYS), 0)
    col = lax.broadcasted_iota(i32, (ABLK, NKEYS), 1)
    kj = col - CTX
    in_window = jnp.abs(qi + ABLK - kj) <= WINDOW
    key_block = i - 1 + jnp.right_shift(kj, 7)
    first_ok = jnp.where(i >= FIRST_LAT, FIRST_LAT, NQB)
    mask = (col < CTX) | (in_window & (key_block >= first_ok) & (key_block < NQB))
    lane_q = lax.broadcasted_iota(i32, (ABLK, LANES), 1)

    contract_last = (((1,), (1,)), ((), ()))
    for h in range(2):
        qs = jnp.concatenate([q_ref[:, (h * 4 + p) * LANES:(h * 4 + p + 1) * LANES] for p in range(4)], axis=0)
        s_par = (lax.dot_general(qs, k_lo[h].astype(bf16), contract_last, preferred_element_type=f32),
                 lax.dot_general(qs, k_hi[h].astype(bf16), contract_last, preferred_element_type=f32))
        probs = ([], [])
        rden = ([], [])
        for par in range(2):
            for p in range(4):
                sink = sink_ref[h * 8 + 2 * p + par]
                s = jnp.where(mask, s_par[par][p * ABLK:(p + 1) * ABLK], NEG_INF)
                m = jnp.maximum(jnp.max(s, axis=-1, keepdims=True), sink)
                e = jnp.exp(s - m)
                den = jnp.sum(e, axis=-1, keepdims=True) + jnp.exp(sink - m)
                probs[par].append(e.astype(bf16))
                rden[par].append(1.0 / den)
        o = (jnp.dot(jnp.concatenate(probs[0], axis=0), v_lo[h].astype(bf16), preferred_element_type=f32)
             + jnp.dot(jnp.concatenate(probs[1], axis=0), v_hi[h].astype(bf16), preferred_element_type=f32))
        for p in range(4):
            scale = jnp.where(lane_q < HEAD_DIM, rden[0][p], rden[1][p])
            o_ref[:, (h * 4 + p) * LANES:(h * 4 + p + 1) * LANES] = o[p * ABLK:(p + 1) * ABLK] * scale


def _attention(sink, q, k, v):
    clip = lambda b: jnp.clip(b, FIRST_LAT, NQB - 1)
    kv_specs = [pl.BlockSpec((CTX, KV_W), lambda i: (0, 0)),
                pl.BlockSpec((ABLK, KV_W), lambda i: (clip(i - 1), 0)),
                pl.BlockSpec((ABLK, KV_W), lambda i: (i, 0)),
                pl.BlockSpec((ABLK, KV_W), lambda i: (clip(i + 1), 0))]
    return pl.pallas_call(
        _attn_kernel,
        grid=(NQB,),
        in_specs=[pl.BlockSpec(memory_space=pltpu.SMEM),
                  pl.BlockSpec((ABLK, ATTN_W), lambda i: (i, 0))] + kv_specs + kv_specs,
        out_specs=pl.BlockSpec((ABLK, ATTN_W), lambda i: (i, 0)),
        out_shape=jax.ShapeDtypeStruct((R, ATTN_W), f32),
        compiler_params=_cparams(("arbitrary",)),
        name="attention",
    )(sink, q, k, k, k, k, v, v, v, v)


DP = D // 2
HIGH_HALF = 0xFFFF0000


def _pack_bf16_pairs(t):
    lo = lax.bitcast_convert_type(t[:, :DP].astype(bf16).astype(f32), jnp.uint32)
    hi = lax.bitcast_convert_type(t[:, DP:].astype(bf16).astype(f32), jnp.uint32)
    return jnp.right_shift(lo, jnp.uint32(16)) | (hi & jnp.uint32(HIGH_HALF))


def _unpack_pairs_f32(w):
    lo = lax.bitcast_convert_type(jnp.left_shift(w, jnp.uint32(16)), f32)
    hi = lax.bitcast_convert_type(w & jnp.uint32(HIGH_HALF), f32)
    return jnp.concatenate([lo, hi], axis=1)


def _unpack_bf16_pairs(w):
    lo = lax.bitcast_convert_type(jnp.left_shift(w, jnp.uint32(16)), f32)
    hi = lax.bitcast_convert_type(w & jnp.uint32(HIGH_HALF), f32)
    return jnp.concatenate([lo.astype(bf16), hi.astype(bf16)], axis=1)


def _outproj_kernel(layer, split, attn_ref, cu_ref, b_ref, cup_ref, cun_ref, x_ref, *refs):
    ctx_ref, refs = (refs[0], refs[1:]) if split else (None, refs)
    tab_ref, gout_ref, cw_ref, w_hbm, wr_ref, xo_ref, h2_ref, lg_ref, wbuf, stage, sems = refs
    i = pl.program_id(0)

    @pl.when(i == 0)
    def _():
        _load_weight_bf16(w_hbm.at[layer], wbuf, stage, sems)

    stream = jnp.where(i == 0, 0, 1)
    prev_ok = (i >= 2).astype(f32)
    next_ok = ((i >= 1) & (i < pl.num_programs(0) - 1)).astype(f32)
    w = cu_ref[...]
    w_before = cup_ref[SUBLANES - 1:SUBLANES, :] * prev_ok
    w_after = cun_ref[0:1, :] * next_ok
    rows = lax.broadcasted_iota(i32, (TM, 1), 0)
    w_prev = jnp.where(rows == 0, w_before, pltpu.roll(w, 1, 0))
    w_next = jnp.where(rows == TM - 1, w_after, pltpu.roll(w, TM - 1, 0))
    conv = b_ref[...] * (cw_ref[0:1, :] * w_prev + cw_ref[1:2, :] * w + cw_ref[2:3, :] * w_next)

    attn = attn_ref[...]
    a_n = attn * _rms_scale(attn) * gout_ref[0:1, :]
    c_n = conv * _rms_scale(conv) * gout_ref[1:2, :]
    cat = jnp.concatenate([a_n, c_n], axis=1).astype(bf16)
    mixed = jnp.dot(cat, wbuf[...], preferred_element_type=f32)
    xn = _tile_rows(i, x_ref, ctx_ref) + tab_ref[stream, 0:1, :] * mixed
    xo_ref[...] = xn

    h2 = xn * _rms_scale(xn) * tab_ref[stream, 1:2, :] + tab_ref[stream, 2:3, :]
    h2_ref[...] = _pack_bf16_pairs(h2)
    hi = h2.astype(bf16)
    lo = (h2 - hi.astype(f32)).astype(bf16)
    lg_ref[...] = (jnp.dot(hi, wr_ref[0], preferred_element_type=f32)
                   + jnp.dot(lo, wr_ref[0], preferred_element_type=f32)
                   + jnp.dot(hi, wr_ref[1], preferred_element_type=f32))


def _outproj(layer, attn, cu, b, xs, tab, gout, cw, w_out, wr):
    halo = TM // SUBLANES
    last8 = R // SUBLANES - 1
    row = lambda i: (i, 0)
    prev8 = lambda i: (jnp.maximum(i * halo - 1, 0), 0)
    next8 = lambda i: (jnp.minimum((i + 1) * halo, last8), 0)
    split = len(xs) == 2
    return pl.pallas_call(
        functools.partial(_outproj_kernel, layer, split),
        grid=(NT,),
        in_specs=[pl.BlockSpec((TM, ATTN_W), row),
                  pl.BlockSpec((TM, CONV_W), row),
                  pl.BlockSpec((TM, CONV_W), row),
                  pl.BlockSpec((SUBLANES, CONV_W), prev8),
                  pl.BlockSpec((SUBLANES, CONV_W), next8)]
                 + _row_specs(split)
                 + [pl.BlockSpec((2, 3, D), lambda i: (0, 0, 0)),
                    pl.BlockSpec((2, CONV_W), lambda i: (0, 0)),
                    pl.BlockSpec((3, CONV_W), lambda i: (0, 0)),
                    pl.BlockSpec(memory_space=pl.ANY),
                    pl.BlockSpec((2, D, LANES), lambda i: (0, 0, 0))],
        out_specs=[pl.BlockSpec((TM, D), row),
                   pl.BlockSpec((TM, DP), row),
                   pl.BlockSpec((TM, LANES), row)],
        out_shape=[jax.ShapeDtypeStruct((R, D), f32),
                   jax.ShapeDtypeStruct((R, DP), jnp.uint32),
                   jax.ShapeDtypeStruct((R, LANES), f32)],
        scratch_shapes=[pltpu.VMEM((D, D), bf16), pltpu.VMEM((2, WCH, D), f32), pltpu.SemaphoreType.DMA((2,))],
        compiler_params=_cparams(("arbitrary",)),
        name="outproj",
    )(attn, cu, b, cu, cu, *xs, tab, gout, cw, w_out, wr)


def _router_kernel(bias_ref, lt_ref, upper_ref, lower_ref, ones_ref, e_ref, g_ref, d_ref, cnt_ref):
    score = [1.0 / (1.0 + jnp.exp(-lt_ref[e])) for e in range(N_EXPERTS)]
    sel = [score[e] + bias_ref[e] for e in range(N_EXPERTS)]

    def top2_sum(a, b, c, d):
        p, q = jnp.maximum(a, b), jnp.minimum(a, b)
        r, s = jnp.maximum(c, d), jnp.minimum(c, d)
        return jnp.maximum(p, r) + jnp.maximum(jnp.minimum(p, r), jnp.maximum(q, s))

    gscore = [top2_sum(*sel[EPG * g:EPG * (g + 1)]) for g in range(N_GROUPS)]
    best, gidx = gscore[0], jnp.zeros(gscore[0].shape, i32)
    for g in range(1, N_GROUPS):
        take = gscore[g] > best
        best = jnp.where(take, gscore[g], best)
        gidx = jnp.where(take, g, gidx)

    def pick_group(vals):
        out = []
        for j in range(EPG):
            v = vals[j]
            for g in range(1, N_GROUPS):
                v = jnp.where(gidx == g, vals[EPG * g + j], v)
            out.append(v)
        return out

    in_sel = pick_group(sel)
    in_score = pick_group(score)

    def argmax_first(vals, excluded):
        bv, bi = None, None
        for j in range(EPG):
            v = vals[j] if excluded is None else jnp.where(excluded == j, -jnp.inf, vals[j])
            if bv is None:
                bv, bi = v, jnp.zeros(v.shape, i32)
            else:
                take = v > bv
                bv = jnp.where(take, v, bv)
                bi = jnp.where(take, j, bi)
        return bi

    i1 = argmax_first(in_sel, None)
    i2 = argmax_first(in_sel, i1)

    def pick_local(vals, idx):
        v = vals[0]
        for j in range(1, EPG):
            v = jnp.where(idx == j, vals[j], v)
        return v

    s1, s2 = pick_local(in_score, i1), pick_local(in_score, i2)
    tot = s1 + s2
    e1 = gidx * EPG + i1
    e2 = gidx * EPG + i2
    e_ref[0], e_ref[1] = e1, e2
    g_ref[0], g_ref[1] = s1 / tot, s2 / tot

    tok = (lax.broadcasted_iota(i32, (RT, LANES), 0) * LANES + lax.broadcasted_iota(i32, (RT, LANES), 1))
    valid = tok < R
    onehot = [(((e1 == e) | (e2 == e)) & valid).astype(f32) for e in range(N_EXPERTS)]
    stack = jnp.concatenate(onehot, axis=0).astype(bf16)
    within = jnp.dot(stack, upper_ref[...], preferred_element_type=f32)
    rowtot = jnp.dot(stack, ones_ref[...], preferred_element_type=f32)
    d1 = jnp.zeros((RT, LANES), i32)
    d2 = jnp.zeros((RT, LANES), i32)
    seg_start = jnp.zeros((1, LANES), i32)
    for e in range(N_EXPERTS):
        rt_e = rowtot[e * RT:(e + 1) * RT]
        before = jnp.dot(lower_ref[...], rt_e.astype(bf16), preferred_element_type=f32)
        rank_e = within[e * RT:(e + 1) * RT] + before
        count_e = (before + rt_e)[RT - 1:RT, :]
        cnt_ref[e:e + 1, :] = count_e
        slot = seg_start + rank_e.astype(i32)
        d1 = jnp.where(e1 == e, slot, d1)
        d2 = jnp.where(e2 == e, slot, d2)
        blocks_e = jnp.right_shift(count_e.astype(i32) + (EBLK - 1), EBLK_SHIFT)
        seg_start = seg_start + jnp.left_shift(blocks_e, EBLK_SHIFT)
    d_ref[0], d_ref[1] = d1, d2


def _router(bias, logits_t, upper, lower, ones):
    full = lambda *shape: pl.BlockSpec(shape, lambda: (0,) * len(shape))
    return pl.pallas_call(
        _router_kernel,
        in_specs=[pl.BlockSpec(memory_space=pltpu.SMEM),
                  full(N_EXPERTS, RT, LANES), full(LANES, LANES), full(RT, RT), full(LANES, LANES)],
        out_specs=[full(2, RT, LANES), full(2, RT, LANES), full(2, RT, LANES), full(N_EXPERTS, LANES)],
        out_shape=[jax.ShapeDtypeStruct((2, RT, LANES), i32),
                   jax.ShapeDtypeStruct((2, RT, LANES), f32),
                   jax.ShapeDtypeStruct((2, RT, LANES), i32),
                   jax.ShapeDtypeStruct((N_EXPERTS, LANES), f32)],
        compiler_params=pltpu.CompilerParams(vmem_limit_bytes=VMEM_LIMIT),
        name="router",
    )(bias, logits_t, upper, lower, ones)


ROW_UNROLL = TM


HSLOTS = 3


def _dispatch_kernel(pend_ref, dest_ref, h_hbm, xs_hbm, hbuf, zbuf, lsem, rsem, zsem):
    i = pl.program_id(0)
    last = pl.num_programs(0) - 1
    slot = lax.rem(i, HSLOTS)
    prev_slot = lax.rem(i + (HSLOTS - 1), HSLOTS)
    next_slot = lax.rem(i + 1, HSLOTS)

    def load(tile, s):
        rows = pl.ds(pl.multiple_of(tile * TM, TM), TM)
        return pltpu.make_async_copy(h_hbm.at[rows], hbuf.at[s], lsem.at[s])

    @pl.when(i == 0)
    def _():
        load(0, 0).start()
        zbuf[...] = jnp.zeros_like(zbuf)

        def zero_copy(e):
            first = pl.multiple_of(pend_ref[e] - EBLK, EBLK)
            return pltpu.make_async_copy(zbuf, xs_hbm.at[pl.ds(first, EBLK)], zsem)

        def nonempty(e):
            return pend_ref[e] > (pend_ref[e - 1] if e else 0)

        for e in range(N_EXPERTS):
            pl.when(nonempty(e))(lambda e=e: zero_copy(e).start())
        for e in range(N_EXPERTS):
            pl.when(nonempty(e))(lambda e=e: zero_copy(e).wait())

        def tail_copy(j):
            return pltpu.make_async_copy(zbuf, xs_hbm.at[pl.ds(pl.multiple_of(j * EBLK, EBLK), EBLK)], zsem)

        first_unused = jnp.right_shift(pend_ref[N_EXPERTS - 1], EBLK_SHIFT)
        lax.fori_loop(first_unused, NBLK, lambda j, c: (tail_copy(j).start(), c)[1], 0)
        lax.fori_loop(first_unused, NBLK, lambda j, c: (tail_copy(j).wait(), c)[1], 0)

    @pl.when(i < last)
    def _():
        load(i + 1, next_slot).start()

    load(i, slot).wait()

    def start(t, c):
        for k in range(2):
            pltpu.make_async_copy(hbuf.at[slot, pl.ds(t, 1)], xs_hbm.at[pl.ds(dest_ref[0, 0, k * TM + t], 1)],
                                  rsem.at[slot]).start(priority=k)
        return c

    def drain(s):
        def wait(t, c):
            for k in range(2):
                pltpu.make_async_copy(hbuf.at[s, pl.ds(0, 1)], xs_hbm.at[pl.ds(0, 1)], rsem.at[s]).wait()
            return c
        lax.fori_loop(0, TM, wait, 0, unroll=ROW_UNROLL)

    lax.fori_loop(0, TM, start, 0, unroll=ROW_UNROLL)
    pl.when(i > 0)(lambda: drain(prev_slot))
    pl.when(i == last)(lambda: drain(slot))


def _dispatch(pend, dest3, h2):
    return pl.pallas_call(
        _dispatch_kernel,
        grid_spec=pltpu.PrefetchScalarGridSpec(
            num_scalar_prefetch=1,
            grid=(NT,),
            in_specs=[pl.BlockSpec((1, 1, 2 * TM), lambda i, pend: (i, 0, 0), memory_space=pltpu.SMEM),
                      pl.BlockSpec(memory_space=pl.ANY)],
            out_specs=pl.BlockSpec(memory_space=pl.ANY),
            scratch_shapes=[pltpu.VMEM((HSLOTS, TM, DP), jnp.uint32), pltpu.VMEM((EBLK, DP), jnp.uint32),
                            pltpu.SemaphoreType.DMA((HSLOTS,)), pltpu.SemaphoreType.DMA((HSLOTS,)),
                            pltpu.SemaphoreType.DMA(())]),
        out_shape=jax.ShapeDtypeStruct((NROWS, DP), jnp.uint32),
        compiler_params=_cparams(("arbitrary",)),
        name="dispatch",
    )(pend, dest3, h2)


NCH = 4
FCH = D_FF // NCH
NSTEPS = NBLK + (N_EXPERTS + 1) * NCH
(T_COMP, T_XBLK, T_CONV, T_CEXP, T_CCH, T_CSLOT, T_SLOT, T_ZERO, T_OBLK) = range(9)
T_STRIDE = 16


def _expert_schedule(counts):
    experts = jnp.arange(N_EXPERTS, dtype=i32)
    nb = (counts + EBLK - 1) // EBLK
    bstart = jnp.cumsum(nb) - nb
    nblocks = jnp.sum(nb)
    steps = jnp.maximum(nb, NCH)
    send = NCH + jnp.cumsum(steps)
    total = send[-1]
    s = jnp.arange(NSTEPS, dtype=i32)
    e_s = jnp.minimum(jnp.sum((s[:, None] >= send[None, :]).astype(i32), axis=1), N_EXPERTS - 1)
    onehot = (e_s[:, None] == experts[None, :]).astype(i32)
    pick = lambda table: jnp.sum(onehot * table[None, :], axis=1)
    i_s = s - pick(send - steps)
    nb_s = pick(nb)
    pre = s < NCH
    active = (s >= NCH) & (s < total)
    done = s >= total
    comp = active & (i_s < nb_s)
    xblk = jnp.clip(pick(bstart) + jnp.minimum(i_s, nb_s - 1), 0, nblocks - 1)
    xblk = jnp.where(pre, 0, jnp.where(done, nblocks - 1, xblk))
    has_next = e_s < N_EXPERTS - 1
    conv = pre | (active & (i_s < NCH) & has_next)
    cexp = jnp.where(pre, 0, jnp.minimum(e_s + 1, N_EXPERTS - 1))
    cch = jnp.where(pre, s, jnp.where(has_next & active, jnp.minimum(i_s, NCH - 1), NCH - 1))
    zblk = nblocks + (s - total)
    zero = done & (zblk < NBLK)
    oblk = jnp.where(done, jnp.minimum(zblk, NBLK - 1), xblk)
    cols = [comp, xblk, conv, cexp, cch, cexp % 2, e_s % 2, zero, oblk]
    cols = [col.astype(i32) for col in cols] + [jnp.zeros_like(s)] * (T_STRIDE - 9)
    return jnp.stack(cols, axis=1).reshape(NSTEPS * T_STRIDE)


def _expert_kernel(tab_ref, x_ref, g_ref, u_ref, d_ref, o_ref, wg_s, wu_s, wd_s):
    base = pl.program_id(0) * T_STRIDE

    @pl.when(tab_ref[base + T_CONV] == 1)
    def _():
        slot, c = tab_ref[base + T_CSLOT], tab_ref[base + T_CCH]
        wg_s[slot, c] = g_ref[0, 0].astype(bf16)
        wu_s[slot, c] = u_ref[0, 0].astype(bf16)
        wd_s[slot, c] = d_ref[0, 0].astype(bf16)

    @pl.when(tab_ref[base + T_COMP] == 1)
    def _():
        slot = tab_ref[base + T_SLOT]
        x = _unpack_bf16_pairs(x_ref[...])
        y = None
        for c in range(NCH):
            a = jnp.dot(x, wg_s[slot, c], preferred_element_type=f32)
            b = jnp.dot(x, wu_s[slot, c], preferred_element_type=f32)
            h = (a / (1.0 + jnp.exp(-a)) * b).astype(bf16)
            yc = jnp.dot(h, wd_s[slot, c], preferred_element_type=f32)
            y = yc if y is None else y + yc
        o_ref[...] = _pack_bf16_pairs(y)

    @pl.when(tab_ref[base + T_ZERO] == 1)
    def _():
        o_ref[...] = jnp.zeros_like(o_ref)


def _experts(layer, schedule, xs, w_gate, w_up, w_down):
    at = lambda s, tab, col: tab[s * T_STRIDE + col]
    up_spec = pl.BlockSpec((1, 1, D, FCH), lambda s, tab: (layer, at(s, tab, T_CEXP), 0, at(s, tab, T_CCH)))
    return pl.pallas_call(
        _expert_kernel,
        grid_spec=pltpu.PrefetchScalarGridSpec(
            num_scalar_prefetch=1,
            grid=(NSTEPS,),
            in_specs=[pl.BlockSpec((EBLK, DP), lambda s, tab: (at(s, tab, T_XBLK), 0)),
                      up_spec, up_spec,
                      pl.BlockSpec((1, 1, FCH, D),
                                   lambda s, tab: (layer, at(s, tab, T_CEXP), at(s, tab, T_CCH), 0))],
            out_specs=pl.BlockSpec((EBLK, DP), lambda s, tab: (at(s, tab, T_OBLK), 0)),
            scratch_shapes=[pltpu.VMEM((2, NCH, D, FCH), bf16), pltpu.VMEM((2, NCH, D, FCH), bf16),
                            pltpu.VMEM((2, NCH, FCH, D), bf16)]),
        out_shape=jax.ShapeDtypeStruct((NROWS, DP), jnp.uint32),
        compiler_params=_cparams(("arbitrary",)),
        name="experts",
    )(schedule, xs, w_gate, w_up, w_down)


def _combine_kernel(first, dest_ref, dnext_ref, ys_hbm, x_ref, gates_ref, gate_ref, o_ref, ybuf, sems):
    i = pl.program_id(0)
    slot = jnp.bitwise_and(i, 1)
    stream = jnp.where(i + first == 0, 0, 1)

    def start_tile(idx_ref, s):
        def body(t, c):
            for k in range(2):
                pltpu.make_async_copy(ys_hbm.at[pl.ds(idx_ref[0, 0, k * TM + t], 1)],
                                      ybuf.at[s, k, pl.ds(t, 1)], sems.at[s]).start(priority=k)
            return c
        lax.fori_loop(0, TM, body, 0, unroll=ROW_UNROLL)

    @pl.when(i == 0)
    def _():
        start_tile(dest_ref, 0)

    @pl.when(i + 1 < pl.num_programs(0))
    def _():
        start_tile(dnext_ref, 1 - slot)

    def wait(t, c):
        for k in range(2):
            pltpu.make_async_copy(ys_hbm.at[pl.ds(0, 1)], ybuf.at[slot, k, pl.ds(0, 1)], sems.at[slot]).wait()
        return c

    lax.fori_loop(0, TM, wait, 0, unroll=ROW_UNROLL)
    y = (_unpack_pairs_f32(ybuf[slot, 0]) * gates_ref[:, 0:1]
         + _unpack_pairs_f32(ybuf[slot, 1]) * gates_ref[:, 1:2])
    o_ref[...] = x_ref[...] + gate_ref[stream] * y


def _combine(first, dest3, ys, x, gates, gate2):
    row = lambda i: (i + first, 0)
    return pl.pallas_call(
        functools.partial(_combine_kernel, first),
        grid=(NT - first,),
        in_specs=[pl.BlockSpec((1, 1, 2 * TM), lambda i: (i + first, 0, 0), memory_space=pltpu.SMEM),
                  pl.BlockSpec((1, 1, 2 * TM), lambda i: (jnp.minimum(i + first + 1, NT - 1), 0, 0),
                               memory_space=pltpu.SMEM),
                  pl.BlockSpec(memory_space=pl.ANY),
                  pl.BlockSpec((TM, D), row),
                  pl.BlockSpec((TM, 2), row),
                  pl.BlockSpec((2, 1, D), lambda i: (0, 0, 0))],
        out_specs=pl.BlockSpec((TM, D), lambda i: (i, 0)),
        out_shape=jax.ShapeDtypeStruct((R - first * TM, D), f32),
        scratch_shapes=[pltpu.VMEM((2, 2, TM, DP), jnp.uint32), pltpu.SemaphoreType.DMA((2,))],
        compiler_params=_cparams(("arbitrary",)),
        name="combine",
    )(dest3, dest3, ys, x, gates, gate2)


def _rope_tables():
    half = HEAD_DIM // 2
    inv_freq = (ROPE_THETA ** (-np.arange(0, half, 2, dtype=np.float32) / half)).astype(np.float32)
    t = np.arange(SEQ)
    pos = np.stack([(t // GRID_W).astype(np.float32), (t % GRID_W).astype(np.float32)], axis=1)
    d = np.arange(LANES) % HEAD_DIM
    axis, sub = d // half, d % half
    ang = (pos[:, axis] * inv_freq[sub % (half // 2)][None, :]).astype(np.float32)
    cos, sin = np.cos(ang).astype(np.float32), np.sin(ang).astype(np.float32)
    first = (sub < half // 2)[None, :]
    sa = np.where(first, -sin, 0.0).astype(np.float32)
    sb = np.where(first, 0.0, sin).astype(np.float32)
    ident = np.ones((CTX, LANES), np.float32)
    zeros = np.zeros((CTX, LANES), np.float32)
    return tuple(jnp.asarray(np.concatenate(parts)) for parts in ((ident, cos), (zeros, sa), (zeros, sb)))


def kernel(x, c, ctx, c_ctx, w_ada, b_ada, g_attn, w_in, q_norm_g, k_norm_g, sink, conv_w, g_out_attn, g_out_conv, w_out, g_ffn, w_router, router_bias, w_exp_gate, w_exp_up, w_exp_down):
    assert x.shape == (1, SEQ, D) and ctx.shape == (1, CTX, D)
    xs_in = (x[0], ctx[0])

    silu = lambda t: t * jax.nn.sigmoid(t)
    s_vec = jnp.stack([silu(c_ctx), silu(c[0])])
    mods = _adaln(jnp.broadcast_to(s_vec[:, :, None], (2, D, LANES)), w_ada, b_ada).reshape(DEPTH, 2, 6, D)

    cos, sa, sb = _rope_tables()
    lane = jnp.arange(LANES)
    ones_bd = (lane[:, None] // HEAD_DIM == lane[None, :] // HEAD_DIM).astype(bf16)
    upper = (lane[:, None] < lane[None, :]).astype(bf16)
    ones = jnp.ones((LANES, LANES), bf16)
    rt = jnp.arange(RT)
    lower = (rt[None, :] < rt[:, None]).astype(bf16)
    wr_pad = jnp.pad(w_router, ((0, 0), (0, LANES - N_EXPERTS)))
    wr_hi = wr_pad.astype(bf16)
    wr = jnp.stack([wr_hi, (wr_pad - wr_hi.astype(f32)).astype(bf16)])

    for l in range(DEPTH):
        mod = mods[l]
        ms1 = jnp.stack([g_attn[l] * (1.0 + mod[:, 1]), mod[:, 0]], axis=1)
        g2 = jnp.stack([jnp.tile(q_norm_g[l] * HEAD_DIM ** -0.5, 2), jnp.tile(k_norm_g[l], 2)])
        q, k, v, cu, b_gate = _inproj(l, xs_in, ms1, w_in, cos, sa, sb, g2, ones_bd)
        attn = _attention(sink[l], q, k, v)
        tab = jnp.stack([mod[:, 2], g_ffn[l] * (1.0 + mod[:, 4]), mod[:, 3]], axis=1)
        gout = jnp.stack([g_out_attn[l], g_out_conv[l]])
        xa, h2, logits = _outproj(l, attn, cu, b_gate, xs_in, tab, gout, conv_w[l], w_out, wr)

        lt = jnp.pad(logits[:, :N_EXPERTS].T, ((0, 0), (0, RT * LANES - R))).reshape(N_EXPERTS, RT, LANES)
        _, g_t, d_t, cnt = _router(router_bias, lt, upper, lower, ones)
        gates = g_t.reshape(2, RT * LANES)[:, :R].T
        dest = d_t.reshape(2, RT * LANES)[:, :R]
        dest3 = dest.reshape(2, NT, TM).transpose(1, 0, 2).reshape(NT, 1, 2 * TM)
        counts = cnt[:, 0].astype(i32)
        pend = jnp.cumsum((counts + EBLK - 1) // EBLK * EBLK)

        xs = _dispatch(pend, dest3, h2)
        ys = _experts(l, _expert_schedule(counts), xs, w_exp_gate, w_exp_up, w_exp_down)
        xa = _combine(1 if l == DEPTH - 1 else 0, dest3, ys, xa, gates, mod[:, 5][:, None, :])
        xs_in = (xa,)

    return xa[None]
```

```python
import functools

import jax
import numpy as np
import jax.numpy as jnp
from jax import lax
from jax.experimental import pallas as pl
from jax.experimental.pallas import tpu as pltpu

f32 = jnp.float32
bf16 = jnp.bfloat16
i32 = jnp.int32

D = 2048
DEPTH = 4
SEQ = 8192
CTX = 256
R = CTX + SEQ
GRID_W = 64
ATTN_W = 1024
CONV_W = 1024
HEAD_DIM = 64
N_HEADS = 16
KV_W = 128
IN_COLS = ATTN_W + 2 * KV_W + 3 * CONV_W
ABLK = 128
WINDOW = 128
N_EXPERTS = 16
N_GROUPS = 4
EPG = 4
D_FF = 1024
EPS = 1e-6
NEG_INF = -1e30
ROPE_THETA = 10000.0

LANES = 128
SUBLANES = 8
TM = 256
NT = R // TM
EBLK = 256
EBLK_SHIFT = EBLK.bit_length() - 1
assert 1 << EBLK_SHIFT == EBLK
NBLK = -(-(2 * R + N_EXPERTS * (EBLK - 1)) // EBLK)
NROWS = NBLK * EBLK
RT = LANES
VMEM_LIMIT = 56 * 1024 * 1024


def _cparams(sem):
    return pltpu.CompilerParams(dimension_semantics=sem, vmem_limit_bytes=VMEM_LIMIT)


ADA_TK = 2048
ADA_TN = 1536


def _adaln_kernel(s_ref, w_ref, b_ref, o_ref, acc_ref):
    k = pl.program_id(2)

    @pl.when(k == 0)
    def _():
        acc_ref[...] = jnp.zeros_like(acc_ref)

    for j in range(ADA_TN // LANES):
        w = w_ref[0, :, j * LANES:(j + 1) * LANES]
        for v in range(2):
            p = (w * s_ref[v]).reshape(ADA_TK // SUBLANES, SUBLANES, LANES).sum(axis=0)
            acc_ref[v, :, j * LANES:(j + 1) * LANES] += p

    @pl.when(k == pl.num_programs(2) - 1)
    def _():
        for v in range(2):
            o_ref[0, v:v + 1, :] = acc_ref[v].sum(axis=0, keepdims=True) + b_ref[0]


def _adaln(s_bcast, w_ada, b_ada):
    return pl.pallas_call(
        _adaln_kernel,
        grid=(DEPTH, 6 * D // ADA_TN, D // ADA_TK),
        in_specs=[pl.BlockSpec((2, ADA_TK, LANES), lambda l, n, k: (0, k, 0)),
                  pl.BlockSpec((1, ADA_TK, ADA_TN), lambda l, n, k: (l, k, n)),
                  pl.BlockSpec((1, 1, ADA_TN), lambda l, n, k: (l, 0, n))],
        out_specs=pl.BlockSpec((1, 2, ADA_TN), lambda l, n, k: (l, 0, n)),
        out_shape=jax.ShapeDtypeStruct((DEPTH, 2, 6 * D), f32),
        scratch_shapes=[pltpu.VMEM((2, SUBLANES, ADA_TN), f32)],
        compiler_params=_cparams(("arbitrary", "arbitrary", "arbitrary")),
        name="adaln",
    )(s_bcast, w_ada, b_ada.reshape(DEPTH, 1, 6 * D))


def _rms_scale(x):
    return lax.rsqrt(jnp.mean(x * x, axis=-1, keepdims=True) + EPS)


WCH = 256


def _load_weight_bf16(w_hbm, wbuf, stage, sems):
    n = wbuf.shape[0] // WCH

    def chunk(c):
        return pltpu.make_async_copy(w_hbm.at[pl.ds(c * WCH, WCH)], stage.at[c % 2], sems.at[c % 2])

    chunk(0).start()
    for c in range(n):
        if c + 1 < n:
            chunk(c + 1).start()
        chunk(c).wait()
        wbuf[pl.ds(c * WCH, WCH), :] = stage[c % 2].astype(bf16)


Q0, K0, V0, U0, B0, C0 = 0, ATTN_W, ATTN_W + KV_W, ATTN_W + 2 * KV_W, ATTN_W + 2 * KV_W + CONV_W, \
    ATTN_W + 2 * KV_W + 2 * CONV_W


def _tile_rows(i, x_ref, ctx_ref):
    if ctx_ref is None:
        return x_ref[...]
    return jnp.where(i == 0, ctx_ref[...], x_ref[...])


def _row_specs(split):
    if split:
        return [pl.BlockSpec((TM, D), lambda i: (jnp.maximum(i - 1, 0), 0)), pl.BlockSpec((TM, D), lambda i: (0, 0))]
    return [pl.BlockSpec((TM, D), lambda i: (i, 0))]


def _inproj_kernel(layer, split, x_ref, *refs):
    ctx_ref, refs = (refs[0], refs[1:]) if split else (None, refs)
    (ms_ref, cos_ref, sa_ref, sb_ref, g_ref, ones_ref, w_hbm,
     q_ref, k_ref, v_ref, cu_ref, b_ref, wbuf, stage, sems) = refs

    @pl.when(pl.program_id(0) == 0)
    def _():
        _load_weight_bf16(w_hbm.at[layer], wbuf, stage, sems)

    stream = jnp.where(pl.program_id(0) == 0, 0, 1)
    x = _tile_rows(pl.program_id(0), x_ref, ctx_ref)
    gain = ms_ref[stream, 0:1, :]
    shift = ms_ref[stream, 1:2, :]
    h = (x * _rms_scale(x) * gain + shift).astype(bf16)
    p = jnp.dot(h, wbuf[...], preferred_element_type=f32)
    cu_ref[...] = p[:, C0:] * p[:, U0:B0]
    b_ref[...] = p[:, B0:C0]

    cos, sa, sb = cos_ref[...], sa_ref[...], sb_ref[...]
    ones = ones_ref[...]

    def norm_rope(xc, g):
        sq = xc * xc
        hi = sq.astype(bf16)
        lo = (sq - hi.astype(f32)).astype(bf16)
        ss = (jnp.dot(hi, ones, preferred_element_type=f32)
              + jnp.dot(lo, ones, preferred_element_type=f32))
        y = xc * lax.rsqrt(ss * (1.0 / HEAD_DIM) + EPS) * g
        return y * cos + pltpu.roll(y, LANES - 16, 1) * sa + pltpu.roll(y, 16, 1) * sb

    for j in range(ATTN_W // LANES):
        sl = slice(j * LANES, (j + 1) * LANES)
        q_ref[:, sl] = norm_rope(p[:, sl], g_ref[0:1, :]).astype(bf16)
    k_ref[...] = norm_rope(p[:, K0:V0], g_ref[1:2, :]).astype(bf16)
    v_ref[...] = p[:, V0:U0].astype(bf16)


def _inproj(layer, xs, ms, w_in, cos, sa, sb, g2, ones_bd):
    row = lambda i: (i, 0)
    outs = ((ATTN_W, bf16), (KV_W, bf16), (KV_W, bf16), (CONV_W, f32), (CONV_W, f32))
    split = len(xs) == 2
    return pl.pallas_call(
        functools.partial(_inproj_kernel, layer, split),
        grid=(NT,),
        in_specs=_row_specs(split) + [pl.BlockSpec((2, 2, D), lambda i: (0, 0, 0)),
                                      pl.BlockSpec((TM, LANES), row),
                                      pl.BlockSpec((TM, LANES), row),
                                      pl.BlockSpec((TM, LANES), row),
                                      pl.BlockSpec((2, LANES), lambda i: (0, 0)),
                                      pl.BlockSpec((LANES, LANES), lambda i: (0, 0)),
                                      pl.BlockSpec(memory_space=pl.ANY)],
        out_specs=[pl.BlockSpec((TM, n), row) for n, _ in outs],
        out_shape=[jax.ShapeDtypeStruct((R, n), dt) for n, dt in outs],
        scratch_shapes=[pltpu.VMEM((D, IN_COLS), bf16), pltpu.VMEM((2, WCH, IN_COLS), f32),
                        pltpu.SemaphoreType.DMA((2,))],
        compiler_params=_cparams(("arbitrary",)),
        name="inproj",
    )(*xs, ms, cos, sa, sb, g2, ones_bd, w_in)


NQB = R // ABLK
FIRST_LAT = CTX // ABLK
NKEYS = CTX + 3 * ABLK


def _attn_kernel(sink_ref, q_ref, kc_ref, kp_ref, kk_ref, kn_ref, vc_ref, vp_ref, vk_ref, vn_ref, o_ref):
    i = pl.program_id(0)
    kf = jnp.concatenate([kc_ref[...], kp_ref[...], kk_ref[...], kn_ref[...]], axis=0).astype(f32)
    vf = jnp.concatenate([vc_ref[...], vp_ref[...], vk_ref[...], vn_ref[...]], axis=0).astype(f32)
    lane = lax.broadcasted_iota(i32, (NKEYS, LANES), 1)
    low = lane < HEAD_DIM
    kr = pltpu.roll(kf, HEAD_DIM, 1)
    vr = pltpu.roll(vf, HEAD_DIM, 1)
    zero = jnp.zeros_like(kf)
    k_lo = (jnp.where(low, kf, zero), jnp.where(low, kr, zero))
    k_hi = (jnp.where(low, zero, kr), jnp.where(low, zero, kf))
    v_lo = (jnp.where(low, vf, zero), jnp.where(low, vr, zero))
    v_hi = (jnp.where(low, zero, vr), jnp.where(low, zero, vf))

    qi = lax.broadcasted_iota(i32, (ABLK, NKEYS), 0)
    col = lax.broadcasted_iota(i32, (ABLK, NKEYS), 1)
    kj = col - CTX
    in_window = jnp.abs(qi + ABLK - kj) <= WINDOW
    key_block = i - 1 + jnp.right_shift(kj, 7)
    first_ok = jnp.where(i >= FIRST_LAT, FIRST_LAT, NQB)
    mask = (col < CTX) | (in_window & (key_block >= first_ok) & (key_block < NQB))
    lane_q = lax.broadcasted_iota(i32, (ABLK, LANES), 1)

    contract_last = (((1,), (1,)), ((), ()))
    for h in range(2):
        qs = jnp.concatenate([q_ref[:, (h * 4 + p) * LANES:(h * 4 + p + 1) * LANES] for p in range(4)], axis=0)
        s_par = (lax.dot_general(qs, k_lo[h].astype(bf16), contract_last, preferred_element_type=f32),
                 lax.dot_general(qs, k_hi[h].astype(bf16), contract_last, preferred_element_type=f32))
        probs = ([], [])
        rden = ([], [])
        for par in range(2):
            for p in range(4):
                sink = sink_ref[h * 8 + 2 * p + par]
                s = jnp.where(mask, s_par[par][p * ABLK:(p + 1) * ABLK], NEG_INF)
                m = jnp.maximum(jnp.max(s, axis=-1, keepdims=True), sink)
                e = jnp.exp(s - m)
                den = jnp.sum(e, axis=-1, keepdims=True) + jnp.exp(sink - m)
                probs[par].append(e.astype(bf16))
                rden[par].append(1.0 / den)
        o = (jnp.dot(jnp.concatenate(probs[0], axis=0), v_lo[h].astype(bf16), preferred_element_type=f32)
             + jnp.dot(jnp.concatenate(probs[1], axis=0), v_hi[h].astype(bf16), preferred_element_type=f32))
        for p in range(4):
            scale = jnp.where(lane_q < HEAD_DIM, rden[0][p], rden[1][p])
            o_ref[:, (h * 4 + p) * LANES:(h * 4 + p + 1) * LANES] = o[p * ABLK:(p + 1) * ABLK] * scale


def _attention(sink, q, k, v):
    clip = lambda b: jnp.clip(b, FIRST_LAT, NQB - 1)
    kv_specs = [pl.BlockSpec((CTX, KV_W), lambda i: (0, 0)),
                pl.BlockSpec((ABLK, KV_W), lambda i: (clip(i - 1), 0)),
                pl.BlockSpec((ABLK, KV_W), lambda i: (i, 0)),
                pl.BlockSpec((ABLK, KV_W), lambda i: (clip(i + 1), 0))]
    return pl.pallas_call(
        _attn_kernel,
        grid=(NQB,),
        in_specs=[pl.BlockSpec(memory_space=pltpu.SMEM),
                  pl.BlockSpec((ABLK, ATTN_W), lambda i: (i, 0))] + kv_specs + kv_specs,
        out_specs=pl.BlockSpec((ABLK, ATTN_W), lambda i: (i, 0)),
        out_shape=jax.ShapeDtypeStruct((R, ATTN_W), f32),
        compiler_params=_cparams(("arbitrary",)),
        name="attention",
    )(sink, q, k, k, k, k, v, v, v, v)


DP = D // 2
HIGH_HALF = 0xFFFF0000


def _pack_bf16_pairs(t):
    lo = lax.bitcast_convert_type(t[:, :DP].astype(bf16).astype(f32), jnp.uint32)
    hi = lax.bitcast_convert_type(t[:, DP:].astype(bf16).astype(f32), jnp.uint32)
    return jnp.right_shift(lo, jnp.uint32(16)) | (hi & jnp.uint32(HIGH_HALF))


def _unpack_pairs_f32(w):
    lo = lax.bitcast_convert_type(jnp.left_shift(w, jnp.uint32(16)), f32)
    hi = lax.bitcast_convert_type(w & jnp.uint32(HIGH_HALF), f32)
    return jnp.concatenate([lo, hi], axis=1)


def _unpack_bf16_pairs(w):
    lo = lax.bitcast_convert_type(jnp.left_shift(w, jnp.uint32(16)), f32)
    hi = lax.bitcast_convert_type(w & jnp.uint32(HIGH_HALF), f32)
    return jnp.concatenate([lo.astype(bf16), hi.astype(bf16)], axis=1)


def _outproj_kernel(layer, split, attn_ref, cu_ref, b_ref, cup_ref, cun_ref, x_ref, *refs):
    ctx_ref, refs = (refs[0], refs[1:]) if split else (None, refs)
    tab_ref, gout_ref, cw_ref, w_hbm, wr_ref, xo_ref, h2_ref, lg_ref, wbuf, stage, sems = refs
    i = pl.program_id(0)

    @pl.when(i == 0)
    def _():
        _load_weight_bf16(w_hbm.at[layer], wbuf, stage, sems)

    stream = jnp.where(i == 0, 0, 1)
    prev_ok = (i >= 2).astype(f32)
    next_ok = ((i >= 1) & (i < pl.num_programs(0) - 1)).astype(f32)
    w = cu_ref[...]
    w_before = cup_ref[SUBLANES - 1:SUBLANES, :] * prev_ok
    w_after = cun_ref[0:1, :] * next_ok
    rows = lax.broadcasted_iota(i32, (TM, 1), 0)
    w_prev = jnp.where(rows == 0, w_before, pltpu.roll(w, 1, 0))
    w_next = jnp.where(rows == TM - 1, w_after, pltpu.roll(w, TM - 1, 0))
    conv = b_ref[...] * (cw_ref[0:1, :] * w_prev + cw_ref[1:2, :] * w + cw_ref[2:3, :] * w_next)

    attn = attn_ref[...]
    a_n = attn * _rms_scale(attn) * gout_ref[0:1, :]
    c_n = conv * _rms_scale(conv) * gout_ref[1:2, :]
    cat = jnp.concatenate([a_n, c_n], axis=1).astype(bf16)
    mixed = jnp.dot(cat, wbuf[...], preferred_element_type=f32)
    xn = _tile_rows(i, x_ref, ctx_ref) + tab_ref[stream, 0:1, :] * mixed
    xo_ref[...] = xn

    h2 = xn * _rms_scale(xn) * tab_ref[stream, 1:2, :] + tab_ref[stream, 2:3, :]
    h2_ref[...] = _pack_bf16_pairs(h2)
    hi = h2.astype(bf16)
    lo = (h2 - hi.astype(f32)).astype(bf16)
    lg_ref[...] = (jnp.dot(hi, wr_ref[0], preferred_element_type=f32)
                   + jnp.dot(lo, wr_ref[0], preferred_element_type=f32)
                   + jnp.dot(hi, wr_ref[1], preferred_element_type=f32))


def _outproj(layer, attn, cu, b, xs, tab, gout, cw, w_out, wr):
    halo = TM // SUBLANES
    last8 = R // SUBLANES - 1
    row = lambda i: (i, 0)
    prev8 = lambda i: (jnp.maximum(i * halo - 1, 0), 0)
    next8 = lambda i: (jnp.minimum((i + 1) * halo, last8), 0)
    split = len(xs) == 2
    return pl.pallas_call(
        functools.partial(_outproj_kernel, layer, split),
        grid=(NT,),
        in_specs=[pl.BlockSpec((TM, ATTN_W), row),
                  pl.BlockSpec((TM, CONV_W), row),
                  pl.BlockSpec((TM, CONV_W), row),
                  pl.BlockSpec((SUBLANES, CONV_W), prev8),
                  pl.BlockSpec((SUBLANES, CONV_W), next8)]
                 + _row_specs(split)
                 + [pl.BlockSpec((2, 3, D), lambda i: (0, 0, 0)),
                    pl.BlockSpec((2, CONV_W), lambda i: (0, 0)),
                    pl.BlockSpec((3, CONV_W), lambda i: (0, 0)),
                    pl.BlockSpec(memory_space=pl.ANY),
                    pl.BlockSpec((2, D, LANES), lambda i: (0, 0, 0))],
        out_specs=[pl.BlockSpec((TM, D), row),
                   pl.BlockSpec((TM, DP), row),
                   pl.BlockSpec((TM, LANES), row)],
        out_shape=[jax.ShapeDtypeStruct((R, D), f32),
                   jax.ShapeDtypeStruct((R, DP), jnp.uint32),
                   jax.ShapeDtypeStruct((R, LANES), f32)],
        scratch_shapes=[pltpu.VMEM((D, D), bf16), pltpu.VMEM((2, WCH, D), f32), pltpu.SemaphoreType.DMA((2,))],
        compiler_params=_cparams(("arbitrary",)),
        name="outproj",
    )(attn, cu, b, cu, cu, *xs, tab, gout, cw, w_out, wr)


def _router_kernel(bias_ref, lt_ref, upper_ref, lower_ref, ones_ref, e_ref, g_ref, d_ref, cnt_ref):
    score = [1.0 / (1.0 + jnp.exp(-lt_ref[e])) for e in range(N_EXPERTS)]
    sel = [score[e] + bias_ref[e] for e in range(N_EXPERTS)]

    def top2_sum(a, b, c, d):
        p, q = jnp.maximum(a, b), jnp.minimum(a, b)
        r, s = jnp.maximum(c, d), jnp.minimum(c, d)
        return jnp.maximum(p, r) + jnp.maximum(jnp.minimum(p, r), jnp.maximum(q, s))

    gscore = [top2_sum(*sel[EPG * g:EPG * (g + 1)]) for g in range(N_GROUPS)]
    best, gidx = gscore[0], jnp.zeros(gscore[0].shape, i32)
    for g in range(1, N_GROUPS):
        take = gscore[g] > best
        best = jnp.where(take, gscore[g], best)
        gidx = jnp.where(take, g, gidx)

    def pick_group(vals):
        out = []
        for j in range(EPG):
            v = vals[j]
            for g in range(1, N_GROUPS):
                v = jnp.where(gidx == g, vals[EPG * g + j], v)
            out.append(v)
        return out

    in_sel = pick_group(sel)
    in_score = pick_group(score)

    def argmax_first(vals, excluded):
        bv, bi = None, None
        for j in range(EPG):
            v = vals[j] if excluded is None else jnp.where(excluded == j, -jnp.inf, vals[j])
            if bv is None:
                bv, bi = v, jnp.zeros(v.shape, i32)
            else:
                take = v > bv
                bv = jnp.where(take, v, bv)
                bi = jnp.where(take, j, bi)
        return bi

    i1 = argmax_first(in_sel, None)
    i2 = argmax_first(in_sel, i1)

    def pick_local(vals, idx):
        v = vals[0]
        for j in range(1, EPG):
            v = jnp.where(idx == j, vals[j], v)
        return v

    s1, s2 = pick_local(in_score, i1), pick_local(in_score, i2)
    tot = s1 + s2
    e1 = gidx * EPG + i1
    e2 = gidx * EPG + i2
    e_ref[0], e_ref[1] = e1, e2
    g_ref[0], g_ref[1] = s1 / tot, s2 / tot

    tok = (lax.broadcasted_iota(i32, (RT, LANES), 0) * LANES + lax.broadcasted_iota(i32, (RT, LANES), 1))
    valid = tok < R
    onehot = [(((e1 == e) | (e2 == e)) & valid).astype(f32) for e in range(N_EXPERTS)]
    stack = jnp.concatenate(onehot, axis=0).astype(bf16)
    within = jnp.dot(stack, upper_ref[...], preferred_element_type=f32)
    rowtot = jnp.dot(stack, ones_ref[...], preferred_element_type=f32)
    d1 = jnp.zeros((RT, LANES), i32)
    d2 = jnp.zeros((RT, LANES), i32)
    seg_start = jnp.zeros((1, LANES), i32)
    for e in range(N_EXPERTS):
        rt_e = rowtot[e * RT:(e + 1) * RT]
        before = jnp.dot(lower_ref[...], rt_e.astype(bf16), preferred_element_type=f32)
        rank_e = within[e * RT:(e + 1) * RT] + before
        count_e = (before + rt_e)[RT - 1:RT, :]
        cnt_ref[e:e + 1, :] = count_e
        slot = seg_start + rank_e.astype(i32)
        d1 = jnp.where(e1 == e, slot, d1)
        d2 = jnp.where(e2 == e, slot, d2)
        blocks_e = jnp.right_shift(count_e.astype(i32) + (EBLK - 1), EBLK_SHIFT)
        seg_start = seg_start + jnp.left_shift(blocks_e, EBLK_SHIFT)
    d_ref[0], d_ref[1] = d1, d2


def _router(bias, logits_t, upper, lower, ones):
    full = lambda *shape: pl.BlockSpec(shape, lambda: (0,) * len(shape))
    return pl.pallas_call(
        _router_kernel,
        in_specs=[pl.BlockSpec(memory_space=pltpu.SMEM),
                  full(N_EXPERTS, RT, LANES), full(LANES, LANES), full(RT, RT), full(LANES, LANES)],
        out_specs=[full(2, RT, LANES), full(2, RT, LANES), full(2, RT, LANES), full(N_EXPERTS, LANES)],
        out_shape=[jax.ShapeDtypeStruct((2, RT, LANES), i32),
                   jax.ShapeDtypeStruct((2, RT, LANES), f32),
                   jax.ShapeDtypeStruct((2, RT, LANES), i32),
                   jax.ShapeDtypeStruct((N_EXPERTS, LANES), f32)],
        compiler_params=pltpu.CompilerParams(vmem_limit_bytes=VMEM_LIMIT),
        name="router",
    )(bias, logits_t, upper, lower, ones)


ROW_UNROLL = TM


HSLOTS = 3


def _dispatch_kernel(pend_ref, dest_ref, h_hbm, xs_hbm, hbuf, zbuf, lsem, rsem, zsem):
    i = pl.program_id(0)
    last = pl.num_programs(0) - 1
    slot = lax.rem(i, HSLOTS)
    prev_slot = lax.rem(i + (HSLOTS - 1), HSLOTS)
    next_slot = lax.rem(i + 1, HSLOTS)

    def load(tile, s):
        rows = pl.ds(pl.multiple_of(tile * TM, TM), TM)
        return pltpu.make_async_copy(h_hbm.at[rows], hbuf.at[s], lsem.at[s])

    @pl.when(i == 0)
    def _():
        load(0, 0).start()
        zbuf[...] = jnp.zeros_like(zbuf)

        def zero_copy(e):
            first = pl.multiple_of(pend_ref[e] - EBLK, EBLK)
            return pltpu.make_async_copy(zbuf, xs_hbm.at[pl.ds(first, EBLK)], zsem)

        def nonempty(e):
            return pend_ref[e] > (pend_ref[e - 1] if e else 0)

        for e in range(N_EXPERTS):
            pl.when(nonempty(e))(lambda e=e: zero_copy(e).start())
        for e in range(N_EXPERTS):
            pl.when(nonempty(e))(lambda e=e: zero_copy(e).wait())

        def tail_copy(j):
            return pltpu.make_async_copy(zbuf, xs_hbm.at[pl.ds(pl.multiple_of(j * EBLK, EBLK), EBLK)], zsem)

        first_unused = jnp.right_shift(pend_ref[N_EXPERTS - 1], EBLK_SHIFT)
        lax.fori_loop(first_unused, NBLK, lambda j, c: (tail_copy(j).start(), c)[1], 0)
        lax.fori_loop(first_unused, NBLK, lambda j, c: (tail_copy(j).wait(), c)[1], 0)

    @pl.when(i < last)
    def _():
        load(i + 1, next_slot).start()

    load(i, slot).wait()

    def start(t, c):
        for k in range(2):
            pltpu.make_async_copy(hbuf.at[slot, pl.ds(t, 1)], xs_hbm.at[pl.ds(dest_ref[0, 0, k * TM + t], 1)],
                                  rsem.at[slot]).start(priority=k)
        return c

    def drain(s):
        for k in range(2):
            pltpu.make_async_copy(hbuf.at[s], xs_hbm.at[pl.ds(0, TM)], rsem.at[s]).wait()

    lax.fori_loop(0, TM, start, 0, unroll=ROW_UNROLL)
    pl.when(i > 0)(lambda: drain(prev_slot))
    pl.when(i == last)(lambda: drain(slot))


def _dispatch(pend, dest3, h2):
    return pl.pallas_call(
        _dispatch_kernel,
        grid_spec=pltpu.PrefetchScalarGridSpec(
            num_scalar_prefetch=1,
            grid=(NT,),
            in_specs=[pl.BlockSpec((1, 1, 2 * TM), lambda i, pend: (i, 0, 0), memory_space=pltpu.SMEM),
                      pl.BlockSpec(memory_space=pl.ANY)],
            out_specs=pl.BlockSpec(memory_space=pl.ANY),
            scratch_shapes=[pltpu.VMEM((HSLOTS, TM, DP), jnp.uint32), pltpu.VMEM((EBLK, DP), jnp.uint32),
                            pltpu.SemaphoreType.DMA((HSLOTS,)), pltpu.SemaphoreType.DMA((HSLOTS,)),
                            pltpu.SemaphoreType.DMA(())]),
        out_shape=jax.ShapeDtypeStruct((NROWS, DP), jnp.uint32),
        compiler_params=_cparams(("arbitrary",)),
        name="dispatch",
    )(pend, dest3, h2)


NCH = 4
FCH = D_FF // NCH
KCH = D // NCH
NSTEPS = NBLK + (N_EXPERTS + 1) * NCH
(T_COMP, T_XBLK, T_CONV, T_CEXP, T_CCH, T_CSLOT, T_SLOT, T_ZERO, T_OBLK) = range(9)
T_STRIDE = 16


def _expert_schedule(counts):
    experts = jnp.arange(N_EXPERTS, dtype=i32)
    nb = (counts + EBLK - 1) // EBLK
    bstart = jnp.cumsum(nb) - nb
    nblocks = jnp.sum(nb)
    steps = jnp.maximum(nb, NCH)
    send = NCH + jnp.cumsum(steps)
    total = send[-1]
    s = jnp.arange(NSTEPS, dtype=i32)
    e_s = jnp.minimum(jnp.sum((s[:, None] >= send[None, :]).astype(i32), axis=1), N_EXPERTS - 1)
    onehot = (e_s[:, None] == experts[None, :]).astype(i32)
    pick = lambda table: jnp.sum(onehot * table[None, :], axis=1)
    i_s = s - pick(send - steps)
    nb_s = pick(nb)
    pre = s < NCH
    active = (s >= NCH) & (s < total)
    done = s >= total
    comp = active & (i_s < nb_s)
    xblk = jnp.clip(pick(bstart) + jnp.minimum(i_s, nb_s - 1), 0, nblocks - 1)
    xblk = jnp.where(pre, 0, jnp.where(done, nblocks - 1, xblk))
    has_next = e_s < N_EXPERTS - 1
    conv = pre | (active & (i_s < NCH) & has_next)
    cexp = jnp.where(pre, 0, jnp.minimum(e_s + 1, N_EXPERTS - 1))
    cch = jnp.where(pre, s, jnp.where(has_next & active, jnp.minimum(i_s, NCH - 1), NCH - 1))
    zblk = nblocks + (s - total)
    zero = done & (zblk < NBLK)
    oblk = jnp.where(done, jnp.minimum(zblk, NBLK - 1), xblk)
    cols = [comp, xblk, conv, cexp, cch, cexp % 2, e_s % 2, zero, oblk]
    cols = [col.astype(i32) for col in cols] + [jnp.zeros_like(s)] * (T_STRIDE - 9)
    return jnp.stack(cols, axis=1).reshape(NSTEPS * T_STRIDE)


def _expert_kernel(tab_ref, x_ref, g_ref, u_ref, d_ref, o_ref, wg_s, wu_s, wd_s):
    base = pl.program_id(0) * T_STRIDE

    @pl.when(tab_ref[base + T_CONV] == 1)
    def _():
        slot, c = tab_ref[base + T_CSLOT], tab_ref[base + T_CCH]
        wg_s[slot, c] = g_ref[0, 0].astype(bf16)
        wu_s[slot, c] = u_ref[0, 0].astype(bf16)
        wd_s[slot, c] = d_ref[0, 0].astype(bf16)

    @pl.when(tab_ref[base + T_COMP] == 1)
    def _():
        slot = tab_ref[base + T_SLOT]
        x = _unpack_bf16_pairs(x_ref[...])
        a = b = None
        for c in range(NCH):
            xc = x[:, c * KCH:(c + 1) * KCH]
            pa = jnp.dot(xc, wg_s[slot, c], preferred_element_type=f32)
            pb = jnp.dot(xc, wu_s[slot, c], preferred_element_type=f32)
            a = pa if a is None else a + pa
            b = pb if b is None else b + pb
        h = (a / (1.0 + jnp.exp(-a)) * b).astype(bf16)
        y = None
        for c in range(NCH):
            yc = jnp.dot(h[:, c * FCH:(c + 1) * FCH], wd_s[slot, c], preferred_element_type=f32)
            y = yc if y is None else y + yc
        o_ref[...] = _pack_bf16_pairs(y)

    @pl.when(tab_ref[base + T_ZERO] == 1)
    def _():
        o_ref[...] = jnp.zeros_like(o_ref)


def _experts(layer, schedule, xs, w_gate, w_up, w_down):
    at = lambda s, tab, col: tab[s * T_STRIDE + col]
    up_spec = pl.BlockSpec((1, 1, KCH, D_FF), lambda s, tab: (layer, at(s, tab, T_CEXP), at(s, tab, T_CCH), 0))
    return pl.pallas_call(
        _expert_kernel,
        grid_spec=pltpu.PrefetchScalarGridSpec(
            num_scalar_prefetch=1,
            grid=(NSTEPS,),
            in_specs=[pl.BlockSpec((EBLK, DP), lambda s, tab: (at(s, tab, T_XBLK), 0)),
                      up_spec, up_spec,
                      pl.BlockSpec((1, 1, FCH, D),
                                   lambda s, tab: (layer, at(s, tab, T_CEXP), at(s, tab, T_CCH), 0))],
            out_specs=pl.BlockSpec((EBLK, DP), lambda s, tab: (at(s, tab, T_OBLK), 0)),
            scratch_shapes=[pltpu.VMEM((2, NCH, KCH, D_FF), bf16), pltpu.VMEM((2, NCH, KCH, D_FF), bf16),
                            pltpu.VMEM((2, NCH, FCH, D), bf16)]),
        out_shape=jax.ShapeDtypeStruct((NROWS, DP), jnp.uint32),
        compiler_params=_cparams(("arbitrary",)),
        name="experts",
    )(schedule, xs, w_gate, w_up, w_down)


def _combine_kernel(first, dest_ref, dnext_ref, ys_hbm, x_ref, gates_ref, gate_ref, o_ref, ybuf, sems):
    i = pl.program_id(0)
    slot = jnp.bitwise_and(i, 1)
    stream = jnp.where(i + first == 0, 0, 1)

    def start_tile(idx_ref, s):
        def body(t, c):
            for k in range(2):
                pltpu.make_async_copy(ys_hbm.at[pl.ds(idx_ref[0, 0, k * TM + t], 1)],
                                      ybuf.at[s, k, pl.ds(t, 1)], sems.at[s]).start(priority=k)
            return c
        lax.fori_loop(0, TM, body, 0, unroll=ROW_UNROLL)

    @pl.when(i == 0)
    def _():
        start_tile(dest_ref, 0)

    @pl.when(i + 1 < pl.num_programs(0))
    def _():
        start_tile(dnext_ref, 1 - slot)

    for k in range(2):
        pltpu.make_async_copy(ys_hbm.at[pl.ds(0, TM)], ybuf.at[slot, k], sems.at[slot]).wait()
    y = (_unpack_pairs_f32(ybuf[slot, 0]) * gates_ref[:, 0:1]
         + _unpack_pairs_f32(ybuf[slot, 1]) * gates_ref[:, 1:2])
    o_ref[...] = x_ref[...] + gate_ref[stream] * y


def _combine(first, dest3, ys, x, gates, gate2):
    row = lambda i: (i + first, 0)
    return pl.pallas_call(
        functools.partial(_combine_kernel, first),
        grid=(NT - first,),
        in_specs=[pl.BlockSpec((1, 1, 2 * TM), lambda i: (i + first, 0, 0), memory_space=pltpu.SMEM),
                  pl.BlockSpec((1, 1, 2 * TM), lambda i: (jnp.minimum(i + first + 1, NT - 1), 0, 0),
                               memory_space=pltpu.SMEM),
                  pl.BlockSpec(memory_space=pl.ANY),
                  pl.BlockSpec((TM, D), row),
                  pl.BlockSpec((TM, 2), row),
                  pl.BlockSpec((2, 1, D), lambda i: (0, 0, 0))],
        out_specs=pl.BlockSpec((TM, D), lambda i: (i, 0)),
        out_shape=jax.ShapeDtypeStruct((R - first * TM, D), f32),
        scratch_shapes=[pltpu.VMEM((2, 2, TM, DP), jnp.uint32), pltpu.SemaphoreType.DMA((2,))],
        compiler_params=_cparams(("arbitrary",)),
        name="combine",
    )(dest3, dest3, ys, x, gates, gate2)


def _rope_tables():
    half = HEAD_DIM // 2
    inv_freq = (ROPE_THETA ** (-np.arange(0, half, 2, dtype=np.float32) / half)).astype(np.float32)
    t = np.arange(SEQ)
    pos = np.stack([(t // GRID_W).astype(np.float32), (t % GRID_W).astype(np.float32)], axis=1)
    d = np.arange(LANES) % HEAD_DIM
    axis, sub = d // half, d % half
    ang = (pos[:, axis] * inv_freq[sub % (half // 2)][None, :]).astype(np.float32)
    cos, sin = np.cos(ang).astype(np.float32), np.sin(ang).astype(np.float32)
    first = (sub < half // 2)[None, :]
    sa = np.where(first, -sin, 0.0).astype(np.float32)
    sb = np.where(first, 0.0, sin).astype(np.float32)
    ident = np.ones((CTX, LANES), np.float32)
    zeros = np.zeros((CTX, LANES), np.float32)
    return tuple(jnp.asarray(np.concatenate(parts)) for parts in ((ident, cos), (zeros, sa), (zeros, sb)))


def kernel(x, c, ctx, c_ctx, w_ada, b_ada, g_attn, w_in, q_norm_g, k_norm_g, sink, conv_w, g_out_attn, g_out_conv, w_out, g_ffn, w_router, router_bias, w_exp_gate, w_exp_up, w_exp_down):
    assert x.shape == (1, SEQ, D) and ctx.shape == (1, CTX, D)
    xs_in = (x[0], ctx[0])

    silu = lambda t: t * jax.nn.sigmoid(t)
    s_vec = jnp.stack([silu(c_ctx), silu(c[0])])
    mods = _adaln(jnp.broadcast_to(s_vec[:, :, None], (2, D, LANES)), w_ada, b_ada).reshape(DEPTH, 2, 6, D)

    cos, sa, sb = _rope_tables()
    lane = jnp.arange(LANES)
    ones_bd = (lane[:, None] // HEAD_DIM == lane[None, :] // HEAD_DIM).astype(bf16)
    upper = (lane[:, None] < lane[None, :]).astype(bf16)
    ones = jnp.ones((LANES, LANES), bf16)
    rt = jnp.arange(RT)
    lower = (rt[None, :] < rt[:, None]).astype(bf16)
    wr_pad = jnp.pad(w_router, ((0, 0), (0, LANES - N_EXPERTS)))
    wr_hi = wr_pad.astype(bf16)
    wr = jnp.stack([wr_hi, (wr_pad - wr_hi.astype(f32)).astype(bf16)])

    for l in range(DEPTH):
        mod = mods[l]
        ms1 = jnp.stack([g_attn[l] * (1.0 + mod[:, 1]), mod[:, 0]], axis=1)
        g2 = jnp.stack([jnp.tile(q_norm_g[l] * HEAD_DIM ** -0.5, 2), jnp.tile(k_norm_g[l], 2)])
        q, k, v, cu, b_gate = _inproj(l, xs_in, ms1, w_in, cos, sa, sb, g2, ones_bd)
        attn = _attention(sink[l], q, k, v)
        tab = jnp.stack([mod[:, 2], g_ffn[l] * (1.0 + mod[:, 4]), mod[:, 3]], axis=1)
        gout = jnp.stack([g_out_attn[l], g_out_conv[l]])
        xa, h2, logits = _outproj(l, attn, cu, b_gate, xs_in, tab, gout, conv_w[l], w_out, wr)

        lt = jnp.pad(logits[:, :N_EXPERTS].T, ((0, 0), (0, RT * LANES - R))).reshape(N_EXPERTS, RT, LANES)
        _, g_t, d_t, cnt = _router(router_bias, lt, upper, lower, ones)
        gates = g_t.reshape(2, RT * LANES)[:, :R].T
        dest = d_t.reshape(2, RT * LANES)[:, :R]
        dest3 = dest.reshape(2, NT, TM).transpose(1, 0, 2).reshape(NT, 1, 2 * TM)
        counts = cnt[:, 0].astype(i32)
        pend = jnp.cumsum((counts + EBLK - 1) // EBLK * EBLK)

        xs = _dispatch(pend, dest3, h2)
        ys = _experts(l, _expert_schedule(counts), xs, w_exp_gate, w_exp_up, w_exp_down)
        xa = _combine(1 if l == DEPTH - 1 else 0, dest3, ys, xa, gates, mod[:, 5][:, None, :])
        xs_in = (xa,)

    return xa[None]
```

```python
import functools

import jax
import numpy as np
import jax.numpy as jnp
from jax import lax
from jax.experimental import pallas as pl
from jax.experimental.pallas import tpu as pltpu

f32 = jnp.float32
bf16 = jnp.bfloat16
i32 = jnp.int32

D = 2048
DEPTH = 4
SEQ = 8192
CTX = 256
R = CTX + SEQ
GRID_W = 64
ATTN_W = 1024
CONV_W = 1024
HEAD_DIM = 64
N_HEADS = 16
KV_W = 128
IN_COLS = ATTN_W + 2 * KV_W + 3 * CONV_W
ABLK = 128
WINDOW = 128
N_EXPERTS = 16
N_GROUPS = 4
EPG = 4
D_FF = 1024
EPS = 1e-6
NEG_INF = -1e30
ROPE_THETA = 10000.0

LANES = 128
SUBLANES = 8
TM = 256
NT = R // TM
EBLK = 256
EBLK_SHIFT = EBLK.bit_length() - 1
assert 1 << EBLK_SHIFT == EBLK
NBLK = -(-(2 * R + N_EXPERTS * (EBLK - 1)) // EBLK)
NROWS = NBLK * EBLK
RT = LANES
VMEM_LIMIT = 56 * 1024 * 1024


def _cparams(sem):
    return pltpu.CompilerParams(dimension_semantics=sem, vmem_limit_bytes=VMEM_LIMIT)


ADA_TK = 2048
ADA_TN = 1536


def _adaln_kernel(s_ref, w_ref, b_ref, o_ref, acc_ref):
    k = pl.program_id(2)

    @pl.when(k == 0)
    def _():
        acc_ref[...] = jnp.zeros_like(acc_ref)

    for j in range(ADA_TN // LANES):
        w = w_ref[0, :, j * LANES:(j + 1) * LANES]
        for v in range(2):
            p = (w * s_ref[v]).reshape(ADA_TK // SUBLANES, SUBLANES, LANES).sum(axis=0)
            acc_ref[v, :, j * LANES:(j + 1) * LANES] += p

    @pl.when(k == pl.num_programs(2) - 1)
    def _():
        for v in range(2):
            o_ref[0, v:v + 1, :] = acc_ref[v].sum(axis=0, keepdims=True) + b_ref[0]


def _adaln(s_bcast, w_ada, b_ada):
    return pl.pallas_call(
        _adaln_kernel,
        grid=(DEPTH, 6 * D // ADA_TN, D // ADA_TK),
        in_specs=[pl.BlockSpec((2, ADA_TK, LANES), lambda l, n, k: (0, k, 0)),
                  pl.BlockSpec((1, ADA_TK, ADA_TN), lambda l, n, k: (l, k, n)),
                  pl.BlockSpec((1, 1, ADA_TN), lambda l, n, k: (l, 0, n))],
        out_specs=pl.BlockSpec((1, 2, ADA_TN), lambda l, n, k: (l, 0, n)),
        out_shape=jax.ShapeDtypeStruct((DEPTH, 2, 6 * D), f32),
        scratch_shapes=[pltpu.VMEM((2, SUBLANES, ADA_TN), f32)],
        compiler_params=_cparams(("arbitrary", "arbitrary", "arbitrary")),
        name="adaln",
    )(s_bcast, w_ada, b_ada.reshape(DEPTH, 1, 6 * D))


def _rms_scale(x):
    return lax.rsqrt(jnp.mean(x * x, axis=-1, keepdims=True) + EPS)


WCH = 256


def _load_weight_bf16(w_hbm, wbuf, stage, sems):
    n = wbuf.shape[0] // WCH

    def chunk(c):
        return pltpu.make_async_copy(w_hbm.at[pl.ds(c * WCH, WCH)], stage.at[c % 2], sems.at[c % 2])

    chunk(0).start()
    for c in range(n):
        if c + 1 < n:
            chunk(c + 1).start()
        chunk(c).wait()
        wbuf[pl.ds(c * WCH, WCH), :] = stage[c % 2].astype(bf16)


Q0, K0, V0, U0, B0, C0 = 0, ATTN_W, ATTN_W + KV_W, ATTN_W + 2 * KV_W, ATTN_W + 2 * KV_W + CONV_W, \
    ATTN_W + 2 * KV_W + 2 * CONV_W


def _tile_rows(i, x_ref, ctx_ref):
    if ctx_ref is None:
        return x_ref[...]
    return jnp.where(i == 0, ctx_ref[...], x_ref[...])


def _row_specs(split):
    if split:
        return [pl.BlockSpec((TM, D), lambda i: (jnp.maximum(i - 1, 0), 0)), pl.BlockSpec((TM, D), lambda i: (0, 0))]
    return [pl.BlockSpec((TM, D), lambda i: (i, 0))]


def _inproj_kernel(layer, split, x_ref, *refs):
    ctx_ref, refs = (refs[0], refs[1:]) if split else (None, refs)
    (ms_ref, cos_ref, sa_ref, sb_ref, g_ref, ones_ref, w_hbm,
     q_ref, k_ref, v_ref, cu_ref, b_ref, wbuf, stage, sems) = refs

    @pl.when(pl.program_id(0) == 0)
    def _():
        _load_weight_bf16(w_hbm.at[layer], wbuf, stage, sems)

    stream = jnp.where(pl.program_id(0) == 0, 0, 1)
    x = _tile_rows(pl.program_id(0), x_ref, ctx_ref)
    gain = ms_ref[stream, 0:1, :]
    shift = ms_ref[stream, 1:2, :]
    h = (x * _rms_scale(x) * gain + shift).astype(bf16)
    p = jnp.dot(h, wbuf[...], preferred_element_type=f32)
    cu_ref[...] = p[:, C0:] * p[:, U0:B0]
    b_ref[...] = p[:, B0:C0]

    cos, sa, sb = cos_ref[...], sa_ref[...], sb_ref[...]
    ones = ones_ref[...]

    def norm_rope(xc, g):
        sq = xc * xc
        hi = sq.astype(bf16)
        lo = (sq - hi.astype(f32)).astype(bf16)
        ss = (jnp.dot(hi, ones, preferred_element_type=f32)
              + jnp.dot(lo, ones, preferred_element_type=f32))
        y = xc * lax.rsqrt(ss * (1.0 / HEAD_DIM) + EPS) * g
        return y * cos + pltpu.roll(y, LANES - 16, 1) * sa + pltpu.roll(y, 16, 1) * sb

    for j in range(ATTN_W // LANES):
        sl = slice(j * LANES, (j + 1) * LANES)
        q_ref[:, sl] = norm_rope(p[:, sl], g_ref[0:1, :]).astype(bf16)
    k_ref[...] = norm_rope(p[:, K0:V0], g_ref[1:2, :]).astype(bf16)
    v_ref[...] = p[:, V0:U0].astype(bf16)


def _inproj(layer, xs, ms, w_in, cos, sa, sb, g2, ones_bd):
    row = lambda i: (i, 0)
    outs = ((ATTN_W, bf16), (KV_W, bf16), (KV_W, bf16), (CONV_W, f32), (CONV_W, f32))
    split = len(xs) == 2
    return pl.pallas_call(
        functools.partial(_inproj_kernel, layer, split),
        grid=(NT,),
        in_specs=_row_specs(split) + [pl.BlockSpec((2, 2, D), lambda i: (0, 0, 0)),
                                      pl.BlockSpec((TM, LANES), row),
                                      pl.BlockSpec((TM, LANES), row),
                                      pl.BlockSpec((TM, LANES), row),
                                      pl.BlockSpec((2, LANES), lambda i: (0, 0)),
                                      pl.BlockSpec((LANES, LANES), lambda i: (0, 0)),
                                      pl.BlockSpec(memory_space=pl.ANY)],
        out_specs=[pl.BlockSpec((TM, n), row) for n, _ in outs],
        out_shape=[jax.ShapeDtypeStruct((R, n), dt) for n, dt in outs],
        scratch_shapes=[pltpu.VMEM((D, IN_COLS), bf16), pltpu.VMEM((2, WCH, IN_COLS), f32),
                        pltpu.SemaphoreType.DMA((2,))],
        compiler_params=_cparams(("arbitrary",)),
        name="inproj",
    )(*xs, ms, cos, sa, sb, g2, ones_bd, w_in)


NQB = R // ABLK
FIRST_LAT = CTX // ABLK
NKEYS = CTX + 3 * ABLK


def _attn_kernel(sink_ref, q_ref, kc_ref, kp_ref, kk_ref, kn_ref, vc_ref, vp_ref, vk_ref, vn_ref, o_ref):
    i = pl.program_id(0)
    kf = jnp.concatenate([kc_ref[...], kp_ref[...], kk_ref[...], kn_ref[...]], axis=0).astype(f32)
    vf = jnp.concatenate([vc_ref[...], vp_ref[...], vk_ref[...], vn_ref[...]], axis=0).astype(f32)
    lane = lax.broadcasted_iota(i32, (NKEYS, LANES), 1)
    low = lane < HEAD_DIM
    kr = pltpu.roll(kf, HEAD_DIM, 1)
    vr = pltpu.roll(vf, HEAD_DIM, 1)
    zero = jnp.zeros_like(kf)
    k_lo = (jnp.where(low, kf, zero), jnp.where(low, kr, zero))
    k_hi = (jnp.where(low, zero, kr), jnp.where(low, zero, kf))
    v_lo = (jnp.where(low, vf, zero), jnp.where(low, vr, zero))
    v_hi = (jnp.where(low, zero, vr), jnp.where(low, zero, vf))

    qi = lax.broadcasted_iota(i32, (ABLK, NKEYS), 0)
    col = lax.broadcasted_iota(i32, (ABLK, NKEYS), 1)
    kj = col - CTX
    in_window = jnp.abs(qi + ABLK - kj) <= WINDOW
    key_block = i - 1 + jnp.right_shift(kj, 7)
    first_ok = jnp.where(i >= FIRST_LAT, FIRST_LAT, NQB)
    mask = (col < CTX) | (in_window & (key_block >= first_ok) & (key_block < NQB))
    lane_q = lax.broadcasted_iota(i32, (ABLK, LANES), 1)

    contract_last = (((1,), (1,)), ((), ()))
    for h in range(2):
        qs = jnp.concatenate([q_ref[:, (h * 4 + p) * LANES:(h * 4 + p + 1) * LANES] for p in range(4)], axis=0)
        s_par = (lax.dot_general(qs, k_lo[h].astype(bf16), contract_last, preferred_element_type=f32),
                 lax.dot_general(qs, k_hi[h].astype(bf16), contract_last, preferred_element_type=f32))
        probs = ([], [])
        rden = ([], [])
        for par in range(2):
            for p in range(4):
                sink = sink_ref[h * 8 + 2 * p + par]
                s = jnp.where(mask, s_par[par][p * ABLK:(p + 1) * ABLK], NEG_INF)
                m = jnp.maximum(jnp.max(s, axis=-1, keepdims=True), sink)
                e = jnp.exp(s - m)
                den = jnp.sum(e, axis=-1, keepdims=True) + jnp.exp(sink - m)
                probs[par].append(e.astype(bf16))
                rden[par].append(1.0 / den)
        o = (jnp.dot(jnp.concatenate(probs[0], axis=0), v_lo[h].astype(bf16), preferred_element_type=f32)
             + jnp.dot(jnp.concatenate(probs[1], axis=0), v_hi[h].astype(bf16), preferred_element_type=f32))
        for p in range(4):
            scale = jnp.where(lane_q < HEAD_DIM, rden[0][p], rden[1][p])
            o_ref[:, (h * 4 + p) * LANES:(h * 4 + p + 1) * LANES] = o[p * ABLK:(p + 1) * ABLK] * scale


def _attention(sink, q, k, v):
    clip = lambda b: jnp.clip(b, FIRST_LAT, NQB - 1)
    kv_specs = [pl.BlockSpec((CTX, KV_W), lambda i: (0, 0)),
                pl.BlockSpec((ABLK, KV_W), lambda i: (clip(i - 1), 0)),
                pl.BlockSpec((ABLK, KV_W), lambda i: (i, 0)),
                pl.BlockSpec((ABLK, KV_W), lambda i: (clip(i + 1), 0))]
    return pl.pallas_call(
        _attn_kernel,
        grid=(NQB,),
        in_specs=[pl.BlockSpec(memory_space=pltpu.SMEM),
                  pl.BlockSpec((ABLK, ATTN_W), lambda i: (i, 0))] + kv_specs + kv_specs,
        out_specs=pl.BlockSpec((ABLK, ATTN_W), lambda i: (i, 0)),
        out_shape=jax.ShapeDtypeStruct((R, ATTN_W), f32),
        compiler_params=_cparams(("arbitrary",)),
        name="attention",
    )(sink, q, k, k, k, k, v, v, v, v)


DP = D // 2
HIGH_HALF = 0xFFFF0000


def _pack_bf16_pairs(t):
    lo = lax.bitcast_convert_type(t[:, :DP].astype(bf16).astype(f32), jnp.uint32)
    hi = lax.bitcast_convert_type(t[:, DP:].astype(bf16).astype(f32), jnp.uint32)
    return jnp.right_shift(lo, jnp.uint32(16)) | (hi & jnp.uint32(HIGH_HALF))


def _unpack_pairs_f32(w):
    lo = lax.bitcast_convert_type(jnp.left_shift(w, jnp.uint32(16)), f32)
    hi = lax.bitcast_convert_type(w & jnp.uint32(HIGH_HALF), f32)
    return jnp.concatenate([lo, hi], axis=1)


def _unpack_bf16_pairs(w):
    lo = lax.bitcast_convert_type(jnp.left_shift(w, jnp.uint32(16)), f32)
    hi = lax.bitcast_convert_type(w & jnp.uint32(HIGH_HALF), f32)
    return jnp.concatenate([lo.astype(bf16), hi.astype(bf16)], axis=1)


def _outproj_kernel(layer, split, attn_ref, cu_ref, b_ref, cup_ref, cun_ref, x_ref, *refs):
    ctx_ref, refs = (refs[0], refs[1:]) if split else (None, refs)
    tab_ref, gout_ref, cw_ref, w_hbm, wr_ref, xo_ref, h2_ref, lg_ref, wbuf, stage, sems = refs
    i = pl.program_id(0)

    @pl.when(i == 0)
    def _():
        _load_weight_bf16(w_hbm.at[layer], wbuf, stage, sems)

    stream = jnp.where(i == 0, 0, 1)
    prev_ok = (i >= 2).astype(f32)
    next_ok = ((i >= 1) & (i < pl.num_programs(0) - 1)).astype(f32)
    w = cu_ref[...]
    w_before = cup_ref[SUBLANES - 1:SUBLANES, :] * prev_ok
    w_after = cun_ref[0:1, :] * next_ok
    rows = lax.broadcasted_iota(i32, (TM, 1), 0)
    w_prev = jnp.where(rows == 0, w_before, pltpu.roll(w, 1, 0))
    w_next = jnp.where(rows == TM - 1, w_after, pltpu.roll(w, TM - 1, 0))
    conv = b_ref[...] * (cw_ref[0:1, :] * w_prev + cw_ref[1:2, :] * w + cw_ref[2:3, :] * w_next)

    attn = attn_ref[...]
    a_n = attn * _rms_scale(attn) * gout_ref[0:1, :]
    c_n = conv * _rms_scale(conv) * gout_ref[1:2, :]
    cat = jnp.concatenate([a_n, c_n], axis=1).astype(bf16)
    mixed = jnp.dot(cat, wbuf[...], preferred_element_type=f32)
    xn = _tile_rows(i, x_ref, ctx_ref) + tab_ref[stream, 0:1, :] * mixed
    xo_ref[...] = xn

    h2 = xn * _rms_scale(xn) * tab_ref[stream, 1:2, :] + tab_ref[stream, 2:3, :]
    h2_ref[...] = _pack_bf16_pairs(h2)
    hi = h2.astype(bf16)
    lo = (h2 - hi.astype(f32)).astype(bf16)
    lg_ref[...] = (jnp.dot(hi, wr_ref[0], preferred_element_type=f32)
                   + jnp.dot(lo, wr_ref[0], preferred_element_type=f32)
                   + jnp.dot(hi, wr_ref[1], preferred_element_type=f32))


def _outproj(layer, attn, cu, b, xs, tab, gout, cw, w_out, wr):
    halo = TM // SUBLANES
    last8 = R // SUBLANES - 1
    row = lambda i: (i, 0)
    prev8 = lambda i: (jnp.maximum(i * halo - 1, 0), 0)
    next8 = lambda i: (jnp.minimum((i + 1) * halo, last8), 0)
    split = len(xs) == 2
    return pl.pallas_call(
        functools.partial(_outproj_kernel, layer, split),
        grid=(NT,),
        in_specs=[pl.BlockSpec((TM, ATTN_W), row),
                  pl.BlockSpec((TM, CONV_W), row),
                  pl.BlockSpec((TM, CONV_W), row),
                  pl.BlockSpec((SUBLANES, CONV_W), prev8),
                  pl.BlockSpec((SUBLANES, CONV_W), next8)]
                 + _row_specs(split)
                 + [pl.BlockSpec((2, 3, D), lambda i: (0, 0, 0)),
                    pl.BlockSpec((2, CONV_W), lambda i: (0, 0)),
                    pl.BlockSpec((3, CONV_W), lambda i: (0, 0)),
                    pl.BlockSpec(memory_space=pl.ANY),
                    pl.BlockSpec((2, D, LANES), lambda i: (0, 0, 0))],
        out_specs=[pl.BlockSpec((TM, D), row),
                   pl.BlockSpec((TM, DP), row),
                   pl.BlockSpec((TM, LANES), row)],
        out_shape=[jax.ShapeDtypeStruct((R, D), f32),
                   jax.ShapeDtypeStruct((R, DP), jnp.uint32),
                   jax.ShapeDtypeStruct((R, LANES), f32)],
        scratch_shapes=[pltpu.VMEM((D, D), bf16), pltpu.VMEM((2, WCH, D), f32), pltpu.SemaphoreType.DMA((2,))],
        compiler_params=_cparams(("arbitrary",)),
        name="outproj",
    )(attn, cu, b, cu, cu, *xs, tab, gout, cw, w_out, wr)


def _router_kernel(bias_ref, lt_ref, upper_ref, lower_ref, ones_ref, e_ref, g_ref, d_ref, cnt_ref):
    score = [1.0 / (1.0 + jnp.exp(-lt_ref[e])) for e in range(N_EXPERTS)]
    sel = [score[e] + bias_ref[e] for e in range(N_EXPERTS)]

    def top2_sum(a, b, c, d):
        p, q = jnp.maximum(a, b), jnp.minimum(a, b)
        r, s = jnp.maximum(c, d), jnp.minimum(c, d)
        return jnp.maximum(p, r) + jnp.maximum(jnp.minimum(p, r), jnp.maximum(q, s))

    gscore = [top2_sum(*sel[EPG * g:EPG * (g + 1)]) for g in range(N_GROUPS)]
    best, gidx = gscore[0], jnp.zeros(gscore[0].shape, i32)
    for g in range(1, N_GROUPS):
        take = gscore[g] > best
        best = jnp.where(take, gscore[g], best)
        gidx = jnp.where(take, g, gidx)

    def pick_group(vals):
        out = []
        for j in range(EPG):
            v = vals[j]
            for g in range(1, N_GROUPS):
                v = jnp.where(gidx == g, vals[EPG * g + j], v)
            out.append(v)
        return out

    in_sel = pick_group(sel)
    in_score = pick_group(score)

    def argmax_first(vals, excluded):
        bv, bi = None, None
        for j in range(EPG):
            v = vals[j] if excluded is None else jnp.where(excluded == j, -jnp.inf, vals[j])
            if bv is None:
                bv, bi = v, jnp.zeros(v.shape, i32)
            else:
                take = v > bv
                bv = jnp.where(take, v, bv)
                bi = jnp.where(take, j, bi)
        return bi

    i1 = argmax_first(in_sel, None)
    i2 = argmax_first(in_sel, i1)

    def pick_local(vals, idx):
        v = vals[0]
        for j in range(1, EPG):
            v = jnp.where(idx == j, vals[j], v)
        return v

    s1, s2 = pick_local(in_score, i1), pick_local(in_score, i2)
    tot = s1 + s2
    e1 = gidx * EPG + i1
    e2 = gidx * EPG + i2
    e_ref[0], e_ref[1] = e1, e2
    g_ref[0], g_ref[1] = s1 / tot, s2 / tot

    tok = (lax.broadcasted_iota(i32, (RT, LANES), 0) * LANES + lax.broadcasted_iota(i32, (RT, LANES), 1))
    valid = tok < R
    onehot = [(((e1 == e) | (e2 == e)) & valid).astype(f32) for e in range(N_EXPERTS)]
    stack = jnp.concatenate(onehot, axis=0).astype(bf16)
    within = jnp.dot(stack, upper_ref[...], preferred_element_type=f32)
    rowtot = jnp.dot(stack, ones_ref[...], preferred_element_type=f32)
    d1 = jnp.zeros((RT, LANES), i32)
    d2 = jnp.zeros((RT, LANES), i32)
    seg_start = jnp.zeros((1, LANES), i32)
    for e in range(N_EXPERTS):
        rt_e = rowtot[e * RT:(e + 1) * RT]
        before = jnp.dot(lower_ref[...], rt_e.astype(bf16), preferred_element_type=f32)
        rank_e = within[e * RT:(e + 1) * RT] + before
        count_e = (before + rt_e)[RT - 1:RT, :]
        cnt_ref[e:e + 1, :] = count_e
        slot = seg_start + rank_e.astype(i32)
        d1 = jnp.where(e1 == e, slot, d1)
        d2 = jnp.where(e2 == e, slot, d2)
        blocks_e = jnp.right_shift(count_e.astype(i32) + (EBLK - 1), EBLK_SHIFT)
        seg_start = seg_start + jnp.left_shift(blocks_e, EBLK_SHIFT)
    d_ref[0], d_ref[1] = d1, d2


def _router(bias, logits_t, upper, lower, ones):
    full = lambda *shape: pl.BlockSpec(shape, lambda: (0,) * len(shape))
    return pl.pallas_call(
        _router_kernel,
        in_specs=[pl.BlockSpec(memory_space=pltpu.SMEM),
                  full(N_EXPERTS, RT, LANES), full(LANES, LANES), full(RT, RT), full(LANES, LANES)],
        out_specs=[full(2, RT, LANES), full(2, RT, LANES), full(2, RT, LANES), full(N_EXPERTS, LANES)],
        out_shape=[jax.ShapeDtypeStruct((2, RT, LANES), i32),
                   jax.ShapeDtypeStruct((2, RT, LANES), f32),
                   jax.ShapeDtypeStruct((2, RT, LANES), i32),
                   jax.ShapeDtypeStruct((N_EXPERTS, LANES), f32)],
        compiler_params=pltpu.CompilerParams(vmem_limit_bytes=VMEM_LIMIT),
        name="router",
    )(bias, logits_t, upper, lower, ones)


ROW_UNROLL = TM


HSLOTS = 3


def _dispatch_kernel(pend_ref, dest_ref, h_hbm, xs_hbm, hbuf, zbuf, lsem, rsem, zsem):
    i = pl.program_id(0)
    last = pl.num_programs(0) - 1
    slot = lax.rem(i, HSLOTS)
    prev_slot = lax.rem(i + (HSLOTS - 1), HSLOTS)
    next_slot = lax.rem(i + 1, HSLOTS)

    def load(tile, s):
        rows = pl.ds(pl.multiple_of(tile * TM, TM), TM)
        return pltpu.make_async_copy(h_hbm.at[rows], hbuf.at[s], lsem.at[s])

    @pl.when(i == 0)
    def _():
        load(0, 0).start()
        zbuf[...] = jnp.zeros_like(zbuf)

        def zero_copy(e):
            first = pl.multiple_of(pend_ref[e] - EBLK, EBLK)
            return pltpu.make_async_copy(zbuf, xs_hbm.at[pl.ds(first, EBLK)], zsem)

        def nonempty(e):
            return pend_ref[e] > (pend_ref[e - 1] if e else 0)

        for e in range(N_EXPERTS):
            pl.when(nonempty(e))(lambda e=e: zero_copy(e).start())
        for e in range(N_EXPERTS):
            pl.when(nonempty(e))(lambda e=e: zero_copy(e).wait())

        def tail_copy(j):
            return pltpu.make_async_copy(zbuf, xs_hbm.at[pl.ds(pl.multiple_of(j * EBLK, EBLK), EBLK)], zsem)

        first_unused = jnp.right_shift(pend_ref[N_EXPERTS - 1], EBLK_SHIFT)
        lax.fori_loop(first_unused, NBLK, lambda j, c: (tail_copy(j).start(), c)[1], 0)
        lax.fori_loop(first_unused, NBLK, lambda j, c: (tail_copy(j).wait(), c)[1], 0)

    @pl.when(i < last)
    def _():
        load(i + 1, next_slot).start()

    load(i, slot).wait()

    def issue(s):
        def start(t, c):
            for k in range(2):
                pltpu.make_async_copy(hbuf.at[s, pl.ds(t, 1)], xs_hbm.at[pl.ds(dest_ref[0, 0, k * TM + t], 1)],
                                      rsem.at[s]).start(priority=k)
            return c
        lax.fori_loop(0, TM, start, 0, unroll=ROW_UNROLL)

    def drain(s):
        for k in range(2):
            pltpu.make_async_copy(hbuf.at[s], xs_hbm.at[pl.ds(0, TM)], rsem.at[s]).wait()

    for s in range(HSLOTS):
        pl.when(slot == s)(lambda s=s: issue(s))
    pl.when(i > 0)(lambda: drain(prev_slot))
    pl.when(i == last)(lambda: drain(slot))


def _dispatch(pend, dest3, h2):
    return pl.pallas_call(
        _dispatch_kernel,
        grid_spec=pltpu.PrefetchScalarGridSpec(
            num_scalar_prefetch=1,
            grid=(NT,),
            in_specs=[pl.BlockSpec((1, 1, 2 * TM), lambda i, pend: (i, 0, 0), memory_space=pltpu.SMEM),
                      pl.BlockSpec(memory_space=pl.ANY)],
            out_specs=pl.BlockSpec(memory_space=pl.ANY),
            scratch_shapes=[pltpu.VMEM((HSLOTS, TM, DP), jnp.uint32), pltpu.VMEM((EBLK, DP), jnp.uint32),
                            pltpu.SemaphoreType.DMA((HSLOTS,)), pltpu.SemaphoreType.DMA((HSLOTS,)),
                            pltpu.SemaphoreType.DMA(())]),
        out_shape=jax.ShapeDtypeStruct((NROWS, DP), jnp.uint32),
        compiler_params=_cparams(("arbitrary",)),
        name="dispatch",
    )(pend, dest3, h2)


NCH = 4
FCH = D_FF // NCH
KCH = D // NCH
NSTEPS = NBLK + (N_EXPERTS + 1) * NCH
(T_COMP, T_XBLK, T_CONV, T_CEXP, T_CCH, T_CSLOT, T_SLOT, T_ZERO, T_OBLK) = range(9)
T_STRIDE = 16


def _expert_schedule(counts):
    experts = jnp.arange(N_EXPERTS, dtype=i32)
    nb = (counts + EBLK - 1) // EBLK
    bstart = jnp.cumsum(nb) - nb
    nblocks = jnp.sum(nb)
    steps = jnp.maximum(nb, NCH)
    send = NCH + jnp.cumsum(steps)
    total = send[-1]
    s = jnp.arange(NSTEPS, dtype=i32)
    e_s = jnp.minimum(jnp.sum((s[:, None] >= send[None, :]).astype(i32), axis=1), N_EXPERTS - 1)
    onehot = (e_s[:, None] == experts[None, :]).astype(i32)
    pick = lambda table: jnp.sum(onehot * table[None, :], axis=1)
    i_s = s - pick(send - steps)
    nb_s = pick(nb)
    pre = s < NCH
    active = (s >= NCH) & (s < total)
    done = s >= total
    comp = active & (i_s < nb_s)
    xblk = jnp.clip(pick(bstart) + jnp.minimum(i_s, nb_s - 1), 0, nblocks - 1)
    xblk = jnp.where(pre, 0, jnp.where(done, nblocks - 1, xblk))
    has_next = e_s < N_EXPERTS - 1
    conv = pre | (active & (i_s < NCH) & has_next)
    cexp = jnp.where(pre, 0, jnp.minimum(e_s + 1, N_EXPERTS - 1))
    cch = jnp.where(pre, s, jnp.where(has_next & active, jnp.minimum(i_s, NCH - 1), NCH - 1))
    zblk = nblocks + (s - total)
    zero = done & (zblk < NBLK)
    oblk = jnp.where(done, jnp.minimum(zblk, NBLK - 1), xblk)
    cols = [comp, xblk, conv, cexp, cch, cexp % 2, e_s % 2, zero, oblk]
    cols = [col.astype(i32) for col in cols] + [jnp.zeros_like(s)] * (T_STRIDE - 9)
    return jnp.stack(cols, axis=1).reshape(NSTEPS * T_STRIDE)


def _expert_kernel(tab_ref, x_ref, g_ref, u_ref, d_ref, o_ref, wg_s, wu_s, wd_s):
    base = pl.program_id(0) * T_STRIDE

    @pl.when(tab_ref[base + T_CONV] == 1)
    def _():
        slot, c = tab_ref[base + T_CSLOT], tab_ref[base + T_CCH]
        wg_s[slot, c] = g_ref[0, 0].astype(bf16)
        wu_s[slot, c] = u_ref[0, 0].astype(bf16)
        wd_s[slot, c] = d_ref[0, 0].astype(bf16)

    @pl.when(tab_ref[base + T_COMP] == 1)
    def _():
        slot = tab_ref[base + T_SLOT]
        x = _unpack_bf16_pairs(x_ref[...])
        a = b = None
        for c in range(NCH):
            xc = x[:, c * KCH:(c + 1) * KCH]
            pa = jnp.dot(xc, wg_s[slot, c], preferred_element_type=f32)
            pb = jnp.dot(xc, wu_s[slot, c], preferred_element_type=f32)
            a = pa if a is None else a + pa
            b = pb if b is None else b + pb
        h = (a / (1.0 + jnp.exp(-a)) * b).astype(bf16)
        y = None
        for c in range(NCH):
            yc = jnp.dot(h[:, c * FCH:(c + 1) * FCH], wd_s[slot, c], preferred_element_type=f32)
            y = yc if y is None else y + yc
        o_ref[...] = _pack_bf16_pairs(y)

    @pl.when(tab_ref[base + T_ZERO] == 1)
    def _():
        o_ref[...] = jnp.zeros_like(o_ref)


def _experts(layer, schedule, xs, w_gate, w_up, w_down):
    at = lambda s, tab, col: tab[s * T_STRIDE + col]
    up_spec = pl.BlockSpec((1, 1, KCH, D_FF), lambda s, tab: (layer, at(s, tab, T_CEXP), at(s, tab, T_CCH), 0))
    return pl.pallas_call(
        _expert_kernel,
        grid_spec=pltpu.PrefetchScalarGridSpec(
            num_scalar_prefetch=1,
            grid=(NSTEPS,),
            in_specs=[pl.BlockSpec((EBLK, DP), lambda s, tab: (at(s, tab, T_XBLK), 0)),
                      up_spec, up_spec,
                      pl.BlockSpec((1, 1, FCH, D),
                                   lambda s, tab: (layer, at(s, tab, T_CEXP), at(s, tab, T_CCH), 0))],
            out_specs=pl.BlockSpec((EBLK, DP), lambda s, tab: (at(s, tab, T_OBLK), 0)),
            scratch_shapes=[pltpu.VMEM((2, NCH, KCH, D_FF), bf16), pltpu.VMEM((2, NCH, KCH, D_FF), bf16),
                            pltpu.VMEM((2, NCH, FCH, D), bf16)]),
        out_shape=jax.ShapeDtypeStruct((NROWS, DP), jnp.uint32),
        compiler_params=_cparams(("arbitrary",)),
        name="experts",
    )(schedule, xs, w_gate, w_up, w_down)


def _combine_kernel(first, dest_ref, dnext_ref, ys_hbm, x_ref, gates_ref, gate_ref, o_ref, ybuf, sems):
    i = pl.program_id(0)
    slot = jnp.bitwise_and(i, 1)
    stream = jnp.where(i + first == 0, 0, 1)

    def start_tile(idx_ref, s):
        def body(t, c):
            for k in range(2):
                pltpu.make_async_copy(ys_hbm.at[pl.ds(idx_ref[0, 0, k * TM + t], 1)],
                                      ybuf.at[s, k, pl.ds(t, 1)], sems.at[s]).start(priority=k)
            return c
        lax.fori_loop(0, TM, body, 0, unroll=ROW_UNROLL)

    @pl.when(i == 0)
    def _():
        start_tile(dest_ref, 0)

    has_next = i + 1 < pl.num_programs(0)
    for s in range(2):
        pl.when(has_next & (slot == 1 - s))(lambda s=s: start_tile(dnext_ref, s))

    for k in range(2):
        pltpu.make_async_copy(ys_hbm.at[pl.ds(0, TM)], ybuf.at[slot, k], sems.at[slot]).wait()
    y = (_unpack_pairs_f32(ybuf[slot, 0]) * gates_ref[:, 0:1]
         + _unpack_pairs_f32(ybuf[slot, 1]) * gates_ref[:, 1:2])
    o_ref[...] = x_ref[...] + gate_ref[stream] * y


def _combine(first, dest3, ys, x, gates, gate2):
    row = lambda i: (i + first, 0)
    return pl.pallas_call(
        functools.partial(_combine_kernel, first),
        grid=(NT - first,),
        in_specs=[pl.BlockSpec((1, 1, 2 * TM), lambda i: (i + first, 0, 0), memory_space=pltpu.SMEM),
                  pl.BlockSpec((1, 1, 2 * TM), lambda i: (jnp.minimum(i + first + 1, NT - 1), 0, 0),
                               memory_space=pltpu.SMEM),
                  pl.BlockSpec(memory_space=pl.ANY),
                  pl.BlockSpec((TM, D), row),
                  pl.BlockSpec((TM, 2), row),
                  pl.BlockSpec((2, 1, D), lambda i: (0, 0, 0))],
        out_specs=pl.BlockSpec((TM, D), lambda i: (i, 0)),
        out_shape=jax.ShapeDtypeStruct((R - first * TM, D), f32),
        scratch_shapes=[pltpu.VMEM((2, 2, TM, DP), jnp.uint32), pltpu.SemaphoreType.DMA((2,))],
        compiler_params=_cparams(("arbitrary",)),
        name="combine",
    )(dest3, dest3, ys, x, gates, gate2)


def _rope_tables():
    half = HEAD_DIM // 2
    inv_freq = (ROPE_THETA ** (-np.arange(0, half, 2, dtype=np.float32) / half)).astype(np.float32)
    t = np.arange(SEQ)
    pos = np.stack([(t // GRID_W).astype(np.float32), (t % GRID_W).astype(np.float32)], axis=1)
    d = np.arange(LANES) % HEAD_DIM
    axis, sub = d // half, d % half
    ang = (pos[:, axis] * inv_freq[sub % (half // 2)][None, :]).astype(np.float32)
    cos, sin = np.cos(ang).astype(np.float32), np.sin(ang).astype(np.float32)
    first = (sub < half // 2)[None, :]
    sa = np.where(first, -sin, 0.0).astype(np.float32)
    sb = np.where(first, 0.0, sin).astype(np.float32)
    ident = np.ones((CTX, LANES), np.float32)
    zeros = np.zeros((CTX, LANES), np.float32)
    return tuple(jnp.asarray(np.concatenate(parts)) for parts in ((ident, cos), (zeros, sa), (zeros, sb)))


def kernel(x, c, ctx, c_ctx, w_ada, b_ada, g_attn, w_in, q_norm_g, k_norm_g, sink, conv_w, g_out_attn, g_out_conv, w_out, g_ffn, w_router, router_bias, w_exp_gate, w_exp_up, w_exp_down):
    assert x.shape == (1, SEQ, D) and ctx.shape == (1, CTX, D)
    xs_in = (x[0], ctx[0])

    silu = lambda t: t * jax.nn.sigmoid(t)
    s_vec = jnp.stack([silu(c_ctx), silu(c[0])])
    mods = _adaln(jnp.broadcast_to(s_vec[:, :, None], (2, D, LANES)), w_ada, b_ada).reshape(DEPTH, 2, 6, D)

    cos, sa, sb = _rope_tables()
    lane = jnp.arange(LANES)
    ones_bd = (lane[:, None] // HEAD_DIM == lane[None, :] // HEAD_DIM).astype(bf16)
    upper = (lane[:, None] < lane[None, :]).astype(bf16)
    ones = jnp.ones((LANES, LANES), bf16)
    rt = jnp.arange(RT)
    lower = (rt[None, :] < rt[:, None]).astype(bf16)
    wr_pad = jnp.pad(w_router, ((0, 0), (0, LANES - N_EXPERTS)))
    wr_hi = wr_pad.astype(bf16)
    wr = jnp.stack([wr_hi, (wr_pad - wr_hi.astype(f32)).astype(bf16)])

    for l in range(DEPTH):
        mod = mods[l]
        ms1 = jnp.stack([g_attn[l] * (1.0 + mod[:, 1]), mod[:, 0]], axis=1)
        g2 = jnp.stack([jnp.tile(q_norm_g[l] * HEAD_DIM ** -0.5, 2), jnp.tile(k_norm_g[l], 2)])
        q, k, v, cu, b_gate = _inproj(l, xs_in, ms1, w_in, cos, sa, sb, g2, ones_bd)
        attn = _attention(sink[l], q, k, v)
        tab = jnp.stack([mod[:, 2], g_ffn[l] * (1.0 + mod[:, 4]), mod[:, 3]], axis=1)
        gout = jnp.stack([g_out_attn[l], g_out_conv[l]])
        xa, h2, logits = _outproj(l, attn, cu, b_gate, xs_in, tab, gout, conv_w[l], w_out, wr)

        lt = jnp.pad(logits[:, :N_EXPERTS].T, ((0, 0), (0, RT * LANES - R))).reshape(N_EXPERTS, RT, LANES)
        _, g_t, d_t, cnt = _router(router_bias, lt, upper, lower, ones)
        gates = g_t.reshape(2, RT * LANES)[:, :R].T
        dest = d_t.reshape(2, RT * LANES)[:, :R]
        dest3 = dest.reshape(2, NT, TM).transpose(1, 0, 2).reshape(NT, 1, 2 * TM)
        counts = cnt[:, 0].astype(i32)
        pend = jnp.cumsum((counts + EBLK - 1) // EBLK * EBLK)

        xs = _dispatch(pend, dest3, h2)
        ys = _experts(l, _expert_schedule(counts), xs, w_exp_gate, w_exp_up, w_exp_down)
        xa = _combine(1 if l == DEPTH - 1 else 0, dest3, ys, xa, gates, mod[:, 5][:, None, :])
        xs_in = (xa,)

    return xa[None]
```

```python
import functools

import jax
import numpy as np
import jax.numpy as jnp
from jax import lax
from jax.experimental import pallas as pl
from jax.experimental.pallas import tpu as pltpu

f32 = jnp.float32
bf16 = jnp.bfloat16
i32 = jnp.int32

D = 2048
DEPTH = 4
SEQ = 8192
CTX = 256
R = CTX + SEQ
GRID_W = 64
ATTN_W = 1024
CONV_W = 1024
HEAD_DIM = 64
N_HEADS = 16
KV_W = 128
IN_COLS = ATTN_W + 2 * KV_W + 3 * CONV_W
ABLK = 128
WINDOW = 128
N_EXPERTS = 16
N_GROUPS = 4
EPG = 4
D_FF = 1024
EPS = 1e-6
NEG_INF = -1e30
ROPE_THETA = 10000.0

LANES = 128
SUBLANES = 8
TM = 256
NT = R // TM
EBLK = 256
EBLK_SHIFT = EBLK.bit_length() - 1
assert 1 << EBLK_SHIFT == EBLK
NBLK = -(-(2 * R + N_EXPERTS * (EBLK - 1)) // EBLK)
NROWS = NBLK * EBLK
RT = LANES
VMEM_LIMIT = 56 * 1024 * 1024


def _cparams(sem):
    return pltpu.CompilerParams(dimension_semantics=sem, vmem_limit_bytes=VMEM_LIMIT)


ADA_TK = 2048
ADA_TN = 1536


def _adaln_kernel(s_ref, w_ref, b_ref, o_ref, acc_ref):
    k = pl.program_id(2)

    @pl.when(k == 0)
    def _():
        acc_ref[...] = jnp.zeros_like(acc_ref)

    for j in range(ADA_TN // LANES):
        w = w_ref[0, :, j * LANES:(j + 1) * LANES]
        for v in range(2):
            p = (w * s_ref[v]).reshape(ADA_TK // SUBLANES, SUBLANES, LANES).sum(axis=0)
            acc_ref[v, :, j * LANES:(j + 1) * LANES] += p

    @pl.when(k == pl.num_programs(2) - 1)
    def _():
        for v in range(2):
            o_ref[0, v:v + 1, :] = acc_ref[v].sum(axis=0, keepdims=True) + b_ref[0]


def _adaln(s_bcast, w_ada, b_ada):
    return pl.pallas_call(
        _adaln_kernel,
        grid=(DEPTH, 6 * D // ADA_TN, D // ADA_TK),
        in_specs=[pl.BlockSpec((2, ADA_TK, LANES), lambda l, n, k: (0, k, 0)),
                  pl.BlockSpec((1, ADA_TK, ADA_TN), lambda l, n, k: (l, k, n)),
                  pl.BlockSpec((1, 1, ADA_TN), lambda l, n, k: (l, 0, n))],
        out_specs=pl.BlockSpec((1, 2, ADA_TN), lambda l, n, k: (l, 0, n)),
        out_shape=jax.ShapeDtypeStruct((DEPTH, 2, 6 * D), f32),
        scratch_shapes=[pltpu.VMEM((2, SUBLANES, ADA_TN), f32)],
        compiler_params=_cparams(("arbitrary", "arbitrary", "arbitrary")),
        name="adaln",
    )(s_bcast, w_ada, b_ada.reshape(DEPTH, 1, 6 * D))


def _rms_scale(x):
    return lax.rsqrt(jnp.mean(x * x, axis=-1, keepdims=True) + EPS)


WCH = 256


def _load_weight_bf16(w_hbm, wbuf, stage, sems):
    n = wbuf.shape[0] // WCH

    def chunk(c):
        return pltpu.make_async_copy(w_hbm.at[pl.ds(c * WCH, WCH)], stage.at[c % 2], sems.at[c % 2])

    chunk(0).start()
    for c in range(n):
        if c + 1 < n:
            chunk(c + 1).start()
        chunk(c).wait()
        wbuf[pl.ds(c * WCH, WCH), :] = stage[c % 2].astype(bf16)


Q0, K0, V0, U0, B0, C0 = 0, ATTN_W, ATTN_W + KV_W, ATTN_W + 2 * KV_W, ATTN_W + 2 * KV_W + CONV_W, \
    ATTN_W + 2 * KV_W + 2 * CONV_W


def _tile_rows(i, x_ref, ctx_ref):
    if ctx_ref is None:
        return x_ref[...]
    return jnp.where(i == 0, ctx_ref[...], x_ref[...])


def _row_specs(split):
    if split:
        return [pl.BlockSpec((TM, D), lambda i: (jnp.maximum(i - 1, 0), 0)), pl.BlockSpec((TM, D), lambda i: (0, 0))]
    return [pl.BlockSpec((TM, D), lambda i: (i, 0))]


def _inproj_kernel(layer, split, x_ref, *refs):
    ctx_ref, refs = (refs[0], refs[1:]) if split else (None, refs)
    (ms_ref, cos_ref, sa_ref, sb_ref, g_ref, ones_ref, w_hbm,
     q_ref, k_ref, v_ref, cu_ref, b_ref, wbuf, stage, sems) = refs

    @pl.when(pl.program_id(0) == 0)
    def _():
        _load_weight_bf16(w_hbm.at[layer], wbuf, stage, sems)

    stream = jnp.where(pl.program_id(0) == 0, 0, 1)
    x = _tile_rows(pl.program_id(0), x_ref, ctx_ref)
    gain = ms_ref[stream, 0:1, :]
    shift = ms_ref[stream, 1:2, :]
    h = (x * _rms_scale(x) * gain + shift).astype(bf16)
    p = jnp.dot(h, wbuf[...], preferred_element_type=f32)
    cu_ref[...] = p[:, C0:] * p[:, U0:B0]
    b_ref[...] = p[:, B0:C0]

    cos, sa, sb = cos_ref[...], sa_ref[...], sb_ref[...]
    ones = ones_ref[...]

    def norm_rope(xc, g):
        sq = xc * xc
        hi = sq.astype(bf16)
        lo = (sq - hi.astype(f32)).astype(bf16)
        ss = (jnp.dot(hi, ones, preferred_element_type=f32)
              + jnp.dot(lo, ones, preferred_element_type=f32))
        y = xc * lax.rsqrt(ss * (1.0 / HEAD_DIM) + EPS) * g
        return y * cos + pltpu.roll(y, LANES - 16, 1) * sa + pltpu.roll(y, 16, 1) * sb

    for j in range(ATTN_W // LANES):
        sl = slice(j * LANES, (j + 1) * LANES)
        q_ref[:, sl] = norm_rope(p[:, sl], g_ref[0:1, :]).astype(bf16)
    k_ref[...] = norm_rope(p[:, K0:V0], g_ref[1:2, :]).astype(bf16)
    v_ref[...] = p[:, V0:U0].astype(bf16)


def _inproj(layer, xs, ms, w_in, cos, sa, sb, g2, ones_bd):
    row = lambda i: (i, 0)
    outs = ((ATTN_W, bf16), (KV_W, bf16), (KV_W, bf16), (CONV_W, f32), (CONV_W, f32))
    split = len(xs) == 2
    return pl.pallas_call(
        functools.partial(_inproj_kernel, layer, split),
        grid=(NT,),
        in_specs=_row_specs(split) + [pl.BlockSpec((2, 2, D), lambda i: (0, 0, 0)),
                                      pl.BlockSpec((TM, LANES), row),
                                      pl.BlockSpec((TM, LANES), row),
                                      pl.BlockSpec((TM, LANES), row),
                                      pl.BlockSpec((2, LANES), lambda i: (0, 0)),
                                      pl.BlockSpec((LANES, LANES), lambda i: (0, 0)),
                                      pl.BlockSpec(memory_space=pl.ANY)],
        out_specs=[pl.BlockSpec((TM, n), row) for n, _ in outs],
        out_shape=[jax.ShapeDtypeStruct((R, n), dt) for n, dt in outs],
        scratch_shapes=[pltpu.VMEM((D, IN_COLS), bf16), pltpu.VMEM((2, WCH, IN_COLS), f32),
                        pltpu.SemaphoreType.DMA((2,))],
        compiler_params=_cparams(("arbitrary",)),
        name="inproj",
    )(*xs, ms, cos, sa, sb, g2, ones_bd, w_in)


NQB = R // ABLK
FIRST_LAT = CTX // ABLK
NKEYS = CTX + 3 * ABLK


def _attn_kernel(sink_ref, q_ref, kc_ref, kp_ref, kk_ref, kn_ref, vc_ref, vp_ref, vk_ref, vn_ref, o_ref):
    i = pl.program_id(0)
    kf = jnp.concatenate([kc_ref[...], kp_ref[...], kk_ref[...], kn_ref[...]], axis=0).astype(f32)
    vf = jnp.concatenate([vc_ref[...], vp_ref[...], vk_ref[...], vn_ref[...]], axis=0).astype(f32)
    lane = lax.broadcasted_iota(i32, (NKEYS, LANES), 1)
    low = lane < HEAD_DIM
    kr = pltpu.roll(kf, HEAD_DIM, 1)
    vr = pltpu.roll(vf, HEAD_DIM, 1)
    zero = jnp.zeros_like(kf)
    k_lo = (jnp.where(low, kf, zero), jnp.where(low, kr, zero))
    k_hi = (jnp.where(low, zero, kr), jnp.where(low, zero, kf))
    v_lo = (jnp.where(low, vf, zero), jnp.where(low, vr, zero))
    v_hi = (jnp.where(low, zero, vr), jnp.where(low, zero, vf))

    qi = lax.broadcasted_iota(i32, (ABLK, NKEYS), 0)
    col = lax.broadcasted_iota(i32, (ABLK, NKEYS), 1)
    kj = col - CTX
    in_window = jnp.abs(qi + ABLK - kj) <= WINDOW
    key_block = i - 1 + jnp.right_shift(kj, 7)
    first_ok = jnp.where(i >= FIRST_LAT, FIRST_LAT, NQB)
    mask = (col < CTX) | (in_window & (key_block >= first_ok) & (key_block < NQB))
    lane_q = lax.broadcasted_iota(i32, (ABLK, LANES), 1)

    contract_last = (((1,), (1,)), ((), ()))
    for h in range(2):
        qs = jnp.concatenate([q_ref[:, (h * 4 + p) * LANES:(h * 4 + p + 1) * LANES] for p in range(4)], axis=0)
        s_par = (lax.dot_general(qs, k_lo[h].astype(bf16), contract_last, preferred_element_type=f32),
                 lax.dot_general(qs, k_hi[h].astype(bf16), contract_last, preferred_element_type=f32))
        probs = ([], [])
        rden = ([], [])
        for par in range(2):
            for p in range(4):
                sink = sink_ref[h * 8 + 2 * p + par]
                s = jnp.where(mask, s_par[par][p * ABLK:(p + 1) * ABLK], NEG_INF)
                m = jnp.maximum(jnp.max(s, axis=-1, keepdims=True), sink)
                e = jnp.exp(s - m)
                den = jnp.sum(e, axis=-1, keepdims=True) + jnp.exp(sink - m)
                probs[par].append(e.astype(bf16))
                rden[par].append(1.0 / den)
        o = (jnp.dot(jnp.concatenate(probs[0], axis=0), v_lo[h].astype(bf16), preferred_element_type=f32)
             + jnp.dot(jnp.concatenate(probs[1], axis=0), v_hi[h].astype(bf16), preferred_element_type=f32))
        for p in range(4):
            scale = jnp.where(lane_q < HEAD_DIM, rden[0][p], rden[1][p])
            o_ref[:, (h * 4 + p) * LANES:(h * 4 + p + 1) * LANES] = o[p * ABLK:(p + 1) * ABLK] * scale


def _attention(sink, q, k, v):
    clip = lambda b: jnp.clip(b, FIRST_LAT, NQB - 1)
    kv_specs = [pl.BlockSpec((CTX, KV_W), lambda i: (0, 0)),
                pl.BlockSpec((ABLK, KV_W), lambda i: (clip(i - 1), 0)),
                pl.BlockSpec((ABLK, KV_W), lambda i: (i, 0)),
                pl.BlockSpec((ABLK, KV_W), lambda i: (clip(i + 1), 0))]
    return pl.pallas_call(
        _attn_kernel,
        grid=(NQB,),
        in_specs=[pl.BlockSpec(memory_space=pltpu.SMEM),
                  pl.BlockSpec((ABLK, ATTN_W), lambda i: (i, 0))] + kv_specs + kv_specs,
        out_specs=pl.BlockSpec((ABLK, ATTN_W), lambda i: (i, 0)),
        out_shape=jax.ShapeDtypeStruct((R, ATTN_W), f32),
        compiler_params=_cparams(("arbitrary",)),
        name="attention",
    )(sink, q, k, k, k, k, v, v, v, v)


DP = D // 2
HIGH_HALF = 0xFFFF0000


def _pack_bf16_pairs(t):
    lo = lax.bitcast_convert_type(t[:, :DP].astype(bf16).astype(f32), jnp.uint32)
    hi = lax.bitcast_convert_type(t[:, DP:].astype(bf16).astype(f32), jnp.uint32)
    return jnp.right_shift(lo, jnp.uint32(16)) | (hi & jnp.uint32(HIGH_HALF))


def _unpack_pairs_f32(w):
    lo = lax.bitcast_convert_type(jnp.left_shift(w, jnp.uint32(16)), f32)
    hi = lax.bitcast_convert_type(w & jnp.uint32(HIGH_HALF), f32)
    return jnp.concatenate([lo, hi], axis=1)


def _unpack_bf16_pairs(w):
    lo = lax.bitcast_convert_type(jnp.left_shift(w, jnp.uint32(16)), f32)
    hi = lax.bitcast_convert_type(w & jnp.uint32(HIGH_HALF), f32)
    return jnp.concatenate([lo.astype(bf16), hi.astype(bf16)], axis=1)


def _outproj_kernel(layer, split, attn_ref, cu_ref, b_ref, cup_ref, cun_ref, x_ref, *refs):
    ctx_ref, refs = (refs[0], refs[1:]) if split else (None, refs)
    tab_ref, gout_ref, cw_ref, w_hbm, wr_ref, xo_ref, h2_ref, lg_ref, wbuf, stage, sems = refs
    i = pl.program_id(0)

    @pl.when(i == 0)
    def _():
        _load_weight_bf16(w_hbm.at[layer], wbuf, stage, sems)

    stream = jnp.where(i == 0, 0, 1)
    prev_ok = (i >= 2).astype(f32)
    next_ok = ((i >= 1) & (i < pl.num_programs(0) - 1)).astype(f32)
    w = cu_ref[...]
    w_before = cup_ref[SUBLANES - 1:SUBLANES, :] * prev_ok
    w_after = cun_ref[0:1, :] * next_ok
    rows = lax.broadcasted_iota(i32, (TM, 1), 0)
    w_prev = jnp.where(rows == 0, w_before, pltpu.roll(w, 1, 0))
    w_next = jnp.where(rows == TM - 1, w_after, pltpu.roll(w, TM - 1, 0))
    conv = b_ref[...] * (cw_ref[0:1, :] * w_prev + cw_ref[1:2, :] * w + cw_ref[2:3, :] * w_next)

    attn = attn_ref[...]
    a_n = attn * _rms_scale(attn) * gout_ref[0:1, :]
    c_n = conv * _rms_scale(conv) * gout_ref[1:2, :]
    cat = jnp.concatenate([a_n, c_n], axis=1).astype(bf16)
    mixed = jnp.dot(cat, wbuf[...], preferred_element_type=f32)
    xn = _tile_rows(i, x_ref, ctx_ref) + tab_ref[stream, 0:1, :] * mixed
    xo_ref[...] = xn

    h2 = xn * _rms_scale(xn) * tab_ref[stream, 1:2, :] + tab_ref[stream, 2:3, :]
    h2_ref[...] = _pack_bf16_pairs(h2)
    hi = h2.astype(bf16)
    lo = (h2 - hi.astype(f32)).astype(bf16)
    logits = (jnp.dot(hi, wr_ref[0], preferred_element_type=f32)
              + jnp.dot(lo, wr_ref[0], preferred_element_type=f32)
              + jnp.dot(hi, wr_ref[1], preferred_element_type=f32))
    lg_ref[...] = logits.T


def _outproj(layer, attn, cu, b, xs, tab, gout, cw, w_out, wr):
    halo = TM // SUBLANES
    last8 = R // SUBLANES - 1
    row = lambda i: (i, 0)
    prev8 = lambda i: (jnp.maximum(i * halo - 1, 0), 0)
    next8 = lambda i: (jnp.minimum((i + 1) * halo, last8), 0)
    split = len(xs) == 2
    return pl.pallas_call(
        functools.partial(_outproj_kernel, layer, split),
        grid=(NT,),
        in_specs=[pl.BlockSpec((TM, ATTN_W), row),
                  pl.BlockSpec((TM, CONV_W), row),
                  pl.BlockSpec((TM, CONV_W), row),
                  pl.BlockSpec((SUBLANES, CONV_W), prev8),
                  pl.BlockSpec((SUBLANES, CONV_W), next8)]
                 + _row_specs(split)
                 + [pl.BlockSpec((2, 3, D), lambda i: (0, 0, 0)),
                    pl.BlockSpec((2, CONV_W), lambda i: (0, 0)),
                    pl.BlockSpec((3, CONV_W), lambda i: (0, 0)),
                    pl.BlockSpec(memory_space=pl.ANY),
                    pl.BlockSpec((2, D, LANES), lambda i: (0, 0, 0))],
        out_specs=[pl.BlockSpec((TM, D), row),
                   pl.BlockSpec((TM, DP), row),
                   pl.BlockSpec((LANES, TM), lambda i: (0, i))],
        out_shape=[jax.ShapeDtypeStruct((R, D), f32),
                   jax.ShapeDtypeStruct((R, DP), jnp.uint32),
                   jax.ShapeDtypeStruct((LANES, R), f32)],
        scratch_shapes=[pltpu.VMEM((D, D), bf16), pltpu.VMEM((2, WCH, D), f32), pltpu.SemaphoreType.DMA((2,))],
        compiler_params=_cparams(("arbitrary",)),
        name="outproj",
    )(attn, cu, b, cu, cu, *xs, tab, gout, cw, w_out, wr)


def _router_kernel(bias_ref, lt_ref, upper_ref, lower_ref, ones_ref, e_ref, g_ref, d_ref, cnt_ref):
    score = [1.0 / (1.0 + jnp.exp(-lt_ref[e])) for e in range(N_EXPERTS)]
    sel = [score[e] + bias_ref[e] for e in range(N_EXPERTS)]

    def top2_sum(a, b, c, d):
        p, q = jnp.maximum(a, b), jnp.minimum(a, b)
        r, s = jnp.maximum(c, d), jnp.minimum(c, d)
        return jnp.maximum(p, r) + jnp.maximum(jnp.minimum(p, r), jnp.maximum(q, s))

    gscore = [top2_sum(*sel[EPG * g:EPG * (g + 1)]) for g in range(N_GROUPS)]
    best, gidx = gscore[0], jnp.zeros(gscore[0].shape, i32)
    for g in range(1, N_GROUPS):
        take = gscore[g] > best
        best = jnp.where(take, gscore[g], best)
        gidx = jnp.where(take, g, gidx)

    def pick_group(vals):
        out = []
        for j in range(EPG):
            v = vals[j]
            for g in range(1, N_GROUPS):
                v = jnp.where(gidx == g, vals[EPG * g + j], v)
            out.append(v)
        return out

    in_sel = pick_group(sel)
    in_score = pick_group(score)

    def argmax_first(vals, excluded):
        bv, bi = None, None
        for j in range(EPG):
            v = vals[j] if excluded is None else jnp.where(excluded == j, -jnp.inf, vals[j])
            if bv is None:
                bv, bi = v, jnp.zeros(v.shape, i32)
            else:
                take = v > bv
                bv = jnp.where(take, v, bv)
                bi = jnp.where(take, j, bi)
        return bi

    i1 = argmax_first(in_sel, None)
    i2 = argmax_first(in_sel, i1)

    def pick_local(vals, idx):
        v = vals[0]
        for j in range(1, EPG):
            v = jnp.where(idx == j, vals[j], v)
        return v

    s1, s2 = pick_local(in_score, i1), pick_local(in_score, i2)
    tot = s1 + s2
    e1 = gidx * EPG + i1
    e2 = gidx * EPG + i2
    e_ref[0], e_ref[1] = e1, e2
    g_ref[0], g_ref[1] = s1 / tot, s2 / tot

    tok = (lax.broadcasted_iota(i32, (RT, LANES), 0) * LANES + lax.broadcasted_iota(i32, (RT, LANES), 1))
    valid = tok < R
    onehot = [(((e1 == e) | (e2 == e)) & valid).astype(f32) for e in range(N_EXPERTS)]
    stack = jnp.concatenate(onehot, axis=0).astype(bf16)
    within = jnp.dot(stack, upper_ref[...], preferred_element_type=f32)
    rowtot = jnp.dot(stack, ones_ref[...], preferred_element_type=f32)
    d1 = jnp.zeros((RT, LANES), i32)
    d2 = jnp.zeros((RT, LANES), i32)
    seg_start = jnp.zeros((1, LANES), i32)
    for e in range(N_EXPERTS):
        rt_e = rowtot[e * RT:(e + 1) * RT]
        before = jnp.dot(lower_ref[...], rt_e.astype(bf16), preferred_element_type=f32)
        rank_e = within[e * RT:(e + 1) * RT] + before
        count_e = (before + rt_e)[RT - 1:RT, :]
        cnt_ref[e:e + 1, :] = count_e
        slot = seg_start + rank_e.astype(i32)
        d1 = jnp.where(e1 == e, slot, d1)
        d2 = jnp.where(e2 == e, slot, d2)
        blocks_e = jnp.right_shift(count_e.astype(i32) + (EBLK - 1), EBLK_SHIFT)
        seg_start = seg_start + jnp.left_shift(blocks_e, EBLK_SHIFT)
    d_ref[0], d_ref[1] = d1, d2


def _router(bias, logits_t, upper, lower, ones):
    full = lambda *shape: pl.BlockSpec(shape, lambda: (0,) * len(shape))
    return pl.pallas_call(
        _router_kernel,
        in_specs=[pl.BlockSpec(memory_space=pltpu.SMEM),
                  full(N_EXPERTS, RT, LANES), full(LANES, LANES), full(RT, RT), full(LANES, LANES)],
        out_specs=[full(2, RT, LANES), full(2, RT, LANES), full(2, RT, LANES), full(N_EXPERTS, LANES)],
        out_shape=[jax.ShapeDtypeStruct((2, RT, LANES), i32),
                   jax.ShapeDtypeStruct((2, RT, LANES), f32),
                   jax.ShapeDtypeStruct((2, RT, LANES), i32),
                   jax.ShapeDtypeStruct((N_EXPERTS, LANES), f32)],
        compiler_params=pltpu.CompilerParams(vmem_limit_bytes=VMEM_LIMIT),
        name="router",
    )(bias, logits_t, upper, lower, ones)


ROW_UNROLL = TM


HSLOTS = 3


def _dispatch_kernel(pend_ref, dest_ref, h_hbm, xs_hbm, hbuf, zbuf, lsem, rsem, zsem):
    i = pl.program_id(0)
    last = pl.num_programs(0) - 1
    slot = lax.rem(i, HSLOTS)
    prev_slot = lax.rem(i + (HSLOTS - 1), HSLOTS)
    next_slot = lax.rem(i + 1, HSLOTS)

    def load(tile, s):
        rows = pl.ds(pl.multiple_of(tile * TM, TM), TM)
        return pltpu.make_async_copy(h_hbm.at[rows], hbuf.at[s], lsem.at[s])

    @pl.when(i == 0)
    def _():
        load(0, 0).start()
        zbuf[...] = jnp.zeros_like(zbuf)

        def zero_copy(e):
            first = pl.multiple_of(pend_ref[e] - EBLK, EBLK)
            return pltpu.make_async_copy(zbuf, xs_hbm.at[pl.ds(first, EBLK)], zsem)

        def nonempty(e):
            return pend_ref[e] > (pend_ref[e - 1] if e else 0)

        for e in range(N_EXPERTS):
            pl.when(nonempty(e))(lambda e=e: zero_copy(e).start())
        for e in range(N_EXPERTS):
            pl.when(nonempty(e))(lambda e=e: zero_copy(e).wait())

        def tail_copy(j):
            return pltpu.make_async_copy(zbuf, xs_hbm.at[pl.ds(pl.multiple_of(j * EBLK, EBLK), EBLK)], zsem)

        first_unused = jnp.right_shift(pend_ref[N_EXPERTS - 1], EBLK_SHIFT)
        lax.fori_loop(first_unused, NBLK, lambda j, c: (tail_copy(j).start(), c)[1], 0)
        lax.fori_loop(first_unused, NBLK, lambda j, c: (tail_copy(j).wait(), c)[1], 0)

    @pl.when(i < last)
    def _():
        load(i + 1, next_slot).start()

    load(i, slot).wait()

    def issue(s):
        def start(t, c):
            for k in range(2):
                pltpu.make_async_copy(hbuf.at[s, pl.ds(t, 1)], xs_hbm.at[pl.ds(dest_ref[0, 0, k * TM + t], 1)],
                                      rsem.at[s]).start(priority=k)
            return c
        lax.fori_loop(0, TM, start, 0, unroll=ROW_UNROLL)

    def drain(s):
        for k in range(2):
            pltpu.make_async_copy(hbuf.at[s], xs_hbm.at[pl.ds(0, TM)], rsem.at[s]).wait()

    for s in range(HSLOTS):
        pl.when(slot == s)(lambda s=s: issue(s))
    pl.when(i > 0)(lambda: drain(prev_slot))
    pl.when(i == last)(lambda: drain(slot))


def _dispatch(pend, dest3, h2):
    return pl.pallas_call(
        _dispatch_kernel,
        grid_spec=pltpu.PrefetchScalarGridSpec(
            num_scalar_prefetch=1,
            grid=(NT,),
            in_specs=[pl.BlockSpec((1, 1, 2 * TM), lambda i, pend: (i, 0, 0), memory_space=pltpu.SMEM),
                      pl.BlockSpec(memory_space=pl.ANY)],
            out_specs=pl.BlockSpec(memory_space=pl.ANY),
            scratch_shapes=[pltpu.VMEM((HSLOTS, TM, DP), jnp.uint32), pltpu.VMEM((EBLK, DP), jnp.uint32),
                            pltpu.SemaphoreType.DMA((HSLOTS,)), pltpu.SemaphoreType.DMA((HSLOTS,)),
                            pltpu.SemaphoreType.DMA(())]),
        out_shape=jax.ShapeDtypeStruct((NROWS, DP), jnp.uint32),
        compiler_params=_cparams(("arbitrary",)),
        name="dispatch",
    )(pend, dest3, h2)


NCH = 4
FCH = D_FF // NCH
KCH = D // NCH
NSTEPS = NBLK + (N_EXPERTS + 1) * NCH
(T_COMP, T_XBLK, T_CONV, T_CEXP, T_CCH, T_CSLOT, T_SLOT, T_ZERO, T_OBLK) = range(9)
T_STRIDE = 16


def _expert_schedule(counts):
    experts = jnp.arange(N_EXPERTS, dtype=i32)
    nb = (counts + EBLK - 1) // EBLK
    bstart = jnp.cumsum(nb) - nb
    nblocks = jnp.sum(nb)
    steps = jnp.maximum(nb, NCH)
    send = NCH + jnp.cumsum(steps)
    total = send[-1]
    s = jnp.arange(NSTEPS, dtype=i32)
    e_s = jnp.minimum(jnp.sum((s[:, None] >= send[None, :]).astype(i32), axis=1), N_EXPERTS - 1)
    onehot = (e_s[:, None] == experts[None, :]).astype(i32)
    pick = lambda table: jnp.sum(onehot * table[None, :], axis=1)
    i_s = s - pick(send - steps)
    nb_s = pick(nb)
    pre = s < NCH
    active = (s >= NCH) & (s < total)
    done = s >= total
    comp = active & (i_s < nb_s)
    xblk = jnp.clip(pick(bstart) + jnp.minimum(i_s, nb_s - 1), 0, nblocks - 1)
    xblk = jnp.where(pre, 0, jnp.where(done, nblocks - 1, xblk))
    has_next = e_s < N_EXPERTS - 1
    conv = pre | (active & (i_s < NCH) & has_next)
    cexp = jnp.where(pre, 0, jnp.minimum(e_s + 1, N_EXPERTS - 1))
    cch = jnp.where(pre, s, jnp.where(has_next & active, jnp.minimum(i_s, NCH - 1), NCH - 1))
    zblk = nblocks + (s - total)
    zero = done & (zblk < NBLK)
    oblk = jnp.where(done, jnp.minimum(zblk, NBLK - 1), xblk)
    cols = [comp, xblk, conv, cexp, cch, cexp % 2, e_s % 2, zero, oblk]
    cols = [col.astype(i32) for col in cols] + [jnp.zeros_like(s)] * (T_STRIDE - 9)
    return jnp.stack(cols, axis=1).reshape(NSTEPS * T_STRIDE)


def _expert_kernel(tab_ref, x_ref, g_ref, u_ref, d_ref, o_ref, wg_s, wu_s, wd_s):
    base = pl.program_id(0) * T_STRIDE

    @pl.when(tab_ref[base + T_CONV] == 1)
    def _():
        slot, c = tab_ref[base + T_CSLOT], tab_ref[base + T_CCH]
        wg_s[slot, c] = g_ref[0, 0].astype(bf16)
        wu_s[slot, c] = u_ref[0, 0].astype(bf16)
        wd_s[slot, c] = d_ref[0, 0].astype(bf16)

    @pl.when(tab_ref[base + T_COMP] == 1)
    def _():
        slot = tab_ref[base + T_SLOT]
        x = _unpack_bf16_pairs(x_ref[...])
        a = b = None
        for c in range(NCH):
            xc = x[:, c * KCH:(c + 1) * KCH]
            pa = jnp.dot(xc, wg_s[slot, c], preferred_element_type=f32)
            pb = jnp.dot(xc, wu_s[slot, c], preferred_element_type=f32)
            a = pa if a is None else a + pa
            b = pb if b is None else b + pb
        h = (a / (1.0 + jnp.exp(-a)) * b).astype(bf16)
        y = None
        for c in range(NCH):
            yc = jnp.dot(h[:, c * FCH:(c + 1) * FCH], wd_s[slot, c], preferred_element_type=f32)
            y = yc if y is None else y + yc
        o_ref[...] = _pack_bf16_pairs(y)

    @pl.when(tab_ref[base + T_ZERO] == 1)
    def _():
        o_ref[...] = jnp.zeros_like(o_ref)


def _experts(layer, schedule, xs, w_gate, w_up, w_down):
    at = lambda s, tab, col: tab[s * T_STRIDE + col]
    up_spec = pl.BlockSpec((1, 1, KCH, D_FF), lambda s, tab: (layer, at(s, tab, T_CEXP), at(s, tab, T_CCH), 0))
    return pl.pallas_call(
        _expert_kernel,
        grid_spec=pltpu.PrefetchScalarGridSpec(
            num_scalar_prefetch=1,
            grid=(NSTEPS,),
            in_specs=[pl.BlockSpec((EBLK, DP), lambda s, tab: (at(s, tab, T_XBLK), 0)),
                      up_spec, up_spec,
                      pl.BlockSpec((1, 1, FCH, D),
                                   lambda s, tab: (layer, at(s, tab, T_CEXP), at(s, tab, T_CCH), 0))],
            out_specs=pl.BlockSpec((EBLK, DP), lambda s, tab: (at(s, tab, T_OBLK), 0)),
            scratch_shapes=[pltpu.VMEM((2, NCH, KCH, D_FF), bf16), pltpu.VMEM((2, NCH, KCH, D_FF), bf16),
                            pltpu.VMEM((2, NCH, FCH, D), bf16)]),
        out_shape=jax.ShapeDtypeStruct((NROWS, DP), jnp.uint32),
        compiler_params=_cparams(("arbitrary",)),
        name="experts",
    )(schedule, xs, w_gate, w_up, w_down)


def _combine_kernel(first, dest_ref, dnext_ref, ys_hbm, x_ref, gates_ref, gate_ref, o_ref, ybuf, sems):
    i = pl.program_id(0)
    slot = jnp.bitwise_and(i, 1)
    stream = jnp.where(i + first == 0, 0, 1)

    def start_tile(idx_ref, s):
        def body(t, c):
            for k in range(2):
                pltpu.make_async_copy(ys_hbm.at[pl.ds(idx_ref[0, 0, k * TM + t], 1)],
                                      ybuf.at[s, k, pl.ds(t, 1)], sems.at[s]).start(priority=k)
            return c
        lax.fori_loop(0, TM, body, 0, unroll=ROW_UNROLL)

    @pl.when(i == 0)
    def _():
        start_tile(dest_ref, 0)

    has_next = i + 1 < pl.num_programs(0)
    for s in range(2):
        pl.when(has_next & (slot == 1 - s))(lambda s=s: start_tile(dnext_ref, s))

    for k in range(2):
        pltpu.make_async_copy(ys_hbm.at[pl.ds(0, TM)], ybuf.at[slot, k], sems.at[slot]).wait()
    y = (_unpack_pairs_f32(ybuf[slot, 0]) * gates_ref[:, 0:1]
         + _unpack_pairs_f32(ybuf[slot, 1]) * gates_ref[:, 1:2])
    o_ref[...] = x_ref[...] + gate_ref[stream] * y


def _combine(first, dest3, ys, x, gates, gate2):
    row = lambda i: (i + first, 0)
    return pl.pallas_call(
        functools.partial(_combine_kernel, first),
        grid=(NT - first,),
        in_specs=[pl.BlockSpec((1, 1, 2 * TM), lambda i: (i + first, 0, 0), memory_space=pltpu.SMEM),
                  pl.BlockSpec((1, 1, 2 * TM), lambda i: (jnp.minimum(i + first + 1, NT - 1), 0, 0),
                               memory_space=pltpu.SMEM),
                  pl.BlockSpec(memory_space=pl.ANY),
                  pl.BlockSpec((TM, D), row),
                  pl.BlockSpec((TM, 2), row),
                  pl.BlockSpec((2, 1, D), lambda i: (0, 0, 0))],
        out_specs=pl.BlockSpec((TM, D), lambda i: (i, 0)),
        out_shape=jax.ShapeDtypeStruct((R - first * TM, D), f32),
        scratch_shapes=[pltpu.VMEM((2, 2, TM, DP), jnp.uint32), pltpu.SemaphoreType.DMA((2,))],
        compiler_params=_cparams(("arbitrary",)),
        name="combine",
    )(dest3, dest3, ys, x, gates, gate2)


def _rope_tables():
    half = HEAD_DIM // 2
    inv_freq = (ROPE_THETA ** (-np.arange(0, half, 2, dtype=np.float32) / half)).astype(np.float32)
    t = np.arange(SEQ)
    pos = np.stack([(t // GRID_W).astype(np.float32), (t % GRID_W).astype(np.float32)], axis=1)
    d = np.arange(LANES) % HEAD_DIM
    axis, sub = d // half, d % half
    ang = (pos[:, axis] * inv_freq[sub % (half // 2)][None, :]).astype(np.float32)
    cos, sin = np.cos(ang).astype(np.float32), np.sin(ang).astype(np.float32)
    first = (sub < half // 2)[None, :]
    sa = np.where(first, -sin, 0.0).astype(np.float32)
    sb = np.where(first, 0.0, sin).astype(np.float32)
    ident = np.ones((CTX, LANES), np.float32)
    zeros = np.zeros((CTX, LANES), np.float32)
    return tuple(jnp.asarray(np.concatenate(parts)) for parts in ((ident, cos), (zeros, sa), (zeros, sb)))


def kernel(x, c, ctx, c_ctx, w_ada, b_ada, g_attn, w_in, q_norm_g, k_norm_g, sink, conv_w, g_out_attn, g_out_conv, w_out, g_ffn, w_router, router_bias, w_exp_gate, w_exp_up, w_exp_down):
    assert x.shape == (1, SEQ, D) and ctx.shape == (1, CTX, D)
    xs_in = (x[0], ctx[0])

    silu = lambda t: t * jax.nn.sigmoid(t)
    s_vec = jnp.stack([silu(c_ctx), silu(c[0])])
    mods = _adaln(jnp.broadcast_to(s_vec[:, :, None], (2, D, LANES)), w_ada, b_ada).reshape(DEPTH, 2, 6, D)

    cos, sa, sb = _rope_tables()
    lane = jnp.arange(LANES)
    ones_bd = (lane[:, None] // HEAD_DIM == lane[None, :] // HEAD_DIM).astype(bf16)
    upper = (lane[:, None] < lane[None, :]).astype(bf16)
    ones = jnp.ones((LANES, LANES), bf16)
    rt = jnp.arange(RT)
    lower = (rt[None, :] < rt[:, None]).astype(bf16)
    wr_pad = jnp.pad(w_router, ((0, 0), (0, LANES - N_EXPERTS)))
    wr_hi = wr_pad.astype(bf16)
    wr = jnp.stack([wr_hi, (wr_pad - wr_hi.astype(f32)).astype(bf16)])

    for l in range(DEPTH):
        mod = mods[l]
        ms1 = jnp.stack([g_attn[l] * (1.0 + mod[:, 1]), mod[:, 0]], axis=1)
        g2 = jnp.stack([jnp.tile(q_norm_g[l] * HEAD_DIM ** -0.5, 2), jnp.tile(k_norm_g[l], 2)])
        q, k, v, cu, b_gate = _inproj(l, xs_in, ms1, w_in, cos, sa, sb, g2, ones_bd)
        attn = _attention(sink[l], q, k, v)
        tab = jnp.stack([mod[:, 2], g_ffn[l] * (1.0 + mod[:, 4]), mod[:, 3]], axis=1)
        gout = jnp.stack([g_out_attn[l], g_out_conv[l]])
        xa, h2, logits = _outproj(l, attn, cu, b_gate, xs_in, tab, gout, conv_w[l], w_out, wr)

        lt = jnp.pad(logits[:N_EXPERTS], ((0, 0), (0, RT * LANES - R))).reshape(N_EXPERTS, RT, LANES)
        _, g_t, d_t, cnt = _router(router_bias, lt, upper, lower, ones)
        gates = g_t.reshape(2, RT * LANES)[:, :R].T
        dest = d_t.reshape(2, RT * LANES)[:, :R]
        dest3 = dest.reshape(2, NT, TM).transpose(1, 0, 2).reshape(NT, 1, 2 * TM)
        counts = cnt[:, 0].astype(i32)
        pend = jnp.cumsum((counts + EBLK - 1) // EBLK * EBLK)

        xs = _dispatch(pend, dest3, h2)
        ys = _experts(l, _expert_schedule(counts), xs, w_exp_gate, w_exp_up, w_exp_down)
        xa = _combine(1 if l == DEPTH - 1 else 0, dest3, ys, xa, gates, mod[:, 5][:, None, :])
        xs_in = (xa,)

    return xa[None]
```
